```python
import jax, jax.numpy as jnp
from jax import lax
import numpy as np

D_MODEL = 2048
BATCH = 4
SEQ = 4096
DEPTH = 1

GRID_W = 64
CTX_LEN = 256
EPS = 1e-6
ATTN_HEADS = 8
ATTN_KV_HEADS = 2
HEAD_DIM = 128
Q_BLOCK = 128
ROPE_BASE = 10000.0
RET_HEADS = 8
RET_DK = 128
RET_DV = 256
RET_CHUNK = 128
PEER_HEADS = 8
PEER_N_KEYS = 128
PEER_N_EXPERTS = PEER_N_KEYS * PEER_N_KEYS
PEER_D_HALF = 128
PEER_TOPK = 16
PEER_BLOCK = 128
ATTN_Q_W = ATTN_HEADS * HEAD_DIM
ATTN_KV_W = ATTN_KV_HEADS * HEAD_DIM
RET_QK_W = RET_HEADS * RET_DK
RET_V_W = RET_HEADS * RET_DV
IN_SPLITS = (ATTN_Q_W, ATTN_KV_W, ATTN_KV_W, RET_QK_W, RET_QK_W, RET_V_W, RET_V_W, D_MODEL, D_MODEL)
IN_WIDTH = ATTN_Q_W + 2 * ATTN_KV_W + 2 * RET_QK_W + 2 * RET_V_W + 2 * D_MODEL

kernel_name = 'hybrid_gqa_retention_peer_block'


def rms_norm(x, w):
    xf = x.astype(jnp.float32)
    y = xf * lax.rsqrt(jnp.mean(xf * xf, axis=-1, keepdims=True) + EPS)
    return (y * w.astype(jnp.float32)).astype(x.dtype)


def modulate(h, shift, scale):
    return h * (1.0 + scale[:, None, :]) + shift[:, None, :]


def axial_rope_tables(length):
    t = jnp.arange(length, dtype=jnp.int32)
    row = (t // GRID_W).astype(jnp.float32)
    col = (t % GRID_W).astype(jnp.float32)
    n_freq = HEAD_DIM // 4
    inv_freq = ROPE_BASE ** (-jnp.arange(n_freq, dtype=jnp.float32) / n_freq)
    ang = jnp.concatenate([row[:, None] * inv_freq, col[:, None] * inv_freq], axis=-1)
    return jnp.cos(ang), jnp.sin(ang)


def apply_rope(x, cos, sin):
    x1, x2 = jnp.split(x, 2, axis=-1)
    cs = cos[None, :, None, :].astype(x.dtype)
    sn = sin[None, :, None, :].astype(x.dtype)
    return jnp.concatenate([x1 * cs - x2 * sn, x2 * cs + x1 * sn], axis=-1)


def split_heads(a, n_heads):
    return a.reshape(a.shape[0], a.shape[1], n_heads, a.shape[2] // n_heads)


def split_projection(p):
    points = np.cumsum(IN_SPLITS)[:-1].tolist()
    return jnp.split(p, points, axis=-1)


def gqa_block(q, k, v):
    b, lq, h, dh = q.shape
    kvh = k.shape[2]
    qg = q.reshape(b, lq, kvh, h // kvh, dh)
    s = jnp.einsum('bqkgd,bskd->bkgqs', qg, k).astype(jnp.float32) * (dh ** -0.5)
    p = jax.nn.softmax(s, axis=-1).astype(v.dtype)
    o = jnp.einsum('bkgqs,bskd->bqkgd', p, v)
    return o.reshape(b, lq, h * dh)


def latent_attention(q, k_all, v_all):
    b, l, h, dh = q.shape
    nb = l // Q_BLOCK
    qb = q.reshape(b, nb, Q_BLOCK, h, dh).transpose(1, 0, 2, 3, 4)
    o = lax.map(lambda qblk: gqa_block(qblk, k_all, v_all), qb)
    return o.transpose(1, 0, 2, 3).reshape(b, l, h * dh)


def retention_chunked(q, k, v, log_gamma, s0):
    b, l, h, _ = q.shape
    dv = v.shape[-1]
    nc = l // RET_CHUNK
    pos = jnp.arange(RET_CHUNK, dtype=jnp.float32)
    diff = pos[:, None] - pos[None, :]
    intra = jnp.where(diff >= 0, jnp.exp(log_gamma[:, None, None] * jnp.maximum(diff, 0.0)), 0.0)
    q_decay = jnp.exp(log_gamma[:, None] * (pos + 1.0)).T
    k_decay = jnp.exp(log_gamma[:, None] * (RET_CHUNK - 1.0 - pos))
    chunk_decay = jnp.exp(log_gamma * RET_CHUNK)

    def to_chunks(a):
        return a.reshape(b, nc, RET_CHUNK, h, a.shape[-1]).transpose(1, 0, 2, 3, 4)

    def step(state, chunk):
        qc, kc, vc = chunk
        s = jnp.einsum('bqhd,bkhd->bhqk', qc, kc) * intra[None]
        inner = jnp.einsum('bhqk,bkhv->bqhv', s, vc)
        cross = jnp.einsum('bqhd,bhdv->bqhv', qc, state) * q_decay[None, :, :, None]
        state = state * chunk_decay[None, :, None, None] + jnp.einsum('bkhd,hk,bkhv->bhdv', kc, k_decay, vc)
        return state, inner + cross

    _, out = lax.scan(step, s0, (to_chunks(q), to_chunks(k), to_chunks(v)))
    return out.transpose(1, 0, 2, 3, 4).reshape(b, l, h, dv)


def retention_final_state(k, v, log_gamma):
    l = k.shape[1]
    w = jnp.exp(log_gamma[:, None] * (l - 1.0 - jnp.arange(l, dtype=jnp.float32)))
    return jnp.einsum('blhd,hl,blhv->bhdv', k, w, v)


def retention_bidir(q, k, v, lg_f, lg_b, s0_f, s0_b):
    o_f = retention_chunked(q, k, v, lg_f, s0_f)
    o_b = retention_chunked(jnp.flip(q, 1), jnp.flip(k, 1), jnp.flip(v, 1), lg_b, s0_b)
    return o_f + jnp.flip(o_b, 1)


def head_norm(o, w):
    mu = jnp.mean(o, axis=-1, keepdims=True)
    var = jnp.mean(jnp.square(o - mu), axis=-1, keepdims=True)
    return (o - mu) * lax.rsqrt(var + EPS) * w.reshape(RET_HEADS, RET_DV).astype(jnp.float32)


def token_mixers(h, hc, cos, sin, w_in, q_norm_w, k_norm_w, ret_decay_fwd, ret_decay_bwd, ret_norm_w,
                 w_attn_branch, w_ret_branch, w_merge_out, update_ctx):
    f32 = jnp.float32
    b, l, _ = h.shape
    qa, ka, va, qr, kr, vr, gr, gate_attn, gate_ret = split_projection(h @ w_in)
    qa_c, ka_c, va_c, qr_c, kr_c, vr_c, gr_c, gate_attn_c, gate_ret_c = split_projection(hc @ w_in)

    q = apply_rope(rms_norm(split_heads(qa, ATTN_HEADS), q_norm_w), cos, sin)
    k = apply_rope(rms_norm(split_heads(ka, ATTN_KV_HEADS), k_norm_w), cos, sin)
    v = split_heads(va, ATTN_KV_HEADS)
    k_c = rms_norm(split_heads(ka_c, ATTN_KV_HEADS), k_norm_w)
    v_c = split_heads(va_c, ATTN_KV_HEADS)
    attn = latent_attention(q, jnp.concatenate([k_c, k], axis=1), jnp.concatenate([v_c, v], axis=1))

    lg_f = -jnp.exp(ret_decay_fwd.astype(f32))
    lg_b = -jnp.exp(ret_decay_bwd.astype(f32))
    rq = apply_rope(split_heads(qr, RET_HEADS), cos, sin).astype(f32)
    rk = apply_rope(split_heads(kr, RET_HEADS), cos, sin).astype(f32) * (RET_DK ** -0.5)
    rv = split_heads(vr, RET_HEADS).astype(f32)
    rk_c = split_heads(kr_c, RET_HEADS).astype(f32) * (RET_DK ** -0.5)
    rv_c = split_heads(vr_c, RET_HEADS).astype(f32)
    s_f = retention_final_state(rk_c, rv_c, lg_f)
    s_b = retention_final_state(jnp.flip(rk_c, 1), jnp.flip(rv_c, 1), lg_b)
    ret = retention_bidir(rq, rk, rv, lg_f, lg_b, s_f, s_b)
    ret = jax.nn.silu(gr) * head_norm(ret, ret_norm_w).reshape(b, l, RET_V_W).astype(h.dtype)

    def merge(attn_o, ret_o, g_attn, g_ret):
        mixed = jax.nn.sigmoid(g_attn) * (attn_o @ w_attn_branch) + jax.nn.sigmoid(g_ret) * (ret_o @ w_ret_branch)
        return mixed @ w_merge_out

    y = merge(attn, ret, gate_attn, gate_ret)
    if not update_ctx:
        return y, None

    lc = hc.shape[1]
    q_c = rms_norm(split_heads(qa_c, ATTN_HEADS), q_norm_w)
    attn_c = gqa_block(q_c, k_c, v_c)
    zeros = jnp.zeros_like(s_f)
    ret_c = retention_bidir(split_heads(qr_c, RET_HEADS).astype(f32), rk_c, rv_c, lg_f, lg_b, zeros, zeros)
    ret_c = jax.nn.silu(gr_c) * head_norm(ret_c, ret_norm_w).reshape(b, lc, RET_V_W).astype(hc.dtype)
    return y, merge(attn_c, ret_c, gate_attn_c, gate_ret_c)


def peer_ffn(h, w_q, sub_keys, u, v):
    b, l, d = h.shape
    nb = (b * l) // PEER_BLOCK
    tokens = h.reshape(nb, PEER_BLOCK, d)

    def one_block(xb):
        q = (xb @ w_q).reshape(PEER_BLOCK, PEER_HEADS, 2, PEER_D_HALF)
        s = jnp.einsum('thpd,hpnd->thpn', q, sub_keys).astype(jnp.float32)
        s1, i1 = lax.top_k(s[:, :, 0], PEER_TOPK)
        s2, i2 = lax.top_k(s[:, :, 1], PEER_TOPK)
        cand_s = (s1[..., :, None] + s2[..., None, :]).reshape(PEER_BLOCK, PEER_HEADS, PEER_TOPK * PEER_TOPK)
        cand_i = (i1[..., :, None] * PEER_N_KEYS + i2[..., None, :]).reshape(PEER_BLOCK, PEER_HEADS, PEER_TOPK * PEER_TOPK)
        top_s, pos = lax.top_k(cand_s, PEER_TOPK)
        expert = jnp.take_along_axis(cand_i, pos, axis=-1)
        g = jax.nn.softmax(top_s, axis=-1)
        a = jnp.einsum('td,thkd->thk', xb, u[expert])
        act = (jax.nn.gelu(a.astype(jnp.float32)) * g).astype(xb.dtype)
        return jnp.einsum('thk,thkd->td', act, v[expert])

    return lax.map(one_block, tokens).reshape(b, l, d)


def setup_inputs(seed: int = 0) -> dict:
    key = jax.random.key(seed)
    ks = jax.random.split(key, 22)
    f32 = jnp.float32

    def nrm(k, shape, scale):
        return jax.random.normal(k, shape, f32) * scale

    def gain(k, shape):
        return 1.0 + 0.02 * jax.random.normal(k, shape, f32)

    gamma = 1.0 - 2.0 ** (-5.0 - jnp.arange(RET_HEADS, dtype=f32))
    z0 = jnp.log(-jnp.log(gamma))
    return {
        'x': nrm(ks[0], (BATCH, SEQ, D_MODEL), 1.0),
        'c': nrm(ks[1], (BATCH, D_MODEL), 1.0),
        'ctx': nrm(ks[2], (BATCH, CTX_LEN, D_MODEL), 1.0),
        'c_ctx': nrm(ks[3], (D_MODEL,), 1.0),
        'ada_w': nrm(ks[4], (DEPTH, D_MODEL, 6 * D_MODEL), 0.5 * D_MODEL ** -0.5),
        'ada_b': nrm(ks[5], (DEPTH, 6 * D_MODEL), 0.02),
        'norm1_w': gain(ks[6], (DEPTH, D_MODEL)),
        'w_in': nrm(ks[7], (DEPTH, D_MODEL, IN_WIDTH), D_MODEL ** -0.5),
        'q_norm_w': gain(ks[8], (DEPTH, HEAD_DIM)),
        'k_norm_w': gain(ks[9], (DEPTH, HEAD_DIM)),
        'ret_decay_fwd': z0 + 0.05 * jax.random.normal(ks[10], (DEPTH, RET_HEADS), f32),
        'ret_decay_bwd': z0 + 0.05 * jax.random.normal(ks[11], (DEPTH, RET_HEADS), f32),
        'ret_norm_w': gain(ks[12], (DEPTH, RET_V_W)),
        'w_attn_branch': nrm(ks[13], (DEPTH, ATTN_Q_W, D_MODEL), ATTN_Q_W ** -0.5),
        'w_ret_branch': nrm(ks[14], (DEPTH, RET_V_W, D_MODEL), RET_V_W ** -0.5),
        'w_merge_out': nrm(ks[15], (DEPTH, D_MODEL, D_MODEL), D_MODEL ** -0.5),
        'norm2_w': gain(ks[16], (DEPTH, D_MODEL)),
        'peer_w_q': nrm(ks[17], (DEPTH, D_MODEL, PEER_HEADS * 2 * PEER_D_HALF), D_MODEL ** -0.5),
        'peer_keys': nrm(ks[18], (DEPTH, PEER_HEADS, 2, PEER_N_KEYS, PEER_D_HALF), PEER_D_HALF ** -0.5),
        'peer_u': nrm(ks[19], (DEPTH, PEER_N_EXPERTS, D_MODEL), D_MODEL ** -0.5),
        'peer_v': nrm(ks[20], (DEPTH, PEER_N_EXPERTS, D_MODEL), 1.0),
        'norm_f_w': gain(ks[21], (D_MODEL,)),
    }


def reference(x, c, ctx, c_ctx, ada_w, ada_b, norm1_w, w_in, q_norm_w, k_norm_w, ret_decay_fwd, ret_decay_bwd,
              ret_norm_w, w_attn_branch, w_ret_branch, w_merge_out, norm2_w, peer_w_q, peer_keys, peer_u, peer_v,
              norm_f_w):
    cos, sin = axial_rope_tables(x.shape[1])
    xc = ctx
    for layer in range(DEPTH):
        update_ctx = layer + 1 < DEPTH
        mod_x = jnp.split(jax.nn.silu(c) @ ada_w[layer] + ada_b[layer], 6, axis=-1)
        mod_c = jnp.split(jax.nn.silu(c_ctx)[None, :] @ ada_w[layer] + ada_b[layer], 6, axis=-1)

        h = modulate(rms_norm(x, norm1_w[layer]), mod_x[0], mod_x[1])
        hc = modulate(rms_norm(xc, norm1_w[layer]), mod_c[0], mod_c[1])
        y, yc = token_mixers(h, hc, cos, sin, w_in[layer], q_norm_w[layer], k_norm_w[layer],
                             ret_decay_fwd[layer], ret_decay_bwd[layer], ret_norm_w[layer],
                             w_attn_branch[layer], w_ret_branch[layer], w_merge_out[layer], update_ctx)
        x = x + mod_x[2][:, None, :] * y

        h2 = modulate(rms_norm(x, norm2_w[layer]), mod_x[3], mod_x[4])
        x = x + mod_x[5][:, None, :] * peer_ffn(h2, peer_w_q[layer], peer_keys[layer], peer_u[layer], peer_v[layer])

        if update_ctx:
            xc = xc + mod_c[2][:, None, :] * yc
            hc2 = modulate(rms_norm(xc, norm2_w[layer]), mod_c[3], mod_c[4])
            xc = xc + mod_c[5][:, None, :] * peer_ffn(hc2, peer_w_q[layer], peer_keys[layer], peer_u[layer], peer_v[layer])
    return rms_norm(x, norm_f_w)
```

```python
import functools

import jax
import jax.numpy as jnp
from jax import lax
from jax.experimental import pallas as pl
from jax.experimental.pallas import tpu as pltpu

F32 = jnp.float32
BF16 = jnp.bfloat16
I32 = jnp.int32

EPS = 1e-6
GRID_W = 64
ROPE_BASE = 10000.0
ATTN_HEADS = 8
ATTN_KV_HEADS = 2
HEAD_DIM = 128
RET_HEADS = 8
RET_DK = 128
RET_DV = 256
RET_CHUNK = 128
PEER_HEADS = 8
PEER_N_KEYS = 128
PEER_D_HALF = 128
PEER_TOPK = 16

LANES = 128
SUBLANES = 8
VMEM_LIMIT = 56 * 1024 * 1024

ATTN_Q_W = ATTN_HEADS * HEAD_DIM
ATTN_KV_W = ATTN_KV_HEADS * HEAD_DIM
RET_QK_W = RET_HEADS * RET_DK
RET_V_W = RET_HEADS * RET_DV
D_MODEL = 2048
_SRC_ORDER = (("qa", ATTN_Q_W), ("ka", ATTN_KV_W), ("va", ATTN_KV_W), ("qr", RET_QK_W), ("kr", RET_QK_W),
              ("vr", RET_V_W), ("gr", RET_V_W), ("ga", D_MODEL), ("gt", D_MODEL))
_DST_ORDER = ("ga", "gt", "vr", "gr", "qr", "kr", "qa", "ka", "va")


def _offsets(order, widths):
    off, o = {}, 0
    for name in order:
        off[name] = o
        o += widths[name]
    return off


_WIDTH = dict(_SRC_ORDER)
_SRC_OFF = _offsets([n for n, _ in _SRC_ORDER], _WIDTH)
OFF = _offsets(_DST_ORDER, _WIDTH)


def _regroup_w_in(w_in):
    return jnp.concatenate([w_in[:, _SRC_OFF[n]:_SRC_OFF[n] + _WIDTH[n]] for n in _DST_ORDER], axis=1)


def _cparams(*sem):
    return pltpu.CompilerParams(dimension_semantics=sem, vmem_limit_bytes=VMEM_LIMIT)


def _sigmoid(x):
    return 1.0 / (1.0 + jnp.exp(-x))


def _silu(x):
    return x * _sigmoid(x)


def _rot_half(x):
    return pltpu.roll(x, HEAD_DIM // 2, 1)


def _dot_nt(a, b):
    return lax.dot_general(a, b, (((1,), (1,)), ((), ())), preferred_element_type=F32)


def _dot_tn(a, b):
    return lax.dot_general(a, b, (((0,), (0,)), ((), ())), preferred_element_type=F32)


def _ada_kernel(c_ref, w_ref, b_ref, o_ref):
    sc = _silu(c_ref[...]).astype(BF16)
    o_ref[...] = jnp.dot(sc, w_ref[...].astype(BF16), preferred_element_type=F32) + b_ref[...]


def _ada(c_rows, ada_w, ada_b):
    rows, d = c_rows.shape
    n = ada_w.shape[1]
    tn = 1536
    return pl.pallas_call(
        _ada_kernel,
        grid=(n // tn,),
        in_specs=[pl.BlockSpec((rows, d), lambda j: (0, 0)),
                  pl.BlockSpec((d, tn), lambda j: (0, j)),
                  pl.BlockSpec((1, tn), lambda j: (0, j))],
        out_specs=pl.BlockSpec((rows, tn), lambda j: (0, j)),
        out_shape=jax.ShapeDtypeStruct((rows, n), F32),
        compiler_params=_cparams("arbitrary"),
        name="ada",
    )(c_rows, ada_w, ada_b.reshape(1, n))


def _in_proj_kernel(x_ref, nw_ref, shift_ref, scale_ref, w_ref, o_ref, h_ref):
    @pl.when(pl.program_id(2) == 0)
    def _():
        x = x_ref[...]
        y = x * lax.rsqrt(jnp.mean(x * x, axis=-1, keepdims=True) + EPS) * nw_ref[...]
        h_ref[...] = (y * (1.0 + scale_ref[...]) + shift_ref[...]).astype(BF16)

    o_ref[...] = jnp.dot(h_ref[...], w_ref[...], preferred_element_type=F32).astype(o_ref.dtype)


def _in_proj(x, norm_w, shift, scale, w, tm, tn):
    b, l, d = x.shape
    n = w.shape[1]
    per_batch = shift.shape[0] > 1
    mod_map = (lambda bi, i, j: (bi, 0, 0)) if per_batch else (lambda bi, i, j: (0, 0, 0))
    return pl.pallas_call(
        _in_proj_kernel,
        grid=(b, l // tm, n // tn),
        in_specs=[pl.BlockSpec((None, tm, d), lambda bi, i, j: (bi, i, 0)),
                  pl.BlockSpec((1, d), lambda bi, i, j: (0, 0)),
                  pl.BlockSpec((None, 1, d), mod_map),
                  pl.BlockSpec((None, 1, d), mod_map),
                  pl.BlockSpec((d, tn), lambda bi, i, j: (0, j))],
        out_specs=pl.BlockSpec((None, tm, tn), lambda bi, i, j: (bi, i, j)),
        out_shape=jax.ShapeDtypeStruct((b, l, n), BF16),
        scratch_shapes=[pltpu.VMEM((tm, d), BF16)],
        compiler_params=_cparams("parallel", "parallel", "arbitrary"),
        name="in_proj",
    )(x, norm_w.reshape(1, d), shift, scale, w)


def _head_rms(x, w):
    return x * lax.rsqrt(jnp.mean(x * x, axis=-1, keepdims=True) + EPS) * w


def _attn_kernel(q_ref, kc_ref, k_ref, vc_ref, v_ref, cq_ref, sq_ref, ck_ref, sk_ref, qnw_ref, knw_ref,
                 o_ref, k_s, v_s, *, lc, group):
    @pl.when(pl.program_id(2) == 0)
    def _():
        knw = knw_ref[...]
        k_s[0:lc, :] = _head_rms(kc_ref[...].astype(F32), knw).astype(BF16)
        kn = _head_rms(k_ref[...].astype(F32), knw)
        k_s[lc:, :] = (kn * ck_ref[...] + _rot_half(kn) * sk_ref[...]).astype(BF16)
        v_s[0:lc, :] = vc_ref[...]
        v_s[lc:, :] = v_ref[...]

    c = (HEAD_DIM ** -0.5) * 1.4426950408889634
    qnw = qnw_ref[...]
    cq = cq_ref[...]
    sq = sq_ref[...]
    for g in range(group):
        sl = slice(g * HEAD_DIM, (g + 1) * HEAD_DIM)
        qn = _head_rms(q_ref[:, sl].astype(F32), qnw)
        qr = (qn * cq + _rot_half(qn) * sq).astype(BF16)
        s = _dot_nt(qr, k_s[...])
        m = jnp.max(s, axis=-1, keepdims=True)
        p = jnp.exp2((s - m) * c)
        l = jnp.sum(p, axis=-1, keepdims=True)
        o = jnp.dot(p.astype(BF16), v_s[...], preferred_element_type=F32)
        o_ref[:, sl] = (o * (1.0 / l)).astype(o_ref.dtype)


def _attention(p, pc, rope_c, rope_s, q_norm_w, k_norm_w, tq):
    b, l, _ = p.shape
    lc = pc.shape[1]
    group = ATTN_HEADS // ATTN_KV_HEADS
    gw = group * HEAD_DIM
    kq0 = OFF["qa"] // gw
    kk0 = OFF["ka"] // HEAD_DIM
    kv0 = OFF["va"] // HEAD_DIM
    full = lambda bi, h, i: (0, 0)
    return pl.pallas_call(
        functools.partial(_attn_kernel, lc=lc, group=group),
        grid=(b, ATTN_KV_HEADS, l // tq),
        in_specs=[pl.BlockSpec((None, tq, gw), lambda bi, h, i: (bi, i, kq0 + h)),
                  pl.BlockSpec((None, lc, HEAD_DIM), lambda bi, h, i: (bi, 0, kk0 + h)),
                  pl.BlockSpec((None, l, HEAD_DIM), lambda bi, h, i: (bi, 0, kk0 + h)),
                  pl.BlockSpec((None, lc, HEAD_DIM), lambda bi, h, i: (bi, 0, kv0 + h)),
                  pl.BlockSpec((None, l, HEAD_DIM), lambda bi, h, i: (bi, 0, kv0 + h)),
                  pl.BlockSpec((tq, HEAD_DIM), lambda bi, h, i: (i, 0)),
                  pl.BlockSpec((tq, HEAD_DIM), lambda bi, h, i: (i, 0)),
                  pl.BlockSpec((l, HEAD_DIM), full),
                  pl.BlockSpec((l, HEAD_DIM), full),
                  pl.BlockSpec((1, HEAD_DIM), full),
                  pl.BlockSpec((1, HEAD_DIM), full)],
        out_specs=pl.BlockSpec((None, tq, gw), lambda bi, h, i: (bi, i, h)),
        out_shape=jax.ShapeDtypeStruct((b, l, ATTN_Q_W), BF16),
        scratch_shapes=[pltpu.VMEM((lc + l, HEAD_DIM), BF16), pltpu.VMEM((lc + l, HEAD_DIM), BF16)],
        compiler_params=_cparams("parallel", "parallel", "arbitrary"),
        name="attention",
    )(p, pc, p, pc, p, rope_c, rope_s, rope_c, rope_s, q_norm_w.reshape(1, HEAD_DIM), k_norm_w.reshape(1, HEAD_DIM))


def _ret_kernel(q_ref, k_ref, v_ref, g_ref, kc_ref, vc_ref, cos_ref, sin_ref, z_ref, nw_ref, o_ref,
                q_s, k_s, acc, *, nc, lc):
    ch = RET_CHUNK
    scale = RET_DK ** -0.5
    lg_f = -jnp.exp(z_ref[0])
    lg_b = -jnp.exp(z_ref[1])
    row = lax.broadcasted_iota(I32, (ch, ch), 0).astype(F32)
    col = lax.broadcasted_iota(I32, (ch, ch), 1).astype(F32)
    d = row - col
    intra_f = jnp.where(d >= 0, jnp.exp(lg_f * jnp.maximum(d, 0.0)), 0.0)
    intra_b = jnp.where(d <= 0, jnp.exp(lg_b * jnp.maximum(-d, 0.0)), 0.0)

    def wide(a):
        return jnp.concatenate([a] * (RET_DV // LANES), axis=1)

    qd_f = wide(jnp.exp(lg_f * (row + 1.0)))
    kd_f = jnp.exp(lg_f * (ch - 1.0 - row))
    cd_f = wide(jnp.exp(lg_f * float(ch)))
    qd_b = wide(jnp.exp(lg_b * (ch - row)))
    kd_b = jnp.exp(lg_b * row)
    cd_b = wide(jnp.exp(lg_b * float(ch)))

    j = lax.broadcasted_iota(I32, (lc, RET_DK), 0).astype(F32)
    kc = kc_ref[...].astype(F32) * scale
    vc = vc_ref[...]
    s_f = _dot_tn((kc * jnp.exp(lg_f * (lc - 1.0 - j))).astype(BF16), vc)
    s_b = _dot_tn((kc * jnp.exp(lg_b * j)).astype(BF16), vc)

    def fwd(c, state):
        r0 = pl.multiple_of(c * ch, ch)
        cs = cos_ref[pl.ds(r0, ch), :]
        sn = sin_ref[pl.ds(r0, ch), :]
        qf = q_ref[pl.ds(r0, ch), :].astype(F32)
        kf = k_ref[pl.ds(r0, ch), :].astype(F32)
        qb = (qf * cs + _rot_half(qf) * sn).astype(BF16)
        kr = (kf * cs + _rot_half(kf) * sn) * scale
        q_s[pl.ds(r0, ch), :] = qb
        k_s[pl.ds(r0, ch), :] = kr
        vb = v_ref[pl.ds(r0, ch), :]
        s = _dot_nt(qb, kr.astype(BF16)) * intra_f
        inner = jnp.dot(s.astype(BF16), vb, preferred_element_type=F32)
        cross = jnp.dot(qb, state.astype(BF16), preferred_element_type=F32) * qd_f
        acc[pl.ds(r0, ch), :] = inner + cross
        return state * cd_f + _dot_tn((kr * kd_f).astype(BF16), vb)

    lax.fori_loop(0, nc, fwd, s_f)

    nw = nw_ref[...]

    def bwd(i, state):
        r0 = pl.multiple_of((nc - 1 - i) * ch, ch)
        qb = q_s[pl.ds(r0, ch), :]
        kr = k_s[pl.ds(r0, ch), :]
        vb = v_ref[pl.ds(r0, ch), :]
        s = _dot_nt(qb, kr.astype(BF16)) * intra_b
        inner = jnp.dot(s.astype(BF16), vb, preferred_element_type=F32)
        cross = jnp.dot(qb, state.astype(BF16), preferred_element_type=F32) * qd_b
        o = acc[pl.ds(r0, ch), :] + inner + cross
        mu = jnp.mean(o, axis=-1, keepdims=True)
        oc = o - mu
        var = jnp.mean(oc * oc, axis=-1, keepdims=True)
        hn = oc * lax.rsqrt(var + EPS) * nw
        gate = g_ref[pl.ds(r0, ch), :].astype(F32)
        o_ref[pl.ds(r0, ch), :] = (_silu(gate) * hn).astype(o_ref.dtype)
        return state * cd_b + _dot_tn((kr * kd_b).astype(BF16), vb)

    lax.fori_loop(0, nc, bwd, s_b)


def _retention(p, pc, rope_c, rope_s, decay_fwd, decay_bwd, ret_norm_w):
    b, l, _ = p.shape
    lc = pc.shape[1]
    nc = l // RET_CHUNK
    z = jnp.broadcast_to(jnp.stack([decay_fwd, decay_bwd])[:, :, None, None], (2, RET_HEADS, 1, LANES))
    kq0, kk0 = OFF["qr"] // RET_DK, OFF["kr"] // RET_DK
    kv0, kg0 = OFF["vr"] // RET_DV, OFF["gr"] // RET_DV
    full = lambda bi, h: (0, 0)
    return pl.pallas_call(
        functools.partial(_ret_kernel, nc=nc, lc=lc),
        grid=(b, RET_HEADS),
        in_specs=[pl.BlockSpec((None, l, RET_DK), lambda bi, h: (bi, 0, kq0 + h)),
                  pl.BlockSpec((None, l, RET_DK), lambda bi, h: (bi, 0, kk0 + h)),
                  pl.BlockSpec((None, l, RET_DV), lambda bi, h: (bi, 0, kv0 + h)),
                  pl.BlockSpec((None, l, RET_DV), lambda bi, h: (bi, 0, kg0 + h)),
                  pl.BlockSpec((None, lc, RET_DK), lambda bi, h: (bi, 0, kk0 + h)),
                  pl.BlockSpec((None, lc, RET_DV), lambda bi, h: (bi, 0, kv0 + h)),
                  pl.BlockSpec((l, RET_DK), full),
                  pl.BlockSpec((l, RET_DK), full),
                  pl.BlockSpec((2, None, 1, LANES), lambda bi, h: (0, h, 0, 0)),
                  pl.BlockSpec((1, RET_DV), lambda bi, h: (0, h))],
        out_specs=pl.BlockSpec((None, l, RET_DV), lambda bi, h: (bi, 0, h)),
        out_shape=jax.ShapeDtypeStruct((b, l, RET_V_W), BF16),
        scratch_shapes=[pltpu.VMEM((l, RET_DK), BF16), pltpu.VMEM((l, RET_DK), F32), pltpu.VMEM((l, RET_DV), F32)],
        compiler_params=_cparams("parallel", "parallel"),
        name="retention",
    )(p, p, p, p, pc, pc, rope_c, rope_s, z, ret_norm_w.reshape(1, RET_V_W))


def _merge_kernel(attn_ref, ret_ref, ga_ref, gr_ref, x_ref, gate_ref, shift_ref, scale_ref, nw_ref,
                  wa_ref, wr_ref, wm_ref, x1_ref, h2_ref):
    a = jnp.dot(attn_ref[...], wa_ref[...], preferred_element_type=F32)
    r = jnp.dot(ret_ref[...], wr_ref[...], preferred_element_type=F32)
    mixed = _sigmoid(ga_ref[...].astype(F32)) * a + _sigmoid(gr_ref[...].astype(F32)) * r
    y = jnp.dot(mixed.astype(BF16), wm_ref[...], preferred_element_type=F32)
    x1 = x_ref[...] + gate_ref[...] * y
    x1_ref[...] = x1
    hn = x1 * lax.rsqrt(jnp.mean(x1 * x1, axis=-1, keepdims=True) + EPS) * nw_ref[...]
    h2_ref[...] = (hn * (1.0 + scale_ref[...]) + shift_ref[...]).astype(BF16)


def _merge(attn, ret, p, x, gate, shift2, scale2, norm2_w, wa, wr, wm, tm):
    b, l, d = x.shape
    assert d == D_MODEL
    kga, kgt = OFF["ga"] // d, OFF["gt"] // d
    row = lambda bi, i: (bi, i, 0)
    mod = lambda bi, i: (bi, 0, 0)
    const = lambda bi, i: (0, 0)
    once = pl.Buffered(1)
    return pl.pallas_call(
        _merge_kernel,
        grid=(b, l // tm),
        in_specs=[pl.BlockSpec((None, tm, ATTN_Q_W), row),
                  pl.BlockSpec((None, tm, RET_V_W), row),
                  pl.BlockSpec((None, tm, d), lambda bi, i: (bi, i, kga)),
                  pl.BlockSpec((None, tm, d), lambda bi, i: (bi, i, kgt)),
                  pl.BlockSpec((None, tm, d), row),
                  pl.BlockSpec((None, 1, d), mod),
                  pl.BlockSpec((None, 1, d), mod),
                  pl.BlockSpec((None, 1, d), mod),
                  pl.BlockSpec((1, d), const),
                  pl.BlockSpec(wa.shape, const, pipeline_mode=once),
                  pl.BlockSpec(wr.shape, const, pipeline_mode=once),
                  pl.BlockSpec(wm.shape, const, pipeline_mode=once)],
        out_specs=[pl.BlockSpec((None, tm, d), row), pl.BlockSpec((None, tm, d), row)],
        out_shape=[jax.ShapeDtypeStruct((b, l, d), F32), jax.ShapeDtypeStruct((b, l, d), BF16)],
        compiler_params=_cparams("parallel", "parallel"),
        name="merge",
    )(attn, ret, p, p, x, gate, shift2, scale2, norm2_w.reshape(1, d), wa, wr, wm)


def _top_rows(s, k):
    n = s.shape[0]
    rid = lax.broadcasted_iota(I32, s.shape, 0)
    vals, ids = [], []
    for _ in range(k):
        m = jnp.max(s, axis=0, keepdims=True)
        sel = jnp.min(jnp.where(s == m, rid, n), axis=0, keepdims=True)
        vals.append(m)
        ids.append(sel)
        s = jnp.where(rid == sel, -jnp.inf, s)
    return jnp.concatenate(vals, axis=0), jnp.concatenate(ids, axis=0)


def _pair_candidates(s1, s2):
    k = PEER_TOPK
    t = s1.shape[1]
    r8 = lax.broadcasted_iota(I32, (SUBLANES, t), 0)
    r16 = lax.broadcasted_iota(I32, (k, t), 0)
    neg = -jnp.inf
    sums = [s1[0:1] + s2]
    flat = [r16]
    for a, nb in ((1, 8), (2, 5), (3, 4)):
        sums.append(jnp.where(r8 < nb, s1[a:a + 1] + s2[0:8], neg))
        flat.append(a * k + r8)
    sums.append(s1[8:16] + s2[0:1])
    flat.append((r8 + 8) * k)
    for b_, lo, hi in ((0, 4, 8), (1, 4, 8), (2, 4, 5)):
        sums.append(jnp.where((r8 >= lo) & (r8 < hi), s1[0:8] + s2[b_:b_ + 1], neg))
        flat.append(r8 * k + b_)
    return jnp.concatenate(sums, axis=0), jnp.concatenate(flat, axis=0)


def _peer_topk_kernel(h_ref, wq_ref, keys_ref, i1_ref, i2_ref, g_ref, q_s, i1_s, i2_s, g_s):
    k = PEER_TOPK
    q_s[...] = jnp.dot(h_ref[...], wq_ref[...], preferred_element_type=F32).astype(BF16)
    t = q_s.shape[0]
    big = k * k
    r16 = lax.broadcasted_iota(I32, (k, t), 0)

    def head(h, carry):
        c0 = pl.multiple_of(h * 2 * PEER_D_HALF, 2 * PEER_D_HALF)
        qh = q_s[:, pl.ds(c0, 2 * PEER_D_HALF)]
        s1, n1 = _top_rows(_dot_nt(keys_ref[2 * h], qh[:, :PEER_D_HALF]), k)
        s2, n2 = _top_rows(_dot_nt(keys_ref[2 * h + 1], qh[:, PEER_D_HALF:]), k)
        cand, flat = _pair_candidates(s1, s2)
        tops, e1, e2 = [], [], []
        for _ in range(k):
            m = jnp.max(cand, axis=0, keepdims=True)
            f = jnp.min(jnp.where(cand == m, flat, big), axis=0, keepdims=True)
            cand = jnp.where(flat == f, -jnp.inf, cand)
            a = f // k
            b_ = f - a * k
            tops.append(m)
            e1.append(jnp.sum(jnp.where(r16 == a, n1, 0), axis=0, keepdims=True))
            e2.append(jnp.sum(jnp.where(r16 == b_, n2, 0), axis=0, keepdims=True))
        top = jnp.concatenate(tops, axis=0)
        ex = jnp.exp(top - top[0:1])
        r0 = pl.multiple_of(h * k, k)
        g_s[pl.ds(r0, k), :] = ex / jnp.sum(ex, axis=0, keepdims=True)
        i1_s[pl.ds(r0, k), :] = jnp.concatenate(e1, axis=0)
        i2_s[pl.ds(r0, k), :] = jnp.concatenate(e2, axis=0)
        return carry

    lax.fori_loop(0, PEER_HEADS, head, 0)
    i1_ref[...] = i1_s[...].T
    i2_ref[...] = i2_s[...].T
    g_ref[...] = g_s[...].T


def _peer_topk(h2, wq, keys, tm):
    n, d = h2.shape
    slots = PEER_HEADS * PEER_TOPK
    const2 = lambda i: (0, 0)
    row = lambda i: (i, 0)
    out = jax.ShapeDtypeStruct
    return pl.pallas_call(
        _peer_topk_kernel,
        grid=(n // tm,),
        in_specs=[pl.BlockSpec((tm, d), row),
                  pl.BlockSpec(wq.shape, const2, pipeline_mode=pl.Buffered(1)),
                  pl.BlockSpec(keys.shape, lambda i: (0, 0, 0))],
        out_specs=[pl.BlockSpec((tm, slots), row)] * 3,
        out_shape=[out((n, slots), I32), out((n, slots), I32), out((n, slots), F32)],
        scratch_shapes=[pltpu.VMEM((tm, wq.shape[1]), BF16), pltpu.VMEM((slots, tm), I32),
                        pltpu.VMEM((slots, tm), I32), pltpu.VMEM((slots, tm), F32)],
        compiler_params=_cparams("parallel"),
        name="peer_topk",
    )(h2, wq, keys)


def _peer_w_kernel(i1_ref, i2_ref, g_ref, w_ref, w3):
    nk = PEER_N_KEYS
    tb = i1_ref.shape[0]
    rid = lax.broadcasted_iota(I32, (nk, i1_ref.shape[1]), 0)

    def token(t, carry):
        i1 = i1_ref[pl.ds(t, 1), :]
        i2 = i2_ref[pl.ds(t, 1), :]
        g = g_ref[pl.ds(t, 1), :]
        left = jnp.where(rid == i1, g, 0.0).astype(BF16)
        right = jnp.where(rid == i2, 1.0, 0.0).astype(BF16)
        w3[pl.ds(pl.multiple_of(t * nk, nk), nk), :] = _dot_nt(left, right)
        return carry

    lax.fori_loop(0, tb, token, 0, unroll=8)
    for r in range(nk):
        w_ref[:, r * nk:(r + 1) * nk] = w3[pl.ds(r, tb, stride=nk), :].astype(w_ref.dtype)


def _peer_w(i1, i2, g, tb):
    n, slots = i1.shape
    ne = PEER_N_KEYS * PEER_N_KEYS
    row = lambda i: (i, 0)
    return pl.pallas_call(
        _peer_w_kernel,
        grid=(n // tb,),
        in_specs=[pl.BlockSpec((tb, slots), row)] * 3,
        out_specs=pl.BlockSpec((tb, ne), row),
        out_shape=jax.ShapeDtypeStruct((n, ne), BF16),
        scratch_shapes=[pltpu.VMEM((tb * PEER_N_KEYS, PEER_N_KEYS), F32)],
        compiler_params=_cparams("parallel"),
        name="peer_w",
    )(i1, i2, g)


def _gelu_tanh(x):
    return 0.5 * x * (1.0 + jnp.tanh(0.7978845608028654 * (x + 0.044715 * (x * x * x))))


def _peer_dense_kernel(h_ref, u_ref, v_ref, w_ref, x_ref, gate_ref, nw_ref, o_ref, acc):
    j = pl.program_id(1)

    @pl.when(j == 0)
    def _():
        acc[...] = jnp.zeros_like(acc)

    a = _dot_nt(h_ref[...], u_ref[...])
    act = (_gelu_tanh(a) * w_ref[...].astype(F32)).astype(BF16)
    acc[...] += jnp.dot(act, v_ref[...], preferred_element_type=F32)

    @pl.when(j == pl.num_programs(1) - 1)
    def _():
        x2 = x_ref[...] + gate_ref[...] * acc[...]
        o_ref[...] = x2 * lax.rsqrt(jnp.mean(x2 * x2, axis=-1, keepdims=True) + EPS) * nw_ref[...]


def _peer_dense(h2, u, v, w, x1, gate, norm_f_w, seq, tm, te):
    n, d = h2.shape
    ne = u.shape[0]
    tiles_per_batch = seq // tm
    row = lambda i, j: (i, 0)
    return pl.pallas_call(
        _peer_dense_kernel,
        grid=(n // tm, ne // te),
        in_specs=[pl.BlockSpec((tm, d), row),
                  pl.BlockSpec((te, d), lambda i, j: (j, 0)),
                  pl.BlockSpec((te, d), lambda i, j: (j, 0)),
                  pl.BlockSpec((tm, te), lambda i, j: (i, j)),
                  pl.BlockSpec((tm, d), row),
                  pl.BlockSpec((None, 1, d), lambda i, j: (i // tiles_per_batch, 0, 0)),
                  pl.BlockSpec((1, d), lambda i, j: (0, 0))],
        out_specs=pl.BlockSpec((tm, d), row),
        out_shape=jax.ShapeDtypeStruct((n, d), F32),
        scratch_shapes=[pltpu.VMEM((tm, d), F32)],
        compiler_params=_cparams("parallel", "arbitrary"),
        name="peer_dense",
    )(h2, u, v, w, x1, gate, norm_f_w.reshape(1, d))


def _rope_tables(length):
    t = jnp.arange(length, dtype=jnp.int32)
    row = (t // GRID_W).astype(F32)
    col = (t % GRID_W).astype(F32)
    n_freq = HEAD_DIM // 4
    inv_freq = ROPE_BASE ** (-jnp.arange(n_freq, dtype=F32) / n_freq)
    ang = jnp.concatenate([row[:, None] * inv_freq, col[:, None] * inv_freq], axis=-1)
    cos, sin = jnp.cos(ang), jnp.sin(ang)
    return jnp.concatenate([cos, cos], axis=-1), jnp.concatenate([-sin, sin], axis=-1)


def _layer(x, ctx, c_rows, rope_c, rope_s, ada_w, ada_b, norm1_w, w_in, q_norm_w, k_norm_w, ret_decay_fwd,
           ret_decay_bwd, ret_norm_w, w_attn_branch, w_ret_branch, w_merge_out, norm2_w, peer_w_q, peer_keys,
           peer_u, peer_v, norm_f_w):
    b, l, d = x.shape
    lc = ctx.shape[1]
    mod = _ada(c_rows, ada_w, ada_b)
    mod_x = [mod[:b, i * d:(i + 1) * d].reshape(b, 1, d) for i in range(6)]
    mod_c = [mod[b:b + 1, i * d:(i + 1) * d].reshape(1, 1, d) for i in range(2)]

    w_in_b = _regroup_w_in(w_in).astype(BF16)
    n_in = w_in.shape[1]
    tn = n_in // 4
    p = _in_proj(x, norm1_w, mod_x[0], mod_x[1], w_in_b, tm=min(512, l), tn=tn)
    pc = _in_proj(ctx, norm1_w, mod_c[0], mod_c[1], w_in_b, tm=min(256, lc), tn=tn)

    attn = _attention(p, pc, rope_c, rope_s, q_norm_w, k_norm_w, tq=min(256, l))
    ret = _retention(p, pc, rope_c, rope_s, ret_decay_fwd, ret_decay_bwd, ret_norm_w)
    x1, h2 = _merge(attn, ret, p, x, mod_x[2], mod_x[3], mod_x[4], norm2_w, w_attn_branch.astype(BF16),
                    w_ret_branch.astype(BF16), w_merge_out.astype(BF16), tm=min(256, l))

    n = b * l
    h2f = h2.reshape(n, d)
    keys = peer_keys.reshape(PEER_HEADS * 2, PEER_N_KEYS, PEER_D_HALF).astype(BF16)
    i1, i2, g = _peer_topk(h2f, peer_w_q.astype(BF16), keys, tm=256)
    w = _peer_w(i1, i2, g, tb=128)
    out = _peer_dense(h2f, peer_u.astype(BF16), peer_v.astype(BF16), w, x1.reshape(n, d), mod_x[5], norm_f_w,
                      seq=l, tm=min(512, l), te=512)
    return out.reshape(b, l, d)


def kernel(x, c, ctx, c_ctx, ada_w, ada_b, norm1_w, w_in, q_norm_w, k_norm_w, ret_decay_fwd, ret_decay_bwd,
           ret_norm_w, w_attn_branch, w_ret_branch, w_merge_out, norm2_w, peer_w_q, peer_keys, peer_u, peer_v,
           norm_f_w):
    depth = ada_w.shape[0]
    assert depth == 1, "context-stream update between layers is not implemented"
    b, l, d = x.shape
    rows = -(-(b + 1) // SUBLANES) * SUBLANES
    c_rows = jnp.zeros((rows, d), F32).at[:b].set(c).at[b].set(c_ctx)
    rope_c, rope_s = _rope_tables(l)
    return _layer(x, ctx, c_rows, rope_c, rope_s, ada_w[0], ada_b[0], norm1_w[0], w_in[0], q_norm_w[0], k_norm_w[0],
                  ret_decay_fwd[0], ret_decay_bwd[0], ret_norm_w[0], w_attn_branch[0], w_ret_branch[0],
                  w_merge_out[0], norm2_w[0], peer_w_q[0], peer_keys[0], peer_u[0], peer_v[0], norm_f_w)
```

```python
import functools

import jax
import jax.numpy as jnp
from jax import lax
from jax.experimental import pallas as pl
from jax.experimental.pallas import tpu as pltpu

F32 = jnp.float32
BF16 = jnp.bfloat16
I32 = jnp.int32

EPS = 1e-6
GRID_W = 64
ROPE_BASE = 10000.0
ATTN_HEADS = 8
ATTN_KV_HEADS = 2
HEAD_DIM = 128
RET_HEADS = 8
RET_DK = 128
RET_DV = 256
RET_CHUNK = 128
PEER_HEADS = 8
PEER_N_KEYS = 128
PEER_D_HALF = 128
PEER_TOPK = 16

LANES = 128
SUBLANES = 8
VMEM_LIMIT = 56 * 1024 * 1024

ATTN_Q_W = ATTN_HEADS * HEAD_DIM
ATTN_KV_W = ATTN_KV_HEADS * HEAD_DIM
RET_QK_W = RET_HEADS * RET_DK
RET_V_W = RET_HEADS * RET_DV
D_MODEL = 2048
_SRC_ORDER = (("qa", ATTN_Q_W), ("ka", ATTN_KV_W), ("va", ATTN_KV_W), ("qr", RET_QK_W), ("kr", RET_QK_W),
              ("vr", RET_V_W), ("gr", RET_V_W), ("ga", D_MODEL), ("gt", D_MODEL))
_DST_ORDER = ("ga", "gt", "vr", "gr", "qr", "kr", "qa", "ka", "va")


def _offsets(order, widths):
    off, o = {}, 0
    for name in order:
        off[name] = o
        o += widths[name]
    return off


_WIDTH = dict(_SRC_ORDER)
_SRC_OFF = _offsets([n for n, _ in _SRC_ORDER], _WIDTH)
OFF = _offsets(_DST_ORDER, _WIDTH)


def _regroup_w_in(w_in):
    return jnp.concatenate([w_in[:, _SRC_OFF[n]:_SRC_OFF[n] + _WIDTH[n]] for n in _DST_ORDER], axis=1)


def _cparams(*sem):
    return pltpu.CompilerParams(dimension_semantics=sem, vmem_limit_bytes=VMEM_LIMIT)


def _sigmoid(x):
    return 1.0 / (1.0 + jnp.exp(-x))


def _silu(x):
    return x * _sigmoid(x)


def _rot_half(x):
    return pltpu.roll(x, HEAD_DIM // 2, 1)


def _dot_nt(a, b):
    return lax.dot_general(a, b, (((1,), (1,)), ((), ())), preferred_element_type=F32)


def _dot_tn(a, b):
    return lax.dot_general(a, b, (((0,), (0,)), ((), ())), preferred_element_type=F32)


def _ada_kernel(c_ref, w_ref, b_ref, o_ref):
    sc = _silu(c_ref[...]).astype(BF16)
    o_ref[...] = jnp.dot(sc, w_ref[...].astype(BF16), preferred_element_type=F32) + b_ref[...]


def _ada(c_rows, ada_w, ada_b):
    rows, d = c_rows.shape
    n = ada_w.shape[1]
    tn = 1536
    return pl.pallas_call(
        _ada_kernel,
        grid=(n // tn,),
        in_specs=[pl.BlockSpec((rows, d), lambda j: (0, 0)),
                  pl.BlockSpec((d, tn), lambda j: (0, j)),
                  pl.BlockSpec((1, tn), lambda j: (0, j))],
        out_specs=pl.BlockSpec((rows, tn), lambda j: (0, j)),
        out_shape=jax.ShapeDtypeStruct((rows, n), F32),
        compiler_params=_cparams("arbitrary"),
        name="ada",
    )(c_rows, ada_w, ada_b.reshape(1, n))


def _in_proj_kernel(x_ref, nw_ref, shift_ref, scale_ref, w_ref, o_ref, h_ref):
    @pl.when(pl.program_id(2) == 0)
    def _():
        x = x_ref[...]
        y = x * lax.rsqrt(jnp.mean(x * x, axis=-1, keepdims=True) + EPS) * nw_ref[...]
        h_ref[...] = (y * (1.0 + scale_ref[...]) + shift_ref[...]).astype(BF16)

    o_ref[...] = jnp.dot(h_ref[...], w_ref[...], preferred_element_type=F32).astype(o_ref.dtype)


def _in_proj(x, norm_w, shift, scale, w, tm, tn):
    b, l, d = x.shape
    n = w.shape[1]
    per_batch = shift.shape[0] > 1
    mod_map = (lambda bi, i, j: (bi, 0, 0)) if per_batch else (lambda bi, i, j: (0, 0, 0))
    return pl.pallas_call(
        _in_proj_kernel,
        grid=(b, l // tm, n // tn),
        in_specs=[pl.BlockSpec((None, tm, d), lambda bi, i, j: (bi, i, 0)),
                  pl.BlockSpec((1, d), lambda bi, i, j: (0, 0)),
                  pl.BlockSpec((None, 1, d), mod_map),
                  pl.BlockSpec((None, 1, d), mod_map),
                  pl.BlockSpec((d, tn), lambda bi, i, j: (0, j))],
        out_specs=pl.BlockSpec((None, tm, tn), lambda bi, i, j: (bi, i, j)),
        out_shape=jax.ShapeDtypeStruct((b, l, n), BF16),
        scratch_shapes=[pltpu.VMEM((tm, d), BF16)],
        compiler_params=_cparams("parallel", "parallel", "arbitrary"),
        name="in_proj",
    )(x, norm_w.reshape(1, d), shift, scale, w)


def _head_rms(x, w):
    return x * lax.rsqrt(jnp.mean(x * x, axis=-1, keepdims=True) + EPS) * w


def _attn_kernel(q_ref, kc_ref, k_ref, vc_ref, v_ref, cq_ref, sq_ref, ck_ref, sk_ref, qnw_ref, knw_ref,
                 o_ref, k_s, v_s, *, lc, group):
    @pl.when(pl.program_id(2) == 0)
    def _():
        knw = knw_ref[...]
        k_s[0:lc, :] = _head_rms(kc_ref[...].astype(F32), knw).astype(BF16)
        kn = _head_rms(k_ref[...].astype(F32), knw)
        k_s[lc:, :] = (kn * ck_ref[...] + _rot_half(kn) * sk_ref[...]).astype(BF16)
        v_s[0:lc, :] = vc_ref[...]
        v_s[lc:, :] = v_ref[...]

    c = (HEAD_DIM ** -0.5) * 1.4426950408889634
    qnw = qnw_ref[...]
    cq = cq_ref[...]
    sq = sq_ref[...]
    for g in range(group):
        sl = slice(g * HEAD_DIM, (g + 1) * HEAD_DIM)
        qn = _head_rms(q_ref[:, sl].astype(F32), qnw)
        qr = (qn * cq + _rot_half(qn) * sq).astype(BF16)
        s = _dot_nt(qr, k_s[...])
        m = jnp.max(s, axis=-1, keepdims=True)
        p = jnp.exp2((s - m) * c)
        l = jnp.sum(p, axis=-1, keepdims=True)
        o = jnp.dot(p.astype(BF16), v_s[...], preferred_element_type=F32)
        o_ref[:, sl] = (o * (1.0 / l)).astype(o_ref.dtype)


def _attention(p, pc, rope_c, rope_s, q_norm_w, k_norm_w, tq):
    b, l, _ = p.shape
    lc = pc.shape[1]
    group = ATTN_HEADS // ATTN_KV_HEADS
    gw = group * HEAD_DIM
    kq0 = OFF["qa"] // gw
    kk0 = OFF["ka"] // HEAD_DIM
    kv0 = OFF["va"] // HEAD_DIM
    full = lambda bi, h, i: (0, 0)
    return pl.pallas_call(
        functools.partial(_attn_kernel, lc=lc, group=group),
        grid=(b, ATTN_KV_HEADS, l // tq),
        in_specs=[pl.BlockSpec((None, tq, gw), lambda bi, h, i: (bi, i, kq0 + h)),
                  pl.BlockSpec((None, lc, HEAD_DIM), lambda bi, h, i: (bi, 0, kk0 + h)),
                  pl.BlockSpec((None, l, HEAD_DIM), lambda bi, h, i: (bi, 0, kk0 + h)),
                  pl.BlockSpec((None, lc, HEAD_DIM), lambda bi, h, i: (bi, 0, kv0 + h)),
                  pl.BlockSpec((None, l, HEAD_DIM), lambda bi, h, i: (bi, 0, kv0 + h)),
                  pl.BlockSpec((tq, HEAD_DIM), lambda bi, h, i: (i, 0)),
                  pl.BlockSpec((tq, HEAD_DIM), lambda bi, h, i: (i, 0)),
                  pl.BlockSpec((l, HEAD_DIM), full),
                  pl.BlockSpec((l, HEAD_DIM), full),
                  pl.BlockSpec((1, HEAD_DIM), full),
                  pl.BlockSpec((1, HEAD_DIM), full)],
        out_specs=pl.BlockSpec((None, tq, gw), lambda bi, h, i: (bi, i, h)),
        out_shape=jax.ShapeDtypeStruct((b, l, ATTN_Q_W), BF16),
        scratch_shapes=[pltpu.VMEM((lc + l, HEAD_DIM), BF16), pltpu.VMEM((lc + l, HEAD_DIM), BF16)],
        compiler_params=_cparams("parallel", "parallel", "arbitrary"),
        name="attention",
    )(p, pc, p, pc, p, rope_c, rope_s, rope_c, rope_s, q_norm_w.reshape(1, HEAD_DIM), k_norm_w.reshape(1, HEAD_DIM))


def _ret_kernel(q_ref, k_ref, v_ref, g_ref, kc_ref, vc_ref, cos_ref, sin_ref, z_ref, nw_ref, o_ref,
                q_s, k_s, acc, st_f, st_b, *, nc, lc):
    ch = RET_CHUNK
    assert nc % 2 == 0
    scale = RET_DK ** -0.5
    lg_f = -jnp.exp(z_ref[0])
    lg_b = -jnp.exp(z_ref[1])
    row = lax.broadcasted_iota(I32, (ch, ch), 0).astype(F32)
    col = lax.broadcasted_iota(I32, (ch, ch), 1).astype(F32)
    d = row - col
    intra_f = jnp.where(d >= 0, jnp.exp(lg_f * jnp.maximum(d, 0.0)), 0.0)
    intra_b = jnp.where(d <= 0, jnp.exp(lg_b * jnp.maximum(-d, 0.0)), 0.0)

    def wide(a):
        return jnp.concatenate([a] * (RET_DV // LANES), axis=1)

    qd_f = wide(jnp.exp(lg_f * (row + 1.0)))
    kd_f = jnp.exp(lg_f * (ch - 1.0 - row))
    cd_f = wide(jnp.exp(lg_f * float(ch)))
    qd_b = wide(jnp.exp(lg_b * (ch - row)))
    kd_b = jnp.exp(lg_b * row)
    cd_b = wide(jnp.exp(lg_b * float(ch)))

    j = lax.broadcasted_iota(I32, (lc, RET_DK), 0).astype(F32)
    kc = kc_ref[...].astype(F32) * scale
    vc = vc_ref[...]
    st_f[...] = _dot_tn((kc * jnp.exp(lg_f * (lc - 1.0 - j))).astype(BF16), vc)
    st_b[...] = _dot_tn((kc * jnp.exp(lg_b * j)).astype(BF16), vc)

    def rope(c, carry):
        r0 = pl.multiple_of(c * ch, ch)
        cs = cos_ref[pl.ds(r0, ch), :]
        sn = sin_ref[pl.ds(r0, ch), :]
        qf = q_ref[pl.ds(r0, ch), :].astype(F32)
        kf = k_ref[pl.ds(r0, ch), :].astype(F32)
        q_s[pl.ds(r0, ch), :] = (qf * cs + _rot_half(qf) * sn).astype(BF16)
        k_s[pl.ds(r0, ch), :] = (kf * cs + _rot_half(kf) * sn) * scale
        return carry

    lax.fori_loop(0, nc, rope, 0, unroll=2)

    def scan_chunk(c, st, intra, qd, kd, cd):
        r0 = pl.multiple_of(c * ch, ch)
        qb = q_s[pl.ds(r0, ch), :]
        kr = k_s[pl.ds(r0, ch), :]
        vb = v_ref[pl.ds(r0, ch), :]
        state = st[...]
        s = _dot_nt(qb, kr.astype(BF16)) * intra
        o = jnp.dot(s.astype(BF16), vb, preferred_element_type=F32)
        o = o + jnp.dot(qb, state.astype(BF16), preferred_element_type=F32) * qd
        st[...] = state * cd + _dot_tn((kr * kd).astype(BF16), vb)
        return r0, o

    nw = nw_ref[...]

    def finish(r0, o):
        mu = jnp.mean(o, axis=-1, keepdims=True)
        oc = o - mu
        var = jnp.mean(oc * oc, axis=-1, keepdims=True)
        hn = oc * lax.rsqrt(var + EPS) * nw
        gate = g_ref[pl.ds(r0, ch), :].astype(F32)
        o_ref[pl.ds(r0, ch), :] = (_silu(gate) * hn).astype(o_ref.dtype)

    def first_half(i, carry):
        r0, o = scan_chunk(i, st_f, intra_f, qd_f, kd_f, cd_f)
        acc[pl.ds(r0, ch), :] = o
        r1, o1 = scan_chunk(nc - 1 - i, st_b, intra_b, qd_b, kd_b, cd_b)
        acc[pl.ds(r1, ch), :] = o1
        return carry

    def second_half(i, carry):
        r0, o = scan_chunk(i, st_f, intra_f, qd_f, kd_f, cd_f)
        finish(r0, acc[pl.ds(r0, ch), :] + o)
        r1, o1 = scan_chunk(nc - 1 - i, st_b, intra_b, qd_b, kd_b, cd_b)
        finish(r1, acc[pl.ds(r1, ch), :] + o1)
        return carry

    lax.fori_loop(0, nc // 2, first_half, 0, unroll=2)
    lax.fori_loop(nc // 2, nc, second_half, 0, unroll=2)


def _retention(p, pc, rope_c, rope_s, decay_fwd, decay_bwd, ret_norm_w):
    b, l, _ = p.shape
    lc = pc.shape[1]
    nc = l // RET_CHUNK
    z = jnp.broadcast_to(jnp.stack([decay_fwd, decay_bwd])[:, :, None, None], (2, RET_HEADS, 1, LANES))
    kq0, kk0 = OFF["qr"] // RET_DK, OFF["kr"] // RET_DK
    kv0, kg0 = OFF["vr"] // RET_DV, OFF["gr"] // RET_DV
    full = lambda bi, h: (0, 0)
    return pl.pallas_call(
        functools.partial(_ret_kernel, nc=nc, lc=lc),
        grid=(b, RET_HEADS),
        in_specs=[pl.BlockSpec((None, l, RET_DK), lambda bi, h: (bi, 0, kq0 + h)),
                  pl.BlockSpec((None, l, RET_DK), lambda bi, h: (bi, 0, kk0 + h)),
                  pl.BlockSpec((None, l, RET_DV), lambda bi, h: (bi, 0, kv0 + h)),
                  pl.BlockSpec((None, l, RET_DV), lambda bi, h: (bi, 0, kg0 + h)),
                  pl.BlockSpec((None, lc, RET_DK), lambda bi, h: (bi, 0, kk0 + h)),
                  pl.BlockSpec((None, lc, RET_DV), lambda bi, h: (bi, 0, kv0 + h)),
                  pl.BlockSpec((l, RET_DK), full),
                  pl.BlockSpec((l, RET_DK), full),
                  pl.BlockSpec((2, None, 1, LANES), lambda bi, h: (0, h, 0, 0)),
                  pl.BlockSpec((1, RET_DV), lambda bi, h: (0, h))],
        out_specs=pl.BlockSpec((None, l, RET_DV), lambda bi, h: (bi, 0, h)),
        out_shape=jax.ShapeDtypeStruct((b, l, RET_V_W), BF16),
        scratch_shapes=[pltpu.VMEM((l, RET_DK), BF16), pltpu.VMEM((l, RET_DK), F32), pltpu.VMEM((l, RET_DV), F32),
                        pltpu.VMEM((RET_DK, RET_DV), F32), pltpu.VMEM((RET_DK, RET_DV), F32)],
        compiler_params=_cparams("parallel", "parallel"),
        name="retention",
    )(p, p, p, p, pc, pc, rope_c, rope_s, z, ret_norm_w.reshape(1, RET_V_W))


def _merge_kernel(attn_ref, ret_ref, ga_ref, gr_ref, x_ref, gate_ref, shift_ref, scale_ref, nw_ref,
                  wa_ref, wr_ref, wm_ref, x1_ref, h2_ref):
    a = jnp.dot(attn_ref[...], wa_ref[...], preferred_element_type=F32)
    r = jnp.dot(ret_ref[...], wr_ref[...], preferred_element_type=F32)
    mixed = _sigmoid(ga_ref[...].astype(F32)) * a + _sigmoid(gr_ref[...].astype(F32)) * r
    y = jnp.dot(mixed.astype(BF16), wm_ref[...], preferred_element_type=F32)
    x1 = x_ref[...] + gate_ref[...] * y
    x1_ref[...] = x1
    hn = x1 * lax.rsqrt(jnp.mean(x1 * x1, axis=-1, keepdims=True) + EPS) * nw_ref[...]
    h2_ref[...] = (hn * (1.0 + scale_ref[...]) + shift_ref[...]).astype(BF16)


def _merge(attn, ret, p, x, gate, shift2, scale2, norm2_w, wa, wr, wm, tm):
    b, l, d = x.shape
    assert d == D_MODEL
    kga, kgt = OFF["ga"] // d, OFF["gt"] // d
    row = lambda bi, i: (bi, i, 0)
    mod = lambda bi, i: (bi, 0, 0)
    const = lambda bi, i: (0, 0)
    once = pl.Buffered(1)
    return pl.pallas_call(
        _merge_kernel,
        grid=(b, l // tm),
        in_specs=[pl.BlockSpec((None, tm, ATTN_Q_W), row),
                  pl.BlockSpec((None, tm, RET_V_W), row),
                  pl.BlockSpec((None, tm, d), lambda bi, i: (bi, i, kga)),
                  pl.BlockSpec((None, tm, d), lambda bi, i: (bi, i, kgt)),
                  pl.BlockSpec((None, tm, d), row),
                  pl.BlockSpec((None, 1, d), mod),
                  pl.BlockSpec((None, 1, d), mod),
                  pl.BlockSpec((None, 1, d), mod),
                  pl.BlockSpec((1, d), const),
                  pl.BlockSpec(wa.shape, const, pipeline_mode=once),
                  pl.BlockSpec(wr.shape, const, pipeline_mode=once),
                  pl.BlockSpec(wm.shape, const, pipeline_mode=once)],
        out_specs=[pl.BlockSpec((None, tm, d), row), pl.BlockSpec((None, tm, d), row)],
        out_shape=[jax.ShapeDtypeStruct((b, l, d), F32), jax.ShapeDtypeStruct((b, l, d), BF16)],
        compiler_params=_cparams("parallel", "parallel"),
        name="merge",
    )(attn, ret, p, p, x, gate, shift2, scale2, norm2_w.reshape(1, d), wa, wr, wm)


def _top_rows(s, k):
    n = s.shape[0]
    rid = lax.broadcasted_iota(I32, s.shape, 0)
    vals, ids = [], []
    for _ in range(k):
        m = jnp.max(s, axis=0, keepdims=True)
        sel = jnp.min(jnp.where(s == m, rid, n), axis=0, keepdims=True)
        vals.append(m)
        ids.append(sel)
        s = jnp.where(rid == sel, -jnp.inf, s)
    return jnp.concatenate(vals, axis=0), jnp.concatenate(ids, axis=0)


def _pair_candidates(s1, s2):
    k = PEER_TOPK
    t = s1.shape[1]
    r8 = lax.broadcasted_iota(I32, (SUBLANES, t), 0)
    r16 = lax.broadcasted_iota(I32, (k, t), 0)
    neg = -jnp.inf
    sums = [s1[0:1] + s2]
    flat = [r16]
    for a, nb in ((1, 8), (2, 5), (3, 4)):
        sums.append(jnp.where(r8 < nb, s1[a:a + 1] + s2[0:8], neg))
        flat.append(a * k + r8)
    sums.append(s1[8:16] + s2[0:1])
    flat.append((r8 + 8) * k)
    for b_, lo, hi in ((0, 4, 8), (1, 4, 8), (2, 4, 5)):
        sums.append(jnp.where((r8 >= lo) & (r8 < hi), s1[0:8] + s2[b_:b_ + 1], neg))
        flat.append(r8 * k + b_)
    return jnp.concatenate(sums, axis=0), jnp.concatenate(flat, axis=0)


def _peer_topk_kernel(h_ref, wq_ref, keys_ref, i1_ref, i2_ref, g_ref, q_s, i1_s, i2_s, g_s):
    k = PEER_TOPK
    q_s[...] = jnp.dot(h_ref[...], wq_ref[...], preferred_element_type=F32).astype(BF16)
    t = q_s.shape[0]
    big = k * k
    r16 = lax.broadcasted_iota(I32, (k, t), 0)

    def head(h, carry):
        c0 = pl.multiple_of(h * 2 * PEER_D_HALF, 2 * PEER_D_HALF)
        qh = q_s[:, pl.ds(c0, 2 * PEER_D_HALF)]
        s1, n1 = _top_rows(_dot_nt(keys_ref[2 * h], qh[:, :PEER_D_HALF]), k)
        s2, n2 = _top_rows(_dot_nt(keys_ref[2 * h + 1], qh[:, PEER_D_HALF:]), k)
        cand, flat = _pair_candidates(s1, s2)
        tops, e1, e2 = [], [], []
        for _ in range(k):
            m = jnp.max(cand, axis=0, keepdims=True)
            f = jnp.min(jnp.where(cand == m, flat, big), axis=0, keepdims=True)
            cand = jnp.where(flat == f, -jnp.inf, cand)
            a = f // k
            b_ = f - a * k
            tops.append(m)
            e1.append(jnp.sum(jnp.where(r16 == a, n1, 0), axis=0, keepdims=True))
            e2.append(jnp.sum(jnp.where(r16 == b_, n2, 0), axis=0, keepdims=True))
        top = jnp.concatenate(tops, axis=0)
        ex = jnp.exp(top - top[0:1])
        r0 = pl.multiple_of(h * k, k)
        g_s[pl.ds(r0, k), :] = ex / jnp.sum(ex, axis=0, keepdims=True)
        i1_s[pl.ds(r0, k), :] = jnp.concatenate(e1, axis=0)
        i2_s[pl.ds(r0, k), :] = jnp.concatenate(e2, axis=0)
        return carry

    lax.fori_loop(0, PEER_HEADS, head, 0)
    i1_ref[...] = i1_s[...].T
    i2_ref[...] = i2_s[...].T
    g_ref[...] = g_s[...].T


def _peer_topk(h2, wq, keys, tm):
    n, d = h2.shape
    slots = PEER_HEADS * PEER_TOPK
    const2 = lambda i: (0, 0)
    row = lambda i: (i, 0)
    out = jax.ShapeDtypeStruct
    return pl.pallas_call(
        _peer_topk_kernel,
        grid=(n // tm,),
        in_specs=[pl.BlockSpec((tm, d), row),
                  pl.BlockSpec(wq.shape, const2, pipeline_mode=pl.Buffered(1)),
                  pl.BlockSpec(keys.shape, lambda i: (0, 0, 0))],
        out_specs=[pl.BlockSpec((tm, slots), row)] * 3,
        out_shape=[out((n, slots), I32), out((n, slots), I32), out((n, slots), F32)],
        scratch_shapes=[pltpu.VMEM((tm, wq.shape[1]), BF16), pltpu.VMEM((slots, tm), I32),
                        pltpu.VMEM((slots, tm), I32), pltpu.VMEM((slots, tm), F32)],
        compiler_params=_cparams("parallel"),
        name="peer_topk",
    )(h2, wq, keys)


def _peer_w_kernel(i1_ref, i2_ref, g_ref, w_ref, w3):
    nk = PEER_N_KEYS
    tb = i1_ref.shape[0]
    grp = 2 * SUBLANES
    rid = lax.broadcasted_iota(I32, (nk, i1_ref.shape[1]), 0)
    sub = lax.broadcasted_iota(I32, (SUBLANES, nk), 0)

    def rows_to_tokens(v):
        v = list(v)
        for dist in (4, 2, 1):
            keep = (sub & dist) == 0
            for k in range(SUBLANES):
                if k & dist == 0:
                    a, b_ = v[k], v[k + dist]
                    v[k] = jnp.where(keep, a, pltpu.roll(b_, dist, 0))
                    v[k + dist] = jnp.where(keep, pltpu.roll(a, SUBLANES - dist, 0), b_)
        return v

    def group(gi, carry):
        t0 = pl.multiple_of(gi * grp, grp)
        i1 = i1_ref[pl.ds(t0, grp), :]
        i2 = i2_ref[pl.ds(t0, grp), :]
        g = g_ref[pl.ds(t0, grp), :]
        for k in range(grp):
            left = jnp.where(rid == i1[k:k + 1], g[k:k + 1], 0.0).astype(BF16)
            right = jnp.where(rid == i2[k:k + 1], 1.0, 0.0).astype(BF16)
            w3[k * nk:(k + 1) * nk, :] = _dot_nt(left, right)
        for j in range(nk // SUBLANES):
            halves = []
            for half in range(grp // SUBLANES):
                tiles = [w3[(half * SUBLANES + k) * nk + j * SUBLANES:(half * SUBLANES + k) * nk + (j + 1) * SUBLANES, :]
                         for k in range(SUBLANES)]
                halves.append(rows_to_tokens(tiles))
            for i in range(SUBLANES):
                r = j * SUBLANES + i
                tile = jnp.concatenate([h[i] for h in halves], axis=0)
                w_ref[pl.ds(t0, grp), r * nk:(r + 1) * nk] = tile.astype(w_ref.dtype)
        return carry

    lax.fori_loop(0, tb // grp, group, 0)


def _peer_w(i1, i2, g, tb):
    n, slots = i1.shape
    ne = PEER_N_KEYS * PEER_N_KEYS
    row = lambda i: (i, 0)
    return pl.pallas_call(
        _peer_w_kernel,
        grid=(n // tb,),
        in_specs=[pl.BlockSpec((tb, slots), row)] * 3,
        out_specs=pl.BlockSpec((tb, ne), row),
        out_shape=jax.ShapeDtypeStruct((n, ne), BF16),
        scratch_shapes=[pltpu.VMEM((2 * SUBLANES * PEER_N_KEYS, PEER_N_KEYS), F32)],
        compiler_params=_cparams("parallel"),
        name="peer_w",
    )(i1, i2, g)


def _gelu_tanh(x):
    return 0.5 * x * (1.0 + jnp.tanh(0.7978845608028654 * (x + 0.044715 * (x * x * x))))


def _peer_dense_kernel(h_ref, u_ref, v_ref, w_ref, x_ref, gate_ref, nw_ref, o_ref, acc):
    j = pl.program_id(1)

    @pl.when(j == 0)
    def _():
        acc[...] = jnp.zeros_like(acc)

    sub = 512
    for s in range(u_ref.shape[0] // sub):
        sl = slice(s * sub, (s + 1) * sub)
        a = _dot_nt(h_ref[...], u_ref[sl, :])
        act = (_gelu_tanh(a) * w_ref[:, sl].astype(F32)).astype(BF16)
        acc[...] += jnp.dot(act, v_ref[sl, :], preferred_element_type=F32)

    @pl.when(j == pl.num_programs(1) - 1)
    def _():
        x2 = x_ref[...] + gate_ref[...] * acc[...]
        o_ref[...] = x2 * lax.rsqrt(jnp.mean(x2 * x2, axis=-1, keepdims=True) + EPS) * nw_ref[...]


def _peer_dense(h2, u, v, w, x1, gate, norm_f_w, seq, tm, te):
    n, d = h2.shape
    ne = u.shape[0]
    tiles_per_batch = seq // tm
    row = lambda i, j: (i, 0)
    return pl.pallas_call(
        _peer_dense_kernel,
        grid=(n // tm, ne // te),
        in_specs=[pl.BlockSpec((tm, d), row),
                  pl.BlockSpec((te, d), lambda i, j: (j, 0)),
                  pl.BlockSpec((te, d), lambda i, j: (j, 0)),
                  pl.BlockSpec((tm, te), lambda i, j: (i, j)),
                  pl.BlockSpec((tm, d), row),
                  pl.BlockSpec((None, 1, d), lambda i, j: (i // tiles_per_batch, 0, 0)),
                  pl.BlockSpec((1, d), lambda i, j: (0, 0))],
        out_specs=pl.BlockSpec((tm, d), row),
        out_shape=jax.ShapeDtypeStruct((n, d), F32),
        scratch_shapes=[pltpu.VMEM((tm, d), F32)],
        compiler_params=_cparams("parallel", "arbitrary"),
        name="peer_dense",
    )(h2, u, v, w, x1, gate, norm_f_w.reshape(1, d))


def _rope_tables(length):
    t = jnp.arange(length, dtype=jnp.int32)
    row = (t // GRID_W).astype(F32)
    col = (t % GRID_W).astype(F32)
    n_freq = HEAD_DIM // 4
    inv_freq = ROPE_BASE ** (-jnp.arange(n_freq, dtype=F32) / n_freq)
    ang = jnp.concatenate([row[:, None] * inv_freq, col[:, None] * inv_freq], axis=-1)
    cos, sin = jnp.cos(ang), jnp.sin(ang)
    return jnp.concatenate([cos, cos], axis=-1), jnp.concatenate([-sin, sin], axis=-1)


def _layer(x, ctx, c_rows, rope_c, rope_s, ada_w, ada_b, norm1_w, w_in, q_norm_w, k_norm_w, ret_decay_fwd,
           ret_decay_bwd, ret_norm_w, w_attn_branch, w_ret_branch, w_merge_out, norm2_w, peer_w_q, peer_keys,
           peer_u, peer_v, norm_f_w):
    b, l, d = x.shape
    lc = ctx.shape[1]
    mod = _ada(c_rows, ada_w, ada_b)
    mod_x = [mod[:b, i * d:(i + 1) * d].reshape(b, 1, d) for i in range(6)]
    mod_c = [mod[b:b + 1, i * d:(i + 1) * d].reshape(1, 1, d) for i in range(2)]

    w_in_b = _regroup_w_in(w_in).astype(BF16)
    n_in = w_in.shape[1]
    tn = n_in // 4
    p = _in_proj(x, norm1_w, mod_x[0], mod_x[1], w_in_b, tm=min(512, l), tn=tn)
    pc = _in_proj(ctx, norm1_w, mod_c[0], mod_c[1], w_in_b, tm=min(256, lc), tn=tn)

    attn = _attention(p, pc, rope_c, rope_s, q_norm_w, k_norm_w, tq=min(256, l))
    ret = _retention(p, pc, rope_c, rope_s, ret_decay_fwd, ret_decay_bwd, ret_norm_w)
    x1, h2 = _merge(attn, ret, p, x, mod_x[2], mod_x[3], mod_x[4], norm2_w, w_attn_branch.astype(BF16),
                    w_ret_branch.astype(BF16), w_merge_out.astype(BF16), tm=min(256, l))

    n = b * l
    h2f = h2.reshape(n, d)
    keys = peer_keys.reshape(PEER_HEADS * 2, PEER_N_KEYS, PEER_D_HALF).astype(BF16)
    i1, i2, g = _peer_topk(h2f, peer_w_q.astype(BF16), keys, tm=256)
    w = _peer_w(i1, i2, g, tb=128)
    out = _peer_dense(h2f, peer_u.astype(BF16), peer_v.astype(BF16), w, x1.reshape(n, d), mod_x[5], norm_f_w,
                      seq=l, tm=min(512, l), te=1024)
    return out.reshape(b, l, d)


def kernel(x, c, ctx, c_ctx, ada_w, ada_b, norm1_w, w_in, q_norm_w, k_norm_w, ret_decay_fwd, ret_decay_bwd,
           ret_norm_w, w_attn_branch, w_ret_branch, w_merge_out, norm2_w, peer_w_q, peer_keys, peer_u, peer_v,
           norm_f_w):
    depth = ada_w.shape[0]
    assert depth == 1, "context-stream update between layers is not implemented"
    b, l, d = x.shape
    rows = -(-(b + 1) // SUBLANES) * SUBLANES
    c_rows = jnp.zeros((rows, d), F32).at[:b].set(c).at[b].set(c_ctx)
    rope_c, rope_s = _rope_tables(l)
    return _layer(x, ctx, c_rows, rope_c, rope_s, ada_w[0], ada_b[0], norm1_w[0], w_in[0], q_norm_w[0], k_norm_w[0],
                  ret_decay_fwd[0], ret_decay_bwd[0], ret_norm_w[0], w_attn_branch[0], w_ret_branch[0],
                  w_merge_out[0], norm2_w[0], peer_w_q[0], peer_keys[0], peer_u[0], peer_v[0], norm_f_w)
```

```python
import functools

import jax
import jax.numpy as jnp
from jax import lax
from jax.experimental import pallas as pl
from jax.experimental.pallas import tpu as pltpu

F32 = jnp.float32
BF16 = jnp.bfloat16
I32 = jnp.int32

EPS = 1e-6
GRID_W = 64
ROPE_BASE = 10000.0
ATTN_HEADS = 8
ATTN_KV_HEADS = 2
HEAD_DIM = 128
RET_HEADS = 8
RET_DK = 128
RET_DV = 256
RET_CHUNK = 128
PEER_HEADS = 8
PEER_N_KEYS = 128
PEER_D_HALF = 128
PEER_TOPK = 16

LANES = 128
SUBLANES = 8
VMEM_LIMIT = 56 * 1024 * 1024

ATTN_Q_W = ATTN_HEADS * HEAD_DIM
ATTN_KV_W = ATTN_KV_HEADS * HEAD_DIM
RET_QK_W = RET_HEADS * RET_DK
RET_V_W = RET_HEADS * RET_DV
D_MODEL = 2048
_SRC_ORDER = (("qa", ATTN_Q_W), ("ka", ATTN_KV_W), ("va", ATTN_KV_W), ("qr", RET_QK_W), ("kr", RET_QK_W),
              ("vr", RET_V_W), ("gr", RET_V_W), ("ga", D_MODEL), ("gt", D_MODEL))
_DST_ORDER = ("ga", "gt", "vr", "gr", "qr", "kr", "qa", "ka", "va")


def _offsets(order, widths):
    off, o = {}, 0
    for name in order:
        off[name] = o
        o += widths[name]
    return off


_WIDTH = dict(_SRC_ORDER)
_SRC_OFF = _offsets([n for n, _ in _SRC_ORDER], _WIDTH)
OFF = _offsets(_DST_ORDER, _WIDTH)


def _regroup_w_in(w_in):
    return jnp.concatenate([w_in[:, _SRC_OFF[n]:_SRC_OFF[n] + _WIDTH[n]] for n in _DST_ORDER], axis=1)


def _cparams(*sem):
    return pltpu.CompilerParams(dimension_semantics=sem, vmem_limit_bytes=VMEM_LIMIT)


def _sigmoid(x):
    return 1.0 / (1.0 + jnp.exp(-x))


def _silu(x):
    return x * _sigmoid(x)


def _rot_half(x):
    return pltpu.roll(x, HEAD_DIM // 2, 1)


def _dot_nt(a, b):
    return lax.dot_general(a, b, (((1,), (1,)), ((), ())), preferred_element_type=F32)


def _dot_tn(a, b):
    return lax.dot_general(a, b, (((0,), (0,)), ((), ())), preferred_element_type=F32)


def _ada_kernel(c_ref, w_ref, b_ref, o_ref):
    sc = _silu(c_ref[...]).astype(BF16)
    o_ref[...] = jnp.dot(sc, w_ref[...].astype(BF16), preferred_element_type=F32) + b_ref[...]


def _ada(c_rows, ada_w, ada_b):
    rows, d = c_rows.shape
    n = ada_w.shape[1]
    tn = 1536
    return pl.pallas_call(
        _ada_kernel,
        grid=(n // tn,),
        in_specs=[pl.BlockSpec((rows, d), lambda j: (0, 0)),
                  pl.BlockSpec((d, tn), lambda j: (0, j)),
                  pl.BlockSpec((1, tn), lambda j: (0, j))],
        out_specs=pl.BlockSpec((rows, tn), lambda j: (0, j)),
        out_shape=jax.ShapeDtypeStruct((rows, n), F32),
        compiler_params=_cparams("arbitrary"),
        name="ada",
    )(c_rows, ada_w, ada_b.reshape(1, n))


def _in_proj_kernel(x_ref, nw_ref, shift_ref, scale_ref, w_ref, o_ref, h_ref):
    @pl.when(pl.program_id(2) == 0)
    def _():
        x = x_ref[...]
        y = x * lax.rsqrt(jnp.mean(x * x, axis=-1, keepdims=True) + EPS) * nw_ref[...]
        h_ref[...] = (y * (1.0 + scale_ref[...]) + shift_ref[...]).astype(BF16)

    o_ref[...] = jnp.dot(h_ref[...], w_ref[...], preferred_element_type=F32).astype(o_ref.dtype)


def _in_proj(x, norm_w, shift, scale, w, tm, tn):
    b, l, d = x.shape
    n = w.shape[1]
    per_batch = shift.shape[0] > 1
    mod_map = (lambda bi, i, j: (bi, 0, 0)) if per_batch else (lambda bi, i, j: (0, 0, 0))
    return pl.pallas_call(
        _in_proj_kernel,
        grid=(b, l // tm, n // tn),
        in_specs=[pl.BlockSpec((None, tm, d), lambda bi, i, j: (bi, i, 0)),
                  pl.BlockSpec((1, d), lambda bi, i, j: (0, 0)),
                  pl.BlockSpec((None, 1, d), mod_map),
                  pl.BlockSpec((None, 1, d), mod_map),
                  pl.BlockSpec((d, tn), lambda bi, i, j: (0, j))],
        out_specs=pl.BlockSpec((None, tm, tn), lambda bi, i, j: (bi, i, j)),
        out_shape=jax.ShapeDtypeStruct((b, l, n), BF16),
        scratch_shapes=[pltpu.VMEM((tm, d), BF16)],
        compiler_params=_cparams("parallel", "parallel", "arbitrary"),
        name="in_proj",
    )(x, norm_w.reshape(1, d), shift, scale, w)


def _head_rms(x, w):
    return x * lax.rsqrt(jnp.mean(x * x, axis=-1, keepdims=True) + EPS) * w


def _attn_kernel(q_ref, kc_ref, k_ref, vc_ref, v_ref, cq_ref, sq_ref, ck_ref, sk_ref, qnw_ref, knw_ref,
                 o_ref, k_s, v_s, *, lc, group):
    @pl.when(pl.program_id(2) == 0)
    def _():
        knw = knw_ref[...]
        k_s[0:lc, :] = _head_rms(kc_ref[...].astype(F32), knw).astype(BF16)
        kn = _head_rms(k_ref[...].astype(F32), knw)
        k_s[lc:, :] = (kn * ck_ref[...] + _rot_half(kn) * sk_ref[...]).astype(BF16)
        v_s[0:lc, 0:HEAD_DIM] = vc_ref[...]
        v_s[lc:, 0:HEAD_DIM] = v_ref[...]
        v_s[:, HEAD_DIM:] = jnp.ones((v_s.shape[0], HEAD_DIM), BF16)

    c = (HEAD_DIM ** -0.5) * 1.4426950408889634
    qnw = qnw_ref[...]
    cq = cq_ref[...] * c
    sq = sq_ref[...] * c
    for g in range(group):
        sl = slice(g * HEAD_DIM, (g + 1) * HEAD_DIM)
        qn = _head_rms(q_ref[:, sl].astype(F32), qnw)
        qr = (qn * cq + _rot_half(qn) * sq).astype(BF16)
        s = _dot_nt(qr, k_s[...])
        m = jnp.max(s, axis=-1, keepdims=True)
        p = jnp.exp2((s - m).astype(BF16))
        ol = jnp.dot(p, v_s[...], preferred_element_type=F32)
        o_ref[:, sl] = (ol[:, :HEAD_DIM] * (1.0 / ol[:, HEAD_DIM:HEAD_DIM + 1])).astype(o_ref.dtype)


def _attention(p, pc, rope_c, rope_s, q_norm_w, k_norm_w, tq):
    b, l, _ = p.shape
    lc = pc.shape[1]
    group = ATTN_HEADS // ATTN_KV_HEADS
    gw = group * HEAD_DIM
    kq0 = OFF["qa"] // gw
    kk0 = OFF["ka"] // HEAD_DIM
    kv0 = OFF["va"] // HEAD_DIM
    full = lambda bi, h, i: (0, 0)
    return pl.pallas_call(
        functools.partial(_attn_kernel, lc=lc, group=group),
        grid=(b, ATTN_KV_HEADS, l // tq),
        in_specs=[pl.BlockSpec((None, tq, gw), lambda bi, h, i: (bi, i, kq0 + h)),
                  pl.BlockSpec((None, lc, HEAD_DIM), lambda bi, h, i: (bi, 0, kk0 + h)),
                  pl.BlockSpec((None, l, HEAD_DIM), lambda bi, h, i: (bi, 0, kk0 + h)),
                  pl.BlockSpec((None, lc, HEAD_DIM), lambda bi, h, i: (bi, 0, kv0 + h)),
                  pl.BlockSpec((None, l, HEAD_DIM), lambda bi, h, i: (bi, 0, kv0 + h)),
                  pl.BlockSpec((tq, HEAD_DIM), lambda bi, h, i: (i, 0)),
                  pl.BlockSpec((tq, HEAD_DIM), lambda bi, h, i: (i, 0)),
                  pl.BlockSpec((l, HEAD_DIM), full),
                  pl.BlockSpec((l, HEAD_DIM), full),
                  pl.BlockSpec((1, HEAD_DIM), full),
                  pl.BlockSpec((1, HEAD_DIM), full)],
        out_specs=pl.BlockSpec((None, tq, gw), lambda bi, h, i: (bi, i, h)),
        out_shape=jax.ShapeDtypeStruct((b, l, ATTN_Q_W), BF16),
        scratch_shapes=[pltpu.VMEM((lc + l, HEAD_DIM), BF16), pltpu.VMEM((lc + l, 2 * HEAD_DIM), BF16)],
        compiler_params=_cparams("parallel", "parallel", "arbitrary"),
        name="attention",
    )(p, pc, p, pc, p, rope_c, rope_s, rope_c, rope_s, q_norm_w.reshape(1, HEAD_DIM), k_norm_w.reshape(1, HEAD_DIM))


def _ret_kernel(q_ref, k_ref, v_ref, g_ref, kc_ref, vc_ref, cos_ref, sin_ref, z_ref, nw_ref, o_ref,
                q_s, k_s, acc, st_f, st_b, *, nc, lc):
    ch = RET_CHUNK
    assert nc % 2 == 0
    scale = RET_DK ** -0.5
    lg_f = -jnp.exp(z_ref[0])
    lg_b = -jnp.exp(z_ref[1])
    row = lax.broadcasted_iota(I32, (ch, ch), 0).astype(F32)
    col = lax.broadcasted_iota(I32, (ch, ch), 1).astype(F32)
    d = row - col
    intra_f = jnp.where(d >= 0, jnp.exp(lg_f * jnp.maximum(d, 0.0)), 0.0)
    intra_b = jnp.where(d <= 0, jnp.exp(lg_b * jnp.maximum(-d, 0.0)), 0.0)

    def wide(a):
        return jnp.concatenate([a] * (RET_DV // LANES), axis=1)

    qd_f = wide(jnp.exp(lg_f * (row + 1.0)))
    kd_f = jnp.exp(lg_f * (ch - 1.0 - row))
    cd_f = wide(jnp.exp(lg_f * float(ch)))
    qd_b = wide(jnp.exp(lg_b * (ch - row)))
    kd_b = jnp.exp(lg_b * row)
    cd_b = wide(jnp.exp(lg_b * float(ch)))

    j = lax.broadcasted_iota(I32, (lc, RET_DK), 0).astype(F32)
    kc = kc_ref[...].astype(F32) * scale
    vc = vc_ref[...]
    st_f[...] = _dot_tn((kc * jnp.exp(lg_f * (lc - 1.0 - j))).astype(BF16), vc)
    st_b[...] = _dot_tn((kc * jnp.exp(lg_b * j)).astype(BF16), vc)

    def rope(c, carry):
        r0 = pl.multiple_of(c * ch, ch)
        cs = cos_ref[pl.ds(r0, ch), :]
        sn = sin_ref[pl.ds(r0, ch), :]
        qf = q_ref[pl.ds(r0, ch), :].astype(F32)
        kf = k_ref[pl.ds(r0, ch), :].astype(F32)
        q_s[pl.ds(r0, ch), :] = (qf * cs + _rot_half(qf) * sn).astype(BF16)
        k_s[pl.ds(r0, ch), :] = (kf * cs + _rot_half(kf) * sn) * scale
        return carry

    lax.fori_loop(0, nc, rope, 0, unroll=2)

    def scan_chunk(c, st, intra, qd, kd, cd):
        r0 = pl.multiple_of(c * ch, ch)
        qb = q_s[pl.ds(r0, ch), :]
        kr = k_s[pl.ds(r0, ch), :]
        vb = v_ref[pl.ds(r0, ch), :]
        state = st[...]
        s = _dot_nt(qb, kr.astype(BF16)) * intra
        o = jnp.dot(s.astype(BF16), vb, preferred_element_type=F32)
        o = o + jnp.dot(qb, state.astype(BF16), preferred_element_type=F32) * qd
        st[...] = state * cd + _dot_tn((kr * kd).astype(BF16), vb)
        return r0, o

    nw = nw_ref[...]

    def finish(r0, o):
        mu = jnp.mean(o, axis=-1, keepdims=True)
        oc = o - mu
        var = jnp.mean(oc * oc, axis=-1, keepdims=True)
        hn = oc * lax.rsqrt(var + EPS) * nw
        gate = g_ref[pl.ds(r0, ch), :].astype(F32)
        o_ref[pl.ds(r0, ch), :] = (_silu(gate) * hn).astype(o_ref.dtype)

    def first_half(i, carry):
        r0, o = scan_chunk(i, st_f, intra_f, qd_f, kd_f, cd_f)
        acc[pl.ds(r0, ch), :] = o
        r1, o1 = scan_chunk(nc - 1 - i, st_b, intra_b, qd_b, kd_b, cd_b)
        acc[pl.ds(r1, ch), :] = o1
        return carry

    def second_half(i, carry):
        r0, o = scan_chunk(i, st_f, intra_f, qd_f, kd_f, cd_f)
        finish(r0, acc[pl.ds(r0, ch), :] + o)
        r1, o1 = scan_chunk(nc - 1 - i, st_b, intra_b, qd_b, kd_b, cd_b)
        finish(r1, acc[pl.ds(r1, ch), :] + o1)
        return carry

    lax.fori_loop(0, nc // 2, first_half, 0, unroll=4)
    lax.fori_loop(nc // 2, nc, second_half, 0, unroll=4)


def _retention(p, pc, rope_c, rope_s, decay_fwd, decay_bwd, ret_norm_w):
    b, l, _ = p.shape
    lc = pc.shape[1]
    nc = l // RET_CHUNK
    z = jnp.broadcast_to(jnp.stack([decay_fwd, decay_bwd])[:, :, None, None], (2, RET_HEADS, 1, LANES))
    kq0, kk0 = OFF["qr"] // RET_DK, OFF["kr"] // RET_DK
    kv0, kg0 = OFF["vr"] // RET_DV, OFF["gr"] // RET_DV
    full = lambda bi, h: (0, 0)
    return pl.pallas_call(
        functools.partial(_ret_kernel, nc=nc, lc=lc),
        grid=(b, RET_HEADS),
        in_specs=[pl.BlockSpec((None, l, RET_DK), lambda bi, h: (bi, 0, kq0 + h)),
                  pl.BlockSpec((None, l, RET_DK), lambda bi, h: (bi, 0, kk0 + h)),
                  pl.BlockSpec((None, l, RET_DV), lambda bi, h: (bi, 0, kv0 + h)),
                  pl.BlockSpec((None, l, RET_DV), lambda bi, h: (bi, 0, kg0 + h)),
                  pl.BlockSpec((None, lc, RET_DK), lambda bi, h: (bi, 0, kk0 + h)),
                  pl.BlockSpec((None, lc, RET_DV), lambda bi, h: (bi, 0, kv0 + h)),
                  pl.BlockSpec((l, RET_DK), full),
                  pl.BlockSpec((l, RET_DK), full),
                  pl.BlockSpec((2, None, 1, LANES), lambda bi, h: (0, h, 0, 0)),
                  pl.BlockSpec((1, RET_DV), lambda bi, h: (0, h))],
        out_specs=pl.BlockSpec((None, l, RET_DV), lambda bi, h: (bi, 0, h)),
        out_shape=jax.ShapeDtypeStruct((b, l, RET_V_W), BF16),
        scratch_shapes=[pltpu.VMEM((l, RET_DK), BF16), pltpu.VMEM((l, RET_DK), F32), pltpu.VMEM((l, RET_DV), F32),
                        pltpu.VMEM((RET_DK, RET_DV), F32), pltpu.VMEM((RET_DK, RET_DV), F32)],
        compiler_params=_cparams("parallel", "parallel"),
        name="retention",
    )(p, p, p, p, pc, pc, rope_c, rope_s, z, ret_norm_w.reshape(1, RET_V_W))


def _merge_kernel(attn_ref, ret_ref, ga_ref, gr_ref, x_ref, gate_ref, shift_ref, scale_ref, nw_ref,
                  wa_ref, wr_ref, wm_ref, x1_ref, h2_ref):
    a = jnp.dot(attn_ref[...], wa_ref[...], preferred_element_type=F32)
    r = jnp.dot(ret_ref[...], wr_ref[...], preferred_element_type=F32)
    mixed = _sigmoid(ga_ref[...].astype(F32)) * a + _sigmoid(gr_ref[...].astype(F32)) * r
    y = jnp.dot(mixed.astype(BF16), wm_ref[...], preferred_element_type=F32)
    x1 = x_ref[...] + gate_ref[...] * y
    x1_ref[...] = x1
    hn = x1 * lax.rsqrt(jnp.mean(x1 * x1, axis=-1, keepdims=True) + EPS) * nw_ref[...]
    h2_ref[...] = (hn * (1.0 + scale_ref[...]) + shift_ref[...]).astype(BF16)


def _merge(attn, ret, p, x, gate, shift2, scale2, norm2_w, wa, wr, wm, tm):
    b, l, d = x.shape
    assert d == D_MODEL
    kga, kgt = OFF["ga"] // d, OFF["gt"] // d
    row = lambda bi, i: (bi, i, 0)
    mod = lambda bi, i: (bi, 0, 0)
    const = lambda bi, i: (0, 0)
    once = pl.Buffered(1)
    return pl.pallas_call(
        _merge_kernel,
        grid=(b, l // tm),
        in_specs=[pl.BlockSpec((None, tm, ATTN_Q_W), row),
                  pl.BlockSpec((None, tm, RET_V_W), row),
                  pl.BlockSpec((None, tm, d), lambda bi, i: (bi, i, kga)),
                  pl.BlockSpec((None, tm, d), lambda bi, i: (bi, i, kgt)),
                  pl.BlockSpec((None, tm, d), row),
                  pl.BlockSpec((None, 1, d), mod),
                  pl.BlockSpec((None, 1, d), mod),
                  pl.BlockSpec((None, 1, d), mod),
                  pl.BlockSpec((1, d), const),
                  pl.BlockSpec(wa.shape, const, pipeline_mode=once),
                  pl.BlockSpec(wr.shape, const, pipeline_mode=once),
                  pl.BlockSpec(wm.shape, const, pipeline_mode=once)],
        out_specs=[pl.BlockSpec((None, tm, d), row), pl.BlockSpec((None, tm, d), row)],
        out_shape=[jax.ShapeDtypeStruct((b, l, d), F32), jax.ShapeDtypeStruct((b, l, d), BF16)],
        compiler_params=_cparams("parallel", "parallel"),
        name="merge",
    )(attn, ret, p, p, x, gate, shift2, scale2, norm2_w.reshape(1, d), wa, wr, wm)


def _top_rows(s, k):
    n = s.shape[0]
    rid = lax.broadcasted_iota(I32, s.shape, 0)
    vals, ids = [], []
    for _ in range(k):
        m = jnp.max(s, axis=0, keepdims=True)
        sel = jnp.min(jnp.where(s == m, rid, n), axis=0, keepdims=True)
        vals.append(m)
        ids.append(sel)
        s = jnp.where(rid == sel, -jnp.inf, s)
    return jnp.concatenate(vals, axis=0), jnp.concatenate(ids, axis=0)


def _pair_candidates(s1, s2):
    k = PEER_TOPK
    t = s1.shape[1]
    r8 = lax.broadcasted_iota(I32, (SUBLANES, t), 0)
    r16 = lax.broadcasted_iota(I32, (k, t), 0)
    neg = -jnp.inf
    sums = [s1[0:1] + s2]
    flat = [r16]
    for a, nb in ((1, 8), (2, 5), (3, 4)):
        sums.append(jnp.where(r8 < nb, s1[a:a + 1] + s2[0:8], neg))
        flat.append(a * k + r8)
    sums.append(s1[8:16] + s2[0:1])
    flat.append((r8 + 8) * k)
    for b_, lo, hi in ((0, 4, 8), (1, 4, 8), (2, 4, 5)):
        sums.append(jnp.where((r8 >= lo) & (r8 < hi), s1[0:8] + s2[b_:b_ + 1], neg))
        flat.append(r8 * k + b_)
    return jnp.concatenate(sums, axis=0), jnp.concatenate(flat, axis=0)


def _peer_topk_kernel(h_ref, wq_ref, keys_ref, i1_ref, i2_ref, g_ref, q_s, i1_s, i2_s, g_s):
    k = PEER_TOPK
    q_s[...] = jnp.dot(h_ref[...], wq_ref[...], preferred_element_type=F32).astype(BF16)
    t = q_s.shape[0]
    big = k * k
    r16 = lax.broadcasted_iota(I32, (k, t), 0)

    def head(h, carry):
        c0 = pl.multiple_of(h * 2 * PEER_D_HALF, 2 * PEER_D_HALF)
        qh = q_s[:, pl.ds(c0, 2 * PEER_D_HALF)]
        s1, n1 = _top_rows(_dot_nt(keys_ref[2 * h], qh[:, :PEER_D_HALF]), k)
        s2, n2 = _top_rows(_dot_nt(keys_ref[2 * h + 1], qh[:, PEER_D_HALF:]), k)
        cand, flat = _pair_candidates(s1, s2)
        tops, picks = [], []
        for _ in range(k):
            m = jnp.max(cand, axis=0, keepdims=True)
            f = jnp.min(jnp.where(cand == m, flat, big), axis=0, keepdims=True)
            cand = jnp.where(flat == f, -jnp.inf, cand)
            tops.append(m)
            picks.append(f)
        top = jnp.concatenate(tops, axis=0)
        pick = jnp.concatenate(picks, axis=0)
        pa = pick // k
        pb = pick - pa * k
        e1 = jnp.zeros_like(pick)
        e2 = jnp.zeros_like(pick)
        for a in range(k):
            e1 = jnp.where(pa == a, n1[a:a + 1], e1)
            e2 = jnp.where(pb == a, n2[a:a + 1], e2)
        ex = jnp.exp(top - top[0:1])
        r0 = pl.multiple_of(h * k, k)
        g_s[pl.ds(r0, k), :] = ex / jnp.sum(ex, axis=0, keepdims=True)
        i1_s[pl.ds(r0, k), :] = e1
        i2_s[pl.ds(r0, k), :] = e2
        return carry

    lax.fori_loop(0, PEER_HEADS, head, 0)
    i1_ref[...] = i1_s[...].T
    i2_ref[...] = i2_s[...].T
    g_ref[...] = g_s[...].T


def _peer_topk(h2, wq, keys, tm):
    n, d = h2.shape
    slots = PEER_HEADS * PEER_TOPK
    const2 = lambda i: (0, 0)
    row = lambda i: (i, 0)
    out = jax.ShapeDtypeStruct
    return pl.pallas_call(
        _peer_topk_kernel,
        grid=(n // tm,),
        in_specs=[pl.BlockSpec((tm, d), row),
                  pl.BlockSpec(wq.shape, const2, pipeline_mode=pl.Buffered(1)),
                  pl.BlockSpec(keys.shape, lambda i: (0, 0, 0))],
        out_specs=[pl.BlockSpec((tm, slots), row)] * 3,
        out_shape=[out((n, slots), I32), out((n, slots), I32), out((n, slots), F32)],
        scratch_shapes=[pltpu.VMEM((tm, wq.shape[1]), BF16), pltpu.VMEM((slots, tm), I32),
                        pltpu.VMEM((slots, tm), I32), pltpu.VMEM((slots, tm), F32)],
        compiler_params=_cparams("parallel"),
        name="peer_topk",
    )(h2, wq, keys)


def _peer_w_kernel(i1_ref, i2_ref, g_ref, w_ref, w3):
    nk = PEER_N_KEYS
    tb = i1_ref.shape[0]
    grp = 2 * SUBLANES
    rid = lax.broadcasted_iota(I32, (nk, i1_ref.shape[1]), 0)
    sub = lax.broadcasted_iota(I32, (SUBLANES, nk), 0)

    def rows_to_tokens(v):
        v = list(v)
        for dist in (4, 2, 1):
            keep = (sub & dist) == 0
            for k in range(SUBLANES):
                if k & dist == 0:
                    a, b_ = v[k], v[k + dist]
                    v[k] = jnp.where(keep, a, pltpu.roll(b_, dist, 0))
                    v[k + dist] = jnp.where(keep, pltpu.roll(a, SUBLANES - dist, 0), b_)
        return v

    def group(gi, carry):
        t0 = pl.multiple_of(gi * grp, grp)
        i1 = i1_ref[pl.ds(t0, grp), :]
        i2 = i2_ref[pl.ds(t0, grp), :]
        g = g_ref[pl.ds(t0, grp), :]
        for k in range(grp):
            left = jnp.where(rid == i1[k:k + 1], g[k:k + 1], 0.0).astype(BF16)
            right = jnp.where(rid == i2[k:k + 1], 1.0, 0.0).astype(BF16)
            w3[k * nk:(k + 1) * nk, :] = _dot_nt(left, right)
        for j in range(nk // SUBLANES):
            halves = []
            for half in range(grp // SUBLANES):
                tiles = [w3[(half * SUBLANES + k) * nk + j * SUBLANES:(half * SUBLANES + k) * nk + (j + 1) * SUBLANES, :]
                         for k in range(SUBLANES)]
                halves.append(rows_to_tokens(tiles))
            for i in range(SUBLANES):
                r = j * SUBLANES + i
                tile = jnp.concatenate([h[i] for h in halves], axis=0)
                w_ref[pl.ds(t0, grp), r * nk:(r + 1) * nk] = tile.astype(w_ref.dtype)
        return carry

    lax.fori_loop(0, tb // grp, group, 0)


def _peer_w(i1, i2, g, tb):
    n, slots = i1.shape
    ne = PEER_N_KEYS * PEER_N_KEYS
    row = lambda i: (i, 0)
    return pl.pallas_call(
        _peer_w_kernel,
        grid=(n // tb,),
        in_specs=[pl.BlockSpec((tb, slots), row)] * 3,
        out_specs=pl.BlockSpec((tb, ne), row),
        out_shape=jax.ShapeDtypeStruct((n, ne), BF16),
        scratch_shapes=[pltpu.VMEM((2 * SUBLANES * PEER_N_KEYS, PEER_N_KEYS), F32)],
        compiler_params=_cparams("parallel"),
        name="peer_w",
    )(i1, i2, g)


def _gelu_tanh(x):
    return 0.5 * x * (1.0 + jnp.tanh(0.7978845608028654 * (x + 0.044715 * (x * x * x))))


def _peer_dense_kernel(h_ref, u_ref, v_ref, w_ref, x_ref, gate_ref, nw_ref, o_ref, acc, a_buf):
    j = pl.program_id(1)
    nj = pl.num_programs(1) - 1

    def score(slot):
        a_buf[slot] = _dot_nt(h_ref[...], u_ref[...])

    def finish(slot):
        act = (_gelu_tanh(a_buf[slot]) * w_ref[...].astype(F32)).astype(BF16)
        acc[...] += jnp.dot(act, v_ref[...], preferred_element_type=F32)

    @pl.when(j == 0)
    def _():
        acc[...] = jnp.zeros_like(acc)
        score(0)

    for parity in range(2):
        @pl.when((j > 0) & (j < nj) & (j % 2 == parity))
        def _():
            score(parity)
            finish(1 - parity)

    @pl.when(j == nj)
    def _():
        finish((nj - 1) % 2)
        x2 = x_ref[...] + gate_ref[...] * acc[...]
        o_ref[...] = x2 * lax.rsqrt(jnp.mean(x2 * x2, axis=-1, keepdims=True) + EPS) * nw_ref[...]


def _peer_dense(h2, u, v, w, x1, gate, norm_f_w, seq, tm, te):
    n, d = h2.shape
    ne = u.shape[0]
    nj = ne // te
    assert nj % 2 == 0
    tiles_per_batch = seq // tm
    row = lambda i, j: (i, 0)
    scored = lambda i, j: (jnp.minimum(j, nj - 1), 0)
    finished = lambda j: jnp.maximum(j - 1, 0)
    return pl.pallas_call(
        _peer_dense_kernel,
        grid=(n // tm, nj + 1),
        in_specs=[pl.BlockSpec((tm, d), row),
                  pl.BlockSpec((te, d), scored),
                  pl.BlockSpec((te, d), lambda i, j: (finished(j), 0)),
                  pl.BlockSpec((tm, te), lambda i, j: (i, finished(j))),
                  pl.BlockSpec((tm, d), row),
                  pl.BlockSpec((None, 1, d), lambda i, j: (i // tiles_per_batch, 0, 0)),
                  pl.BlockSpec((1, d), lambda i, j: (0, 0))],
        out_specs=pl.BlockSpec((tm, d), row),
        out_shape=jax.ShapeDtypeStruct((n, d), F32),
        scratch_shapes=[pltpu.VMEM((tm, d), F32), pltpu.VMEM((2, tm, te), F32)],
        compiler_params=_cparams("parallel", "arbitrary"),
        name="peer_dense",
    )(h2, u, v, w, x1, gate, norm_f_w.reshape(1, d))


def _rope_tables(length):
    t = jnp.arange(length, dtype=jnp.int32)
    row = (t // GRID_W).astype(F32)
    col = (t % GRID_W).astype(F32)
    n_freq = HEAD_DIM // 4
    inv_freq = ROPE_BASE ** (-jnp.arange(n_freq, dtype=F32) / n_freq)
    ang = jnp.concatenate([row[:, None] * inv_freq, col[:, None] * inv_freq], axis=-1)
    cos, sin = jnp.cos(ang), jnp.sin(ang)
    return jnp.concatenate([cos, cos], axis=-1), jnp.concatenate([-sin, sin], axis=-1)


def _layer(x, ctx, c_rows, rope_c, rope_s, ada_w, ada_b, norm1_w, w_in, q_norm_w, k_norm_w, ret_decay_fwd,
           ret_decay_bwd, ret_norm_w, w_attn_branch, w_ret_branch, w_merge_out, norm2_w, peer_w_q, peer_keys,
           peer_u, peer_v, norm_f_w):
    b, l, d = x.shape
    lc = ctx.shape[1]
    mod = _ada(c_rows, ada_w, ada_b)
    mod_x = [mod[:b, i * d:(i + 1) * d].reshape(b, 1, d) for i in range(6)]
    mod_c = [mod[b:b + 1, i * d:(i + 1) * d].reshape(1, 1, d) for i in range(2)]

    w_in_b = _regroup_w_in(w_in).astype(BF16)
    n_in = w_in.shape[1]
    tn = n_in // 4
    p = _in_proj(x, norm1_w, mod_x[0], mod_x[1], w_in_b, tm=min(512, l), tn=tn)
    pc = _in_proj(ctx, norm1_w, mod_c[0], mod_c[1], w_in_b, tm=min(256, lc), tn=tn)

    attn = _attention(p, pc, rope_c, rope_s, q_norm_w, k_norm_w, tq=min(256, l))
    ret = _retention(p, pc, rope_c, rope_s, ret_decay_fwd, ret_decay_bwd, ret_norm_w)
    x1, h2 = _merge(attn, ret, p, x, mod_x[2], mod_x[3], mod_x[4], norm2_w, w_attn_branch.astype(BF16),
                    w_ret_branch.astype(BF16), w_merge_out.astype(BF16), tm=min(256, l))

    n = b * l
    h2f = h2.reshape(n, d)
    keys = peer_keys.reshape(PEER_HEADS * 2, PEER_N_KEYS, PEER_D_HALF).astype(BF16)
    i1, i2, g = _peer_topk(h2f, peer_w_q.astype(BF16), keys, tm=256)
    w = _peer_w(i1, i2, g, tb=128)
    out = _peer_dense(h2f, peer_u.astype(BF16), peer_v.astype(BF16), w, x1.reshape(n, d), mod_x[5], norm_f_w,
                      seq=l, tm=min(512, l), te=1024)
    return out.reshape(b, l, d)


def kernel(x, c, ctx, c_ctx, ada_w, ada_b, norm1_w, w_in, q_norm_w, k_norm_w, ret_decay_fwd, ret_decay_bwd,
           ret_norm_w, w_attn_branch, w_ret_branch, w_merge_out, norm2_w, peer_w_q, peer_keys, peer_u, peer_v,
           norm_f_w):
    depth = ada_w.shape[0]
    assert depth == 1, "context-stream update between layers is not implemented"
    b, l, d = x.shape
    rows = -(-(b + 1) // SUBLANES) * SUBLANES
    c_rows = jnp.zeros((rows, d), F32).at[:b].set(c).at[b].set(c_ctx)
    rope_c, rope_s = _rope_tables(l)
    return _layer(x, ctx, c_rows, rope_c, rope_s, ada_w[0], ada_b[0], norm1_w[0], w_in[0], q_norm_w[0], k_norm_w[0],
                  ret_decay_fwd[0], ret_decay_bwd[0], ret_norm_w[0], w_attn_branch[0], w_ret_branch[0],
                  w_merge_out[0], norm2_w[0], peer_w_q[0], peer_keys[0], peer_u[0], peer_v[0], norm_f_w)
```

```python
import functools

import jax
import jax.numpy as jnp
from jax import lax
from jax.experimental import pallas as pl
from jax.experimental.pallas import tpu as pltpu

F32 = jnp.float32
BF16 = jnp.bfloat16
I32 = jnp.int32

EPS = 1e-6
GRID_W = 64
ROPE_BASE = 10000.0
ATTN_HEADS = 8
ATTN_KV_HEADS = 2
HEAD_DIM = 128
RET_HEADS = 8
RET_DK = 128
RET_DV = 256
RET_CHUNK = 128
PEER_HEADS = 8
PEER_N_KEYS = 128
PEER_D_HALF = 128
PEER_TOPK = 16

LANES = 128
SUBLANES = 8
VMEM_LIMIT = 60 * 1024 * 1024

ATTN_Q_W = ATTN_HEADS * HEAD_DIM
ATTN_KV_W = ATTN_KV_HEADS * HEAD_DIM
RET_QK_W = RET_HEADS * RET_DK
RET_V_W = RET_HEADS * RET_DV
D_MODEL = 2048
_SRC_ORDER = (("qa", ATTN_Q_W), ("ka", ATTN_KV_W), ("va", ATTN_KV_W), ("qr", RET_QK_W), ("kr", RET_QK_W),
              ("vr", RET_V_W), ("gr", RET_V_W), ("ga", D_MODEL), ("gt", D_MODEL))
_DST_ORDER = ("ga", "gt", "vr", "gr", "qr", "kr", "qa", "ka", "va")


def _offsets(order, widths):
    off, o = {}, 0
    for name in order:
        off[name] = o
        o += widths[name]
    return off


_WIDTH = dict(_SRC_ORDER)
_SRC_OFF = _offsets([n for n, _ in _SRC_ORDER], _WIDTH)
OFF = _offsets(_DST_ORDER, _WIDTH)


def _regroup_w_in(w_in):
    return jnp.concatenate([w_in[:, _SRC_OFF[n]:_SRC_OFF[n] + _WIDTH[n]] for n in _DST_ORDER], axis=1)


def _cparams(*sem):
    return pltpu.CompilerParams(dimension_semantics=sem, vmem_limit_bytes=VMEM_LIMIT)


def _sigmoid(x):
    return 1.0 / (1.0 + jnp.exp(-x))


def _silu(x):
    return x * _sigmoid(x)


def _rot_half(x):
    return pltpu.roll(x, HEAD_DIM // 2, 1)


def _dot_nt(a, b):
    return lax.dot_general(a, b, (((1,), (1,)), ((), ())), preferred_element_type=F32)


def _dot_tn(a, b):
    return lax.dot_general(a, b, (((0,), (0,)), ((), ())), preferred_element_type=F32)


def _ada_kernel(c_ref, w_ref, b_ref, o_ref):
    sc = _silu(c_ref[...]).astype(BF16)
    o_ref[...] = jnp.dot(sc, w_ref[...].astype(BF16), preferred_element_type=F32) + b_ref[...]


def _ada(c_rows, ada_w, ada_b):
    rows, d = c_rows.shape
    n = ada_w.shape[1]
    tn = 1536
    return pl.pallas_call(
        _ada_kernel,
        grid=(n // tn,),
        in_specs=[pl.BlockSpec((rows, d), lambda j: (0, 0)),
                  pl.BlockSpec((d, tn), lambda j: (0, j)),
                  pl.BlockSpec((1, tn), lambda j: (0, j))],
        out_specs=pl.BlockSpec((rows, tn), lambda j: (0, j)),
        out_shape=jax.ShapeDtypeStruct((rows, n), F32),
        compiler_params=_cparams("arbitrary"),
        name="ada",
    )(c_rows, ada_w, ada_b.reshape(1, n))


def _in_proj_kernel(x_ref, nw_ref, shift_ref, scale_ref, w_ref, o_ref, h_ref):
    @pl.when(pl.program_id(2) == 0)
    def _():
        x = x_ref[...]
        y = x * lax.rsqrt(jnp.mean(x * x, axis=-1, keepdims=True) + EPS) * nw_ref[...]
        h_ref[...] = (y * (1.0 + scale_ref[...]) + shift_ref[...]).astype(BF16)

    o_ref[...] = jnp.dot(h_ref[...], w_ref[...], preferred_element_type=F32).astype(o_ref.dtype)


def _in_proj(x, norm_w, shift, scale, w, tm, tn):
    b, l, d = x.shape
    n = w.shape[1]
    per_batch = shift.shape[0] > 1
    mod_map = (lambda bi, i, j: (bi, 0, 0)) if per_batch else (lambda bi, i, j: (0, 0, 0))
    return pl.pallas_call(
        _in_proj_kernel,
        grid=(b, l // tm, n // tn),
        in_specs=[pl.BlockSpec((None, tm, d), lambda bi, i, j: (bi, i, 0)),
                  pl.BlockSpec((1, d), lambda bi, i, j: (0, 0)),
                  pl.BlockSpec((None, 1, d), mod_map),
                  pl.BlockSpec((None, 1, d), mod_map),
                  pl.BlockSpec((d, tn), lambda bi, i, j: (0, j))],
        out_specs=pl.BlockSpec((None, tm, tn), lambda bi, i, j: (bi, i, j)),
        out_shape=jax.ShapeDtypeStruct((b, l, n), BF16),
        scratch_shapes=[pltpu.VMEM((tm, d), BF16)],
        compiler_params=_cparams("parallel", "parallel", "arbitrary"),
        name="in_proj",
    )(x, norm_w.reshape(1, d), shift, scale, w)


def _head_rms(x, w):
    return x * lax.rsqrt(jnp.mean(x * x, axis=-1, keepdims=True) + EPS) * w


def _attn_kernel(q_ref, kc_ref, k_ref, vc_ref, v_ref, cq_ref, sq_ref, ck_ref, sk_ref, qnw_ref, knw_ref,
                 o_ref, k_s, v_s, *, lc, group):
    @pl.when(pl.program_id(2) == 0)
    def _():
        knw = knw_ref[...]
        k_s[0:lc, :] = _head_rms(kc_ref[...].astype(F32), knw).astype(BF16)
        kn = _head_rms(k_ref[...].astype(F32), knw)
        k_s[lc:, :] = (kn * ck_ref[...] + _rot_half(kn) * sk_ref[...]).astype(BF16)
        v_s[0:lc, 0:HEAD_DIM] = vc_ref[...]
        v_s[lc:, 0:HEAD_DIM] = v_ref[...]
        v_s[:, HEAD_DIM:] = jnp.ones((v_s.shape[0], HEAD_DIM), BF16)

    c = (HEAD_DIM ** -0.5) * 1.4426950408889634
    qnw = qnw_ref[...]
    cq = cq_ref[...] * c
    sq = sq_ref[...] * c
    for g in range(group):
        sl = slice(g * HEAD_DIM, (g + 1) * HEAD_DIM)
        qn = _head_rms(q_ref[:, sl].astype(F32), qnw)
        qr = (qn * cq + _rot_half(qn) * sq).astype(BF16)
        s = _dot_nt(qr, k_s[...])
        m = jnp.max(s, axis=-1, keepdims=True)
        p = jnp.exp2((s - m).astype(BF16))
        ol = jnp.dot(p, v_s[...], preferred_element_type=F32)
        o_ref[:, sl] = (ol[:, :HEAD_DIM] * (1.0 / ol[:, HEAD_DIM:HEAD_DIM + 1])).astype(o_ref.dtype)


def _attention(p, pc, rope_c, rope_s, q_norm_w, k_norm_w, tq):
    b, l, _ = p.shape
    lc = pc.shape[1]
    group = ATTN_HEADS // ATTN_KV_HEADS
    gw = group * HEAD_DIM
    kq0 = OFF["qa"] // gw
    kk0 = OFF["ka"] // HEAD_DIM
    kv0 = OFF["va"] // HEAD_DIM
    full = lambda bi, h, i: (0, 0)
    return pl.pallas_call(
        functools.partial(_attn_kernel, lc=lc, group=group),
        grid=(b, ATTN_KV_HEADS, l // tq),
        in_specs=[pl.BlockSpec((None, tq, gw), lambda bi, h, i: (bi, i, kq0 + h)),
                  pl.BlockSpec((None, lc, HEAD_DIM), lambda bi, h, i: (bi, 0, kk0 + h)),
                  pl.BlockSpec((None, l, HEAD_DIM), lambda bi, h, i: (bi, 0, kk0 + h)),
                  pl.BlockSpec((None, lc, HEAD_DIM), lambda bi, h, i: (bi, 0, kv0 + h)),
                  pl.BlockSpec((None, l, HEAD_DIM), lambda bi, h, i: (bi, 0, kv0 + h)),
                  pl.BlockSpec((tq, HEAD_DIM), lambda bi, h, i: (i, 0)),
                  pl.BlockSpec((tq, HEAD_DIM), lambda bi, h, i: (i, 0)),
                  pl.BlockSpec((l, HEAD_DIM), full),
                  pl.BlockSpec((l, HEAD_DIM), full),
                  pl.BlockSpec((1, HEAD_DIM), full),
                  pl.BlockSpec((1, HEAD_DIM), full)],
        out_specs=pl.BlockSpec((None, tq, gw), lambda bi, h, i: (bi, i, h)),
        out_shape=jax.ShapeDtypeStruct((b, l, ATTN_Q_W), BF16),
        scratch_shapes=[pltpu.VMEM((lc + l, HEAD_DIM), BF16), pltpu.VMEM((lc + l, 2 * HEAD_DIM), BF16)],
        compiler_params=_cparams("parallel", "parallel", "arbitrary"),
        name="attention",
    )(p, pc, p, pc, p, rope_c, rope_s, rope_c, rope_s, q_norm_w.reshape(1, HEAD_DIM), k_norm_w.reshape(1, HEAD_DIM))


def _ret_kernel(q_ref, k_ref, v_ref, g_ref, kc_ref, vc_ref, cos_ref, sin_ref, z_ref, nw_ref, o_ref,
                q_s, k_s, acc, st_f, st_b, *, nc, lc):
    ch = RET_CHUNK
    assert nc % 2 == 0
    scale = RET_DK ** -0.5
    lg_f = -jnp.exp(z_ref[0])
    lg_b = -jnp.exp(z_ref[1])
    row = lax.broadcasted_iota(I32, (ch, ch), 0).astype(F32)
    col = lax.broadcasted_iota(I32, (ch, ch), 1).astype(F32)
    d = row - col
    intra_f = jnp.where(d >= 0, jnp.exp(lg_f * jnp.maximum(d, 0.0)), 0.0)
    intra_b = jnp.where(d <= 0, jnp.exp(lg_b * jnp.maximum(-d, 0.0)), 0.0)

    def wide(a):
        return jnp.concatenate([a] * (RET_DV // LANES), axis=1)

    qd_f = wide(jnp.exp(lg_f * (row + 1.0)))
    kd_f = jnp.exp(lg_f * (ch - 1.0 - row))
    cd_f = wide(jnp.exp(lg_f * float(ch)))
    qd_b = wide(jnp.exp(lg_b * (ch - row)))
    kd_b = jnp.exp(lg_b * row)
    cd_b = wide(jnp.exp(lg_b * float(ch)))

    j = lax.broadcasted_iota(I32, (lc, RET_DK), 0).astype(F32)
    kc = kc_ref[...].astype(F32) * scale
    vc = vc_ref[...]
    st_f[...] = _dot_tn((kc * jnp.exp(lg_f * (lc - 1.0 - j))).astype(BF16), vc)
    st_b[...] = _dot_tn((kc * jnp.exp(lg_b * j)).astype(BF16), vc)

    def rope(c, carry):
        r0 = pl.multiple_of(c * ch, ch)
        cs = cos_ref[pl.ds(r0, ch), :]
        sn = sin_ref[pl.ds(r0, ch), :]
        qf = q_ref[pl.ds(r0, ch), :].astype(F32)
        kf = k_ref[pl.ds(r0, ch), :].astype(F32)
        q_s[pl.ds(r0, ch), :] = (qf * cs + _rot_half(qf) * sn).astype(BF16)
        k_s[pl.ds(r0, ch), :] = (kf * cs + _rot_half(kf) * sn) * scale
        return carry

    lax.fori_loop(0, nc, rope, 0, unroll=2)

    def scan_chunk(c, st, intra, qd, kd, cd):
        r0 = pl.multiple_of(c * ch, ch)
        qb = q_s[pl.ds(r0, ch), :]
        kr = k_s[pl.ds(r0, ch), :]
        vb = v_ref[pl.ds(r0, ch), :]
        state = st[...]
        s = _dot_nt(qb, kr.astype(BF16)) * intra
        o = jnp.dot(s.astype(BF16), vb, preferred_element_type=F32)
        o = o + jnp.dot(qb, state.astype(BF16), preferred_element_type=F32) * qd
        st[...] = state * cd + _dot_tn((kr * kd).astype(BF16), vb)
        return r0, o

    nw = nw_ref[...]

    def finish(r0, o):
        mu = jnp.mean(o, axis=-1, keepdims=True)
        oc = o - mu
        var = jnp.mean(oc * oc, axis=-1, keepdims=True)
        hn = oc * lax.rsqrt(var + EPS) * nw
        gate = g_ref[pl.ds(r0, ch), :].astype(F32)
        o_ref[pl.ds(r0, ch), :] = (_silu(gate) * hn).astype(o_ref.dtype)

    def first_half(i, carry):
        r0, o = scan_chunk(i, st_f, intra_f, qd_f, kd_f, cd_f)
        acc[pl.ds(r0, ch), :] = o
        r1, o1 = scan_chunk(nc - 1 - i, st_b, intra_b, qd_b, kd_b, cd_b)
        acc[pl.ds(r1, ch), :] = o1
        return carry

    def second_half(i, carry):
        r0, o = scan_chunk(i, st_f, intra_f, qd_f, kd_f, cd_f)
        finish(r0, acc[pl.ds(r0, ch), :] + o)
        r1, o1 = scan_chunk(nc - 1 - i, st_b, intra_b, qd_b, kd_b, cd_b)
        finish(r1, acc[pl.ds(r1, ch), :] + o1)
        return carry

    lax.fori_loop(0, nc // 2, first_half, 0, unroll=4)
    lax.fori_loop(nc // 2, nc, second_half, 0, unroll=4)


def _retention(p, pc, rope_c, rope_s, decay_fwd, decay_bwd, ret_norm_w):
    b, l, _ = p.shape
    lc = pc.shape[1]
    nc = l // RET_CHUNK
    z = jnp.broadcast_to(jnp.stack([decay_fwd, decay_bwd])[:, :, None, None], (2, RET_HEADS, 1, LANES))
    kq0, kk0 = OFF["qr"] // RET_DK, OFF["kr"] // RET_DK
    kv0, kg0 = OFF["vr"] // RET_DV, OFF["gr"] // RET_DV
    full = lambda bi, h: (0, 0)
    return pl.pallas_call(
        functools.partial(_ret_kernel, nc=nc, lc=lc),
        grid=(b, RET_HEADS),
        in_specs=[pl.BlockSpec((None, l, RET_DK), lambda bi, h: (bi, 0, kq0 + h)),
                  pl.BlockSpec((None, l, RET_DK), lambda bi, h: (bi, 0, kk0 + h)),
                  pl.BlockSpec((None, l, RET_DV), lambda bi, h: (bi, 0, kv0 + h)),
                  pl.BlockSpec((None, l, RET_DV), lambda bi, h: (bi, 0, kg0 + h)),
                  pl.BlockSpec((None, lc, RET_DK), lambda bi, h: (bi, 0, kk0 + h)),
                  pl.BlockSpec((None, lc, RET_DV), lambda bi, h: (bi, 0, kv0 + h)),
                  pl.BlockSpec((l, RET_DK), full),
                  pl.BlockSpec((l, RET_DK), full),
                  pl.BlockSpec((2, None, 1, LANES), lambda bi, h: (0, h, 0, 0)),
                  pl.BlockSpec((1, RET_DV), lambda bi, h: (0, h))],
        out_specs=pl.BlockSpec((None, l, RET_DV), lambda bi, h: (bi, 0, h)),
        out_shape=jax.ShapeDtypeStruct((b, l, RET_V_W), BF16),
        scratch_shapes=[pltpu.VMEM((l, RET_DK), BF16), pltpu.VMEM((l, RET_DK), F32), pltpu.VMEM((l, RET_DV), F32),
                        pltpu.VMEM((RET_DK, RET_DV), F32), pltpu.VMEM((RET_DK, RET_DV), F32)],
        compiler_params=_cparams("parallel", "parallel"),
        name="retention",
    )(p, p, p, p, pc, pc, rope_c, rope_s, z, ret_norm_w.reshape(1, RET_V_W))


def _merge_kernel(attn_ref, ret_ref, ga_ref, gr_ref, x_ref, gate_ref, shift_ref, scale_ref, nw_ref,
                  wa_ref, wr_ref, wm_ref, x1_ref, h2_ref):
    a = jnp.dot(attn_ref[...], wa_ref[...], preferred_element_type=F32)
    r = jnp.dot(ret_ref[...], wr_ref[...], preferred_element_type=F32)
    mixed = _sigmoid(ga_ref[...].astype(F32)) * a + _sigmoid(gr_ref[...].astype(F32)) * r
    y = jnp.dot(mixed.astype(BF16), wm_ref[...], preferred_element_type=F32)
    x1 = x_ref[...] + gate_ref[...] * y
    x1_ref[...] = x1
    hn = x1 * lax.rsqrt(jnp.mean(x1 * x1, axis=-1, keepdims=True) + EPS) * nw_ref[...]
    h2_ref[...] = (hn * (1.0 + scale_ref[...]) + shift_ref[...]).astype(BF16)


def _merge(attn, ret, p, x, gate, shift2, scale2, norm2_w, wa, wr, wm, tm):
    b, l, d = x.shape
    assert d == D_MODEL
    kga, kgt = OFF["ga"] // d, OFF["gt"] // d
    row = lambda bi, i: (bi, i, 0)
    mod = lambda bi, i: (bi, 0, 0)
    const = lambda bi, i: (0, 0)
    once = pl.Buffered(1)
    return pl.pallas_call(
        _merge_kernel,
        grid=(b, l // tm),
        in_specs=[pl.BlockSpec((None, tm, ATTN_Q_W), row),
                  pl.BlockSpec((None, tm, RET_V_W), row),
                  pl.BlockSpec((None, tm, d), lambda bi, i: (bi, i, kga)),
                  pl.BlockSpec((None, tm, d), lambda bi, i: (bi, i, kgt)),
                  pl.BlockSpec((None, tm, d), row),
                  pl.BlockSpec((None, 1, d), mod),
                  pl.BlockSpec((None, 1, d), mod),
                  pl.BlockSpec((None, 1, d), mod),
                  pl.BlockSpec((1, d), const),
                  pl.BlockSpec(wa.shape, const, pipeline_mode=once),
                  pl.BlockSpec(wr.shape, const, pipeline_mode=once),
                  pl.BlockSpec(wm.shape, const, pipeline_mode=once)],
        out_specs=[pl.BlockSpec((None, tm, d), row), pl.BlockSpec((None, tm, d), row)],
        out_shape=[jax.ShapeDtypeStruct((b, l, d), F32), jax.ShapeDtypeStruct((b, l, d), BF16)],
        compiler_params=_cparams("parallel", "parallel"),
        name="merge",
    )(attn, ret, p, p, x, gate, shift2, scale2, norm2_w.reshape(1, d), wa, wr, wm)


def _run(gen):
    try:
        while True:
            next(gen)
    except StopIteration as stop:
        return stop.value


def _top_rows_gen(s, k):
    n = s.shape[0]
    rid = lax.broadcasted_iota(I32, s.shape, 0)
    vals, ids = [], []
    for _ in range(k):
        m = jnp.max(s, axis=0, keepdims=True)
        sel = jnp.min(jnp.where(s == m, rid, n), axis=0, keepdims=True)
        vals.append(m)
        ids.append(sel)
        s = jnp.where(rid == sel, -jnp.inf, s)
        yield
    return jnp.concatenate(vals, axis=0), jnp.concatenate(ids, axis=0)


def _top_rows(s, k):
    return _run(_top_rows_gen(s, k))


def _pair_candidates(s1, s2):
    k = PEER_TOPK
    t = s1.shape[1]
    r8 = lax.broadcasted_iota(I32, (SUBLANES, t), 0)
    r16 = lax.broadcasted_iota(I32, (k, t), 0)
    neg = -jnp.inf
    sums = [s1[0:1] + s2]
    flat = [r16]
    for a, nb in ((1, 8), (2, 5), (3, 4)):
        sums.append(jnp.where(r8 < nb, s1[a:a + 1] + s2[0:8], neg))
        flat.append(a * k + r8)
    sums.append(s1[8:16] + s2[0:1])
    flat.append((r8 + 8) * k)
    for b_, lo, hi in ((0, 4, 8), (1, 4, 8), (2, 4, 5)):
        sums.append(jnp.where((r8 >= lo) & (r8 < hi), s1[0:8] + s2[b_:b_ + 1], neg))
        flat.append(r8 * k + b_)
    return jnp.concatenate(sums, axis=0), jnp.concatenate(flat, axis=0)


def _pair_topk_gen(s1, n1, s2, n2):
    k = PEER_TOPK
    cand, flat = _pair_candidates(s1, s2)
    tops, picks = [], []
    for _ in range(k):
        m = jnp.max(cand, axis=0, keepdims=True)
        f = jnp.min(jnp.where(cand == m, flat, k * k), axis=0, keepdims=True)
        cand = jnp.where(flat == f, -jnp.inf, cand)
        tops.append(m)
        picks.append(f)
        yield
    top = jnp.concatenate(tops, axis=0)
    pick = jnp.concatenate(picks, axis=0)
    pa = pick // k
    pb = pick - pa * k
    e1 = jnp.zeros_like(pick)
    e2 = jnp.zeros_like(pick)
    for a in range(k):
        e1 = jnp.where(pa == a, n1[a:a + 1], e1)
        e2 = jnp.where(pb == a, n2[a:a + 1], e2)
    ex = jnp.exp(top - top[0:1])
    return e1, e2, ex / jnp.sum(ex, axis=0, keepdims=True)


def _pair_topk(s1, n1, s2, n2):
    return _run(_pair_topk_gen(s1, n1, s2, n2))


def _peer_q_kernel(h_ref, wq_ref, q_ref):
    q_ref[...] = jnp.dot(h_ref[...], wq_ref[...], preferred_element_type=F32).astype(q_ref.dtype)


def _peer_q(h2, wq, tm):
    n, d = h2.shape
    return pl.pallas_call(
        _peer_q_kernel,
        grid=(n // tm,),
        in_specs=[pl.BlockSpec((tm, d), lambda i: (i, 0)),
                  pl.BlockSpec(wq.shape, lambda i: (0, 0), pipeline_mode=pl.Buffered(1))],
        out_specs=pl.BlockSpec((tm, wq.shape[1]), lambda i: (i, 0)),
        out_shape=jax.ShapeDtypeStruct((n, wq.shape[1]), BF16),
        compiler_params=_cparams("parallel"),
        name="peer_q",
    )(h2, wq)


def _peer_topk_kernel(q_s, keys_ref, i1_ref, i2_ref, g_ref, i1_s, i2_s, g_s):
    k = PEER_TOPK

    def head(h, carry):
        c0 = pl.multiple_of(h * 2 * PEER_D_HALF, 2 * PEER_D_HALF)
        qh = q_s[:, pl.ds(c0, 2 * PEER_D_HALF)]
        s1, n1 = _top_rows(_dot_nt(keys_ref[2 * h], qh[:, :PEER_D_HALF]), k)
        s2, n2 = _top_rows(_dot_nt(keys_ref[2 * h + 1], qh[:, PEER_D_HALF:]), k)
        e1, e2, gates = _pair_topk(s1, n1, s2, n2)
        r0 = pl.multiple_of(h * k, k)
        g_s[pl.ds(r0, k), :] = gates
        i1_s[pl.ds(r0, k), :] = e1
        i2_s[pl.ds(r0, k), :] = e2
        return carry

    lax.fori_loop(0, PEER_HEADS, head, 0)
    i1_ref[...] = i1_s[...].T
    i2_ref[...] = i2_s[...].T
    g_ref[...] = g_s[...].T


def _peer_topk(q, keys, n, tm):
    slots = PEER_HEADS * PEER_TOPK
    row = lambda i: (i, 0)
    out = jax.ShapeDtypeStruct
    return pl.pallas_call(
        _peer_topk_kernel,
        grid=(n // tm,),
        in_specs=[pl.BlockSpec((tm, q.shape[1]), row),
                  pl.BlockSpec(keys.shape, lambda i: (0, 0, 0))],
        out_specs=[pl.BlockSpec((tm, slots), row)] * 3,
        out_shape=[out((n, slots), I32), out((n, slots), I32), out((n, slots), F32)],
        scratch_shapes=[pltpu.VMEM((slots, tm), I32), pltpu.VMEM((slots, tm), I32), pltpu.VMEM((slots, tm), F32)],
        compiler_params=_cparams("parallel"),
        name="peer_topk",
    )(q, keys)


def _peer_w_kernel(i1_ref, i2_ref, g_ref, w_ref, w3):
    nk = PEER_N_KEYS
    tb = i1_ref.shape[0]
    grp = 2 * SUBLANES
    rid = lax.broadcasted_iota(I32, (nk, i1_ref.shape[1]), 0)
    sub = lax.broadcasted_iota(I32, (SUBLANES, nk), 0)

    def rows_to_tokens(v):
        v = list(v)
        for dist in (4, 2, 1):
            keep = (sub & dist) == 0
            for k in range(SUBLANES):
                if k & dist == 0:
                    a, b_ = v[k], v[k + dist]
                    v[k] = jnp.where(keep, a, pltpu.roll(b_, dist, 0))
                    v[k + dist] = jnp.where(keep, pltpu.roll(a, SUBLANES - dist, 0), b_)
        return v

    def group(gi, carry):
        t0 = pl.multiple_of(gi * grp, grp)
        i1 = i1_ref[pl.ds(t0, grp), :]
        i2 = i2_ref[pl.ds(t0, grp), :]
        g = g_ref[pl.ds(t0, grp), :]
        for k in range(grp):
            left = jnp.where(rid == i1[k:k + 1], g[k:k + 1], 0.0).astype(BF16)
            right = jnp.where(rid == i2[k:k + 1], 1.0, 0.0).astype(BF16)
            w3[k * nk:(k + 1) * nk, :] = _dot_nt(left, right)
        for j in range(nk // SUBLANES):
            halves = []
            for half in range(grp // SUBLANES):
                tiles = [w3[(half * SUBLANES + k) * nk + j * SUBLANES:(half * SUBLANES + k) * nk + (j + 1) * SUBLANES, :]
                         for k in range(SUBLANES)]
                halves.append(rows_to_tokens(tiles))
            for i in range(SUBLANES):
                r = j * SUBLANES + i
                tile = jnp.concatenate([h[i] for h in halves], axis=0)
                w_ref[pl.ds(t0, grp), r * nk:(r + 1) * nk] = tile.astype(w_ref.dtype)
        return carry

    lax.fori_loop(0, tb // grp, group, 0)


def _peer_w(i1, i2, g, tb):
    n, slots = i1.shape
    ne = PEER_N_KEYS * PEER_N_KEYS
    row = lambda i: (i, 0)
    return pl.pallas_call(
        _peer_w_kernel,
        grid=(n // tb,),
        in_specs=[pl.BlockSpec((tb, slots), row)] * 3,
        out_specs=pl.BlockSpec((tb, ne), row),
        out_shape=jax.ShapeDtypeStruct((n, ne), BF16),
        scratch_shapes=[pltpu.VMEM((2 * SUBLANES * PEER_N_KEYS, PEER_N_KEYS), F32)],
        compiler_params=_cparams("parallel"),
        name="peer_w",
    )(i1, i2, g)


def _gelu_tanh(x):
    return 0.5 * x * (1.0 + jnp.tanh(0.7978845608028654 * (x + 0.044715 * (x * x * x))))


def _peer_dense_kernel(*refs, retrieve_next):
    if retrieve_next:
        (h_ref, u_ref, v_ref, w_ref, x_ref, gate_ref, nw_ref, qn_ref, keys_ref,
         o_ref, i1_ref, i2_ref, g_ref, acc, a_buf, s1_s, n1_s, i1_s, i2_s, g_s) = refs
    else:
        h_ref, u_ref, v_ref, w_ref, x_ref, gate_ref, nw_ref, o_ref, acc, a_buf = refs
    j = pl.program_id(1)
    nj = pl.num_programs(1) - 1
    k = PEER_TOPK

    def score(slot):
        a_buf[slot] = jnp.dot(h_ref[...], u_ref[...], preferred_element_type=F32)

    def finish(slot):
        act = (_gelu_tanh(a_buf[slot]) * w_ref[...].astype(F32)).astype(BF16)
        acc[...] += jnp.dot(act, v_ref[...], preferred_element_type=F32)

    def retrieval_chains(half):
        if not retrieve_next:
            return [], 0
        r0 = pl.multiple_of((j // 2) * k, k)

        def chain(t0):
            tl = slice(t0, t0 + LANES)
            s, n = yield from _top_rows_gen(_dot_nt(keys_ref[...], qn_ref[tl, :]), k)
            if half == 0:
                s1_s[:, tl] = s
                n1_s[:, tl] = n
            else:
                e1, e2, gates = yield from _pair_topk_gen(s1_s[:, tl], n1_s[:, tl], s, n)
                i1_s[pl.ds(r0, k), tl] = e1
                i2_s[pl.ds(r0, k), tl] = e2
                g_s[pl.ds(r0, k), tl] = gates

        rounds = (k if half == 0 else 2 * k) + 1
        return [chain(t0) for t0 in range(0, qn_ref.shape[0], LANES)], rounds

    def interleave(main, filler):
        chains, rounds = filler
        units = [c for c in chains for _ in range(rounds)]
        done = 0
        for idx, piece in enumerate(main):
            upto = (idx + 1) * len(units) // len(main)
            for c in units[done:upto]:
                next(c, None)
            done = upto
            piece()

    def score_pieces(slot):
        half_n = a_buf.shape[2] // 2

        def piece(c0):
            a_buf[slot, :, c0:c0 + half_n] = jnp.dot(h_ref[...], u_ref[:, c0:c0 + half_n],
                                                     preferred_element_type=F32)
        return [functools.partial(piece, c0) for c0 in (0, half_n)]

    def finish_pieces(slot):
        quarter = acc.shape[1] // 4
        cell = {}

        def piece(c0):
            if "act" not in cell:
                cell["act"] = (_gelu_tanh(a_buf[slot]) * w_ref[...].astype(F32)).astype(BF16)
            acc[:, c0:c0 + quarter] += jnp.dot(cell["act"], v_ref[:, c0:c0 + quarter], preferred_element_type=F32)
        return [functools.partial(piece, c0) for c0 in range(0, acc.shape[1], quarter)]

    @pl.when(j == 0)
    def _():
        acc[...] = jnp.zeros_like(acc)
        interleave(score_pieces(0), retrieval_chains(0))

    for parity in range(2):
        @pl.when((j > 0) & (j < nj) & (j % 2 == parity))
        def _():
            interleave(score_pieces(parity) + finish_pieces(1 - parity), retrieval_chains(parity))

    @pl.when(j == nj)
    def _():
        finish((nj - 1) % 2)
        x2 = x_ref[...] + gate_ref[...] * acc[...]
        o_ref[...] = x2 * lax.rsqrt(jnp.mean(x2 * x2, axis=-1, keepdims=True) + EPS) * nw_ref[...]
        if retrieve_next:
            i1_ref[...] = i1_s[...].T
            i2_ref[...] = i2_s[...].T
            g_ref[...] = g_s[...].T


def _peer_dense(h2, u, v, w, x1, gate, norm_f_w, seq, tm, te, tile0, n_tiles, retrieval=None):
    n, d = h2.shape
    ne = v.shape[0]
    nj = ne // te
    assert nj % 2 == 0
    tiles_per_batch = seq // tm
    row = lambda i, j: (tile0 + i, 0)
    scored = lambda i, j: (0, jnp.minimum(j, nj - 1))
    finished = lambda j: jnp.maximum(j - 1, 0)
    in_specs = [pl.BlockSpec((tm, d), row),
                pl.BlockSpec((d, te), scored),
                pl.BlockSpec((te, d), lambda i, j: (finished(j), 0)),
                pl.BlockSpec((tm, te), lambda i, j: (i, finished(j))),
                pl.BlockSpec((tm, d), row),
                pl.BlockSpec((None, 1, d), lambda i, j: ((tile0 + i) // tiles_per_batch, 0, 0)),
                pl.BlockSpec((1, d), lambda i, j: (0, 0))]
    args = [h2, u, v, w, x1, gate, norm_f_w.reshape(1, d)]
    out_specs = [pl.BlockSpec((tm, d), row)]
    out_shape = [jax.ShapeDtypeStruct((n, d), F32)]
    scratch = [pltpu.VMEM((tm, d), F32), pltpu.VMEM((2, tm, te), F32)]
    if retrieval is not None:
        q, keys = retrieval
        slots = PEER_HEADS * PEER_TOPK
        assert nj == keys.shape[0]
        half_head = lambda j: jnp.minimum(j, nj - 1)
        in_specs += [pl.BlockSpec((tm, PEER_D_HALF), lambda i, j: (tile0 + n_tiles + i, half_head(j))),
                     pl.BlockSpec((None, PEER_N_KEYS, PEER_D_HALF), lambda i, j: (half_head(j), 0, 0))]
        args += [q, keys]
        out_specs += [pl.BlockSpec((tm, slots), lambda i, j: (i, 0))] * 3
        out_shape += [jax.ShapeDtypeStruct((n_tiles * tm, slots), t) for t in (I32, I32, F32)]
        scratch += [pltpu.VMEM((PEER_TOPK, tm), F32), pltpu.VMEM((PEER_TOPK, tm), I32),
                    pltpu.VMEM((slots, tm), I32), pltpu.VMEM((slots, tm), I32), pltpu.VMEM((slots, tm), F32)]
    outs = pl.pallas_call(
        functools.partial(_peer_dense_kernel, retrieve_next=retrieval is not None),
        grid=(n_tiles, nj + 1),
        in_specs=in_specs,
        out_specs=out_specs,
        out_shape=out_shape,
        scratch_shapes=scratch,
        input_output_aliases={4: 0},
        compiler_params=_cparams("parallel", "arbitrary"),
        name="peer_dense",
    )(*args)
    return outs[0], tuple(outs[1:])


def _rope_tables(length):
    t = jnp.arange(length, dtype=jnp.int32)
    row = (t // GRID_W).astype(F32)
    col = (t % GRID_W).astype(F32)
    n_freq = HEAD_DIM // 4
    inv_freq = ROPE_BASE ** (-jnp.arange(n_freq, dtype=F32) / n_freq)
    ang = jnp.concatenate([row[:, None] * inv_freq, col[:, None] * inv_freq], axis=-1)
    cos, sin = jnp.cos(ang), jnp.sin(ang)
    return jnp.concatenate([cos, cos], axis=-1), jnp.concatenate([-sin, sin], axis=-1)


def _layer(x, ctx, c_rows, rope_c, rope_s, ada_w, ada_b, norm1_w, w_in, q_norm_w, k_norm_w, ret_decay_fwd,
           ret_decay_bwd, ret_norm_w, w_attn_branch, w_ret_branch, w_merge_out, norm2_w, peer_w_q, peer_keys,
           peer_u, peer_v, norm_f_w):
    b, l, d = x.shape
    lc = ctx.shape[1]
    mod = _ada(c_rows, ada_w, ada_b)
    mod_x = [mod[:b, i * d:(i + 1) * d].reshape(b, 1, d) for i in range(6)]
    mod_c = [mod[b:b + 1, i * d:(i + 1) * d].reshape(1, 1, d) for i in range(2)]

    w_in_b = _regroup_w_in(w_in).astype(BF16)
    n_in = w_in.shape[1]
    tn = n_in // 4
    p = _in_proj(x, norm1_w, mod_x[0], mod_x[1], w_in_b, tm=min(512, l), tn=tn)
    pc = _in_proj(ctx, norm1_w, mod_c[0], mod_c[1], w_in_b, tm=min(256, lc), tn=tn)

    attn = _attention(p, pc, rope_c, rope_s, q_norm_w, k_norm_w, tq=min(256, l))
    ret = _retention(p, pc, rope_c, rope_s, ret_decay_fwd, ret_decay_bwd, ret_norm_w)
    x1, h2 = _merge(attn, ret, p, x, mod_x[2], mod_x[3], mod_x[4], norm2_w, w_attn_branch.astype(BF16),
                    w_ret_branch.astype(BF16), w_merge_out.astype(BF16), tm=min(256, l))

    n = b * l
    h2f = h2.reshape(n, d)
    keys = peer_keys.reshape(PEER_HEADS * 2, PEER_N_KEYS, PEER_D_HALF).astype(BF16)
    q = _peer_q(h2f, peer_w_q.astype(BF16), tm=min(1024, l))
    u, v = peer_u.astype(BF16).T, peer_v.astype(BF16)
    tm = min(512, l)
    n_tiles = n // tm
    n_chunks = 4 if n_tiles % 4 == 0 else (2 if n_tiles % 2 == 0 else 1)
    tiles = n_tiles // n_chunks
    sel = _peer_topk(q, keys, n=tiles * tm, tm=256)
    out = x1.reshape(n, d)
    for chunk in range(n_chunks):
        w = _peer_w(*sel, tb=128)
        retrieval = (q, keys) if chunk + 1 < n_chunks else None
        out, sel = _peer_dense(h2f, u, v, w, out, mod_x[5], norm_f_w, seq=l, tm=tm, te=1024,
                               tile0=chunk * tiles, n_tiles=tiles, retrieval=retrieval)
    return out.reshape(b, l, d)


def kernel(x, c, ctx, c_ctx, ada_w, ada_b, norm1_w, w_in, q_norm_w, k_norm_w, ret_decay_fwd, ret_decay_bwd,
           ret_norm_w, w_attn_branch, w_ret_branch, w_merge_out, norm2_w, peer_w_q, peer_keys, peer_u, peer_v,
           norm_f_w):
    depth = ada_w.shape[0]
    assert depth == 1, "context-stream update between layers is not implemented"
    b, l, d = x.shape
    rows = -(-(b + 1) // SUBLANES) * SUBLANES
    c_rows = jnp.zeros((rows, d), F32).at[:b].set(c).at[b].set(c_ctx)
    rope_c, rope_s = _rope_tables(l)
    return _layer(x, ctx, c_rows, rope_c, rope_s, ada_w[0], ada_b[0], norm1_w[0], w_in[0], q_norm_w[0], k_norm_w[0],
                  ret_decay_fwd[0], ret_decay_bwd[0], ret_norm_w[0], w_attn_branch[0], w_ret_branch[0],
                  w_merge_out[0], norm2_w[0], peer_w_q[0], peer_keys[0], peer_u[0], peer_v[0], norm_f_w)
```

```python
import functools

import jax
import jax.numpy as jnp
from jax import lax
from jax.experimental import pallas as pl
from jax.experimental.pallas import tpu as pltpu

F32 = jnp.float32
BF16 = jnp.bfloat16
I32 = jnp.int32

EPS = 1e-6
GRID_W = 64
ROPE_BASE = 10000.0
ATTN_HEADS = 8
ATTN_KV_HEADS = 2
HEAD_DIM = 128
RET_HEADS = 8
RET_DK = 128
RET_DV = 256
RET_CHUNK = 128
PEER_HEADS = 8
PEER_N_KEYS = 128
PEER_D_HALF = 128
PEER_TOPK = 16

LANES = 128
SUBLANES = 8
VMEM_LIMIT = 60 * 1024 * 1024

ATTN_Q_W = ATTN_HEADS * HEAD_DIM
ATTN_KV_W = ATTN_KV_HEADS * HEAD_DIM
RET_QK_W = RET_HEADS * RET_DK
RET_V_W = RET_HEADS * RET_DV
D_MODEL = 2048
_SRC_ORDER = (("qa", ATTN_Q_W), ("ka", ATTN_KV_W), ("va", ATTN_KV_W), ("qr", RET_QK_W), ("kr", RET_QK_W),
              ("vr", RET_V_W), ("gr", RET_V_W), ("ga", D_MODEL), ("gt", D_MODEL))
_DST_ORDER = ("ga", "gt", "vr", "gr", "qr", "kr", "qa", "ka", "va")


def _offsets(order, widths):
    off, o = {}, 0
    for name in order:
        off[name] = o
        o += widths[name]
    return off


_WIDTH = dict(_SRC_ORDER)
_SRC_OFF = _offsets([n for n, _ in _SRC_ORDER], _WIDTH)
OFF = _offsets(_DST_ORDER, _WIDTH)


def _regroup_w_in(w_in):
    return jnp.concatenate([w_in[:, _SRC_OFF[n]:_SRC_OFF[n] + _WIDTH[n]] for n in _DST_ORDER], axis=1)


def _cparams(*sem):
    return pltpu.CompilerParams(dimension_semantics=sem, vmem_limit_bytes=VMEM_LIMIT)


def _sigmoid(x):
    return 1.0 / (1.0 + jnp.exp(-x))


def _silu(x):
    return x * _sigmoid(x)


def _rot_half(x):
    return pltpu.roll(x, HEAD_DIM // 2, 1)


def _dot_nt(a, b):
    return lax.dot_general(a, b, (((1,), (1,)), ((), ())), preferred_element_type=F32)


def _dot_tn(a, b):
    return lax.dot_general(a, b, (((0,), (0,)), ((), ())), preferred_element_type=F32)


def _ada_kernel(c_ref, w_ref, b_ref, o_ref):
    sc = _silu(c_ref[...]).astype(BF16)
    o_ref[...] = jnp.dot(sc, w_ref[...].astype(BF16), preferred_element_type=F32) + b_ref[...]


def _ada(c_rows, ada_w, ada_b):
    rows, d = c_rows.shape
    n = ada_w.shape[1]
    tn = 1536
    return pl.pallas_call(
        _ada_kernel,
        grid=(n // tn,),
        in_specs=[pl.BlockSpec((rows, d), lambda j: (0, 0)),
                  pl.BlockSpec((d, tn), lambda j: (0, j)),
                  pl.BlockSpec((1, tn), lambda j: (0, j))],
        out_specs=pl.BlockSpec((rows, tn), lambda j: (0, j)),
        out_shape=jax.ShapeDtypeStruct((rows, n), F32),
        compiler_params=_cparams("arbitrary"),
        name="ada",
    )(c_rows, ada_w, ada_b.reshape(1, n))


def _in_proj_kernel(x_ref, nw_ref, shift_ref, scale_ref, w_ref, o_ref, h_ref):
    @pl.when(pl.program_id(2) == 0)
    def _():
        x = x_ref[...]
        y = x * lax.rsqrt(jnp.mean(x * x, axis=-1, keepdims=True) + EPS) * nw_ref[...]
        h_ref[...] = (y * (1.0 + scale_ref[...]) + shift_ref[...]).astype(BF16)

    o_ref[...] = jnp.dot(h_ref[...], w_ref[...], preferred_element_type=F32).astype(o_ref.dtype)


def _in_proj(x, norm_w, shift, scale, w, tm, tn):
    b, l, d = x.shape
    n = w.shape[1]
    per_batch = shift.shape[0] > 1
    mod_map = (lambda bi, i, j: (bi, 0, 0)) if per_batch else (lambda bi, i, j: (0, 0, 0))
    return pl.pallas_call(
        _in_proj_kernel,
        grid=(b, l // tm, n // tn),
        in_specs=[pl.BlockSpec((None, tm, d), lambda bi, i, j: (bi, i, 0)),
                  pl.BlockSpec((1, d), lambda bi, i, j: (0, 0)),
                  pl.BlockSpec((None, 1, d), mod_map),
                  pl.BlockSpec((None, 1, d), mod_map),
                  pl.BlockSpec((d, tn), lambda bi, i, j: (0, j))],
        out_specs=pl.BlockSpec((None, tm, tn), lambda bi, i, j: (bi, i, j)),
        out_shape=jax.ShapeDtypeStruct((b, l, n), BF16),
        scratch_shapes=[pltpu.VMEM((tm, d), BF16)],
        compiler_params=_cparams("parallel", "parallel", "arbitrary"),
        name="in_proj",
    )(x, norm_w.reshape(1, d), shift, scale, w)


def _head_rms(x, w):
    return x * lax.rsqrt(jnp.mean(x * x, axis=-1, keepdims=True) + EPS) * w


def _attn_kernel(q_ref, kc_ref, k_ref, vc_ref, v_ref, cq_ref, sq_ref, ck_ref, sk_ref, qnw_ref, knw_ref,
                 o_ref, k_s, v_s, *, lc, group):
    @pl.when(pl.program_id(2) == 0)
    def _():
        knw = knw_ref[...]
        k_s[0:lc, :] = _head_rms(kc_ref[...].astype(F32), knw).astype(BF16)
        kn = _head_rms(k_ref[...].astype(F32), knw)
        k_s[lc:, :] = (kn * ck_ref[...] + _rot_half(kn) * sk_ref[...]).astype(BF16)
        v_s[0:lc, 0:HEAD_DIM] = vc_ref[...]
        v_s[lc:, 0:HEAD_DIM] = v_ref[...]
        v_s[:, HEAD_DIM:] = jnp.ones((v_s.shape[0], HEAD_DIM), BF16)

    c = (HEAD_DIM ** -0.5) * 1.4426950408889634
    qnw = qnw_ref[...]
    cq = cq_ref[...] * c
    sq = sq_ref[...] * c
    for g in range(group):
        sl = slice(g * HEAD_DIM, (g + 1) * HEAD_DIM)
        qn = _head_rms(q_ref[:, sl].astype(F32), qnw)
        qr = (qn * cq + _rot_half(qn) * sq).astype(BF16)
        s = _dot_nt(qr, k_s[...])
        m = jnp.max(s, axis=-1, keepdims=True)
        p = jnp.exp2((s - m).astype(BF16))
        ol = jnp.dot(p, v_s[...], preferred_element_type=F32)
        o_ref[:, sl] = (ol[:, :HEAD_DIM] * (1.0 / ol[:, HEAD_DIM:HEAD_DIM + 1])).astype(o_ref.dtype)


def _attention(p, pc, rope_c, rope_s, q_norm_w, k_norm_w, tq):
    b, l, _ = p.shape
    lc = pc.shape[1]
    group = ATTN_HEADS // ATTN_KV_HEADS
    gw = group * HEAD_DIM
    kq0 = OFF["qa"] // gw
    kk0 = OFF["ka"] // HEAD_DIM
    kv0 = OFF["va"] // HEAD_DIM
    full = lambda bi, h, i: (0, 0)
    return pl.pallas_call(
        functools.partial(_attn_kernel, lc=lc, group=group),
        grid=(b, ATTN_KV_HEADS, l // tq),
        in_specs=[pl.BlockSpec((None, tq, gw), lambda bi, h, i: (bi, i, kq0 + h)),
                  pl.BlockSpec((None, lc, HEAD_DIM), lambda bi, h, i: (bi, 0, kk0 + h)),
                  pl.BlockSpec((None, l, HEAD_DIM), lambda bi, h, i: (bi, 0, kk0 + h)),
                  pl.BlockSpec((None, lc, HEAD_DIM), lambda bi, h, i: (bi, 0, kv0 + h)),
                  pl.BlockSpec((None, l, HEAD_DIM), lambda bi, h, i: (bi, 0, kv0 + h)),
                  pl.BlockSpec((tq, HEAD_DIM), lambda bi, h, i: (i, 0)),
                  pl.BlockSpec((tq, HEAD_DIM), lambda bi, h, i: (i, 0)),
                  pl.BlockSpec((l, HEAD_DIM), full),
                  pl.BlockSpec((l, HEAD_DIM), full),
                  pl.BlockSpec((1, HEAD_DIM), full),
                  pl.BlockSpec((1, HEAD_DIM), full)],
        out_specs=pl.BlockSpec((None, tq, gw), lambda bi, h, i: (bi, i, h)),
        out_shape=jax.ShapeDtypeStruct((b, l, ATTN_Q_W), BF16),
        scratch_shapes=[pltpu.VMEM((lc + l, HEAD_DIM), BF16), pltpu.VMEM((lc + l, 2 * HEAD_DIM), BF16)],
        compiler_params=_cparams("parallel", "parallel", "arbitrary"),
        name="attention",
    )(p, pc, p, pc, p, rope_c, rope_s, rope_c, rope_s, q_norm_w.reshape(1, HEAD_DIM), k_norm_w.reshape(1, HEAD_DIM))


def _ret_kernel(q_ref, k_ref, v_ref, g_ref, kc_ref, vc_ref, cos_ref, sin_ref, z_ref, nw_ref, o_ref,
                q_s, k_s, acc, st_f, st_b, *, nc, lc):
    ch = RET_CHUNK
    assert nc % 2 == 0
    scale = RET_DK ** -0.5
    lg_f = -jnp.exp(z_ref[0])
    lg_b = -jnp.exp(z_ref[1])
    row = lax.broadcasted_iota(I32, (ch, ch), 0).astype(F32)
    col = lax.broadcasted_iota(I32, (ch, ch), 1).astype(F32)
    d = row - col
    intra_f = jnp.where(d >= 0, jnp.exp(lg_f * jnp.maximum(d, 0.0)), 0.0)
    intra_b = jnp.where(d <= 0, jnp.exp(lg_b * jnp.maximum(-d, 0.0)), 0.0)

    def wide(a):
        return jnp.concatenate([a] * (RET_DV // LANES), axis=1)

    qd_f = wide(jnp.exp(lg_f * (row + 1.0)))
    kd_f = jnp.exp(lg_f * (ch - 1.0 - row))
    cd_f = wide(jnp.exp(lg_f * float(ch)))
    qd_b = wide(jnp.exp(lg_b * (ch - row)))
    kd_b = jnp.exp(lg_b * row)
    cd_b = wide(jnp.exp(lg_b * float(ch)))

    j = lax.broadcasted_iota(I32, (lc, RET_DK), 0).astype(F32)
    kc = kc_ref[...].astype(F32) * scale
    vc = vc_ref[...]
    st_f[...] = _dot_tn((kc * jnp.exp(lg_f * (lc - 1.0 - j))).astype(BF16), vc)
    st_b[...] = _dot_tn((kc * jnp.exp(lg_b * j)).astype(BF16), vc)

    def scan_chunk(c, st, intra, qd, kd, cd, first_visit):
        r0 = pl.multiple_of(c * ch, ch)
        if first_visit:
            cs = cos_ref[pl.ds(r0, ch), :]
            sn = sin_ref[pl.ds(r0, ch), :]
            qf = q_ref[pl.ds(r0, ch), :].astype(F32)
            kf = k_ref[pl.ds(r0, ch), :].astype(F32)
            qb = (qf * cs + _rot_half(qf) * sn).astype(BF16)
            kr = (kf * cs + _rot_half(kf) * sn) * scale
            q_s[pl.ds(r0, ch), :] = qb
            k_s[pl.ds(r0, ch), :] = kr
        else:
            qb = q_s[pl.ds(r0, ch), :]
            kr = k_s[pl.ds(r0, ch), :]
        vb = v_ref[pl.ds(r0, ch), :]
        state = st[...]
        s = _dot_nt(qb, kr.astype(BF16)) * intra
        o = jnp.dot(s.astype(BF16), vb, preferred_element_type=F32)
        o = o + jnp.dot(qb, state.astype(BF16), preferred_element_type=F32) * qd
        st[...] = state * cd + _dot_tn((kr * kd).astype(BF16), vb)
        return r0, o

    nw = nw_ref[...]

    def finish(r0, o):
        mu = jnp.mean(o, axis=-1, keepdims=True)
        oc = o - mu
        var = jnp.mean(oc * oc, axis=-1, keepdims=True)
        hn = oc * lax.rsqrt(var + EPS) * nw
        gate = g_ref[pl.ds(r0, ch), :].astype(F32)
        o_ref[pl.ds(r0, ch), :] = (_silu(gate) * hn).astype(o_ref.dtype)

    def first_half(i, carry):
        r0, o = scan_chunk(i, st_f, intra_f, qd_f, kd_f, cd_f, True)
        acc[pl.ds(r0, ch), :] = o
        r1, o1 = scan_chunk(nc - 1 - i, st_b, intra_b, qd_b, kd_b, cd_b, True)
        acc[pl.ds(r1, ch), :] = o1
        return carry

    def second_half(i, carry):
        r0, o = scan_chunk(i, st_f, intra_f, qd_f, kd_f, cd_f, False)
        finish(r0, acc[pl.ds(r0, ch), :] + o)
        r1, o1 = scan_chunk(nc - 1 - i, st_b, intra_b, qd_b, kd_b, cd_b, False)
        finish(r1, acc[pl.ds(r1, ch), :] + o1)
        return carry

    lax.fori_loop(0, nc // 2, first_half, 0, unroll=4)
    lax.fori_loop(nc // 2, nc, second_half, 0, unroll=4)


def _retention(p, pc, rope_c, rope_s, decay_fwd, decay_bwd, ret_norm_w):
    b, l, _ = p.shape
    lc = pc.shape[1]
    nc = l // RET_CHUNK
    z = jnp.broadcast_to(jnp.stack([decay_fwd, decay_bwd])[:, :, None, None], (2, RET_HEADS, 1, LANES))
    kq0, kk0 = OFF["qr"] // RET_DK, OFF["kr"] // RET_DK
    kv0, kg0 = OFF["vr"] // RET_DV, OFF["gr"] // RET_DV
    full = lambda bi, h: (0, 0)
    return pl.pallas_call(
        functools.partial(_ret_kernel, nc=nc, lc=lc),
        grid=(b, RET_HEADS),
        in_specs=[pl.BlockSpec((None, l, RET_DK), lambda bi, h: (bi, 0, kq0 + h)),
                  pl.BlockSpec((None, l, RET_DK), lambda bi, h: (bi, 0, kk0 + h)),
                  pl.BlockSpec((None, l, RET_DV), lambda bi, h: (bi, 0, kv0 + h)),
                  pl.BlockSpec((None, l, RET_DV), lambda bi, h: (bi, 0, kg0 + h)),
                  pl.BlockSpec((None, lc, RET_DK), lambda bi, h: (bi, 0, kk0 + h)),
                  pl.BlockSpec((None, lc, RET_DV), lambda bi, h: (bi, 0, kv0 + h)),
                  pl.BlockSpec((l, RET_DK), full),
                  pl.BlockSpec((l, RET_DK), full),
                  pl.BlockSpec((2, None, 1, LANES), lambda bi, h: (0, h, 0, 0)),
                  pl.BlockSpec((1, RET_DV), lambda bi, h: (0, h))],
        out_specs=pl.BlockSpec((None, l, RET_DV), lambda bi, h: (bi, 0, h)),
        out_shape=jax.ShapeDtypeStruct((b, l, RET_V_W), BF16),
        scratch_shapes=[pltpu.VMEM((l, RET_DK), BF16), pltpu.VMEM((l, RET_DK), F32), pltpu.VMEM((l, RET_DV), F32),
                        pltpu.VMEM((RET_DK, RET_DV), F32), pltpu.VMEM((RET_DK, RET_DV), F32)],
        compiler_params=_cparams("parallel", "parallel"),
        name="retention",
    )(p, p, p, p, pc, pc, rope_c, rope_s, z, ret_norm_w.reshape(1, RET_V_W))


def _merge_kernel(attn_ref, ret_ref, ga_ref, gr_ref, x_ref, gate_ref, shift_ref, scale_ref, nw_ref,
                  wa_ref, wr_ref, wm_ref, x1_ref, h2_ref):
    a = jnp.dot(attn_ref[...], wa_ref[...], preferred_element_type=F32)
    r = jnp.dot(ret_ref[...], wr_ref[...], preferred_element_type=F32)
    mixed = _sigmoid(ga_ref[...].astype(F32)) * a + _sigmoid(gr_ref[...].astype(F32)) * r
    y = jnp.dot(mixed.astype(BF16), wm_ref[...], preferred_element_type=F32)
    x1 = x_ref[...] + gate_ref[...] * y
    x1_ref[...] = x1
    hn = x1 * lax.rsqrt(jnp.mean(x1 * x1, axis=-1, keepdims=True) + EPS) * nw_ref[...]
    h2_ref[...] = (hn * (1.0 + scale_ref[...]) + shift_ref[...]).astype(BF16)


def _merge(attn, ret, p, x, gate, shift2, scale2, norm2_w, wa, wr, wm, tm):
    b, l, d = x.shape
    assert d == D_MODEL
    kga, kgt = OFF["ga"] // d, OFF["gt"] // d
    row = lambda bi, i: (bi, i, 0)
    mod = lambda bi, i: (bi, 0, 0)
    const = lambda bi, i: (0, 0)
    once = pl.Buffered(1)
    return pl.pallas_call(
        _merge_kernel,
        grid=(b, l // tm),
        in_specs=[pl.BlockSpec((None, tm, ATTN_Q_W), row),
                  pl.BlockSpec((None, tm, RET_V_W), row),
                  pl.BlockSpec((None, tm, d), lambda bi, i: (bi, i, kga)),
                  pl.BlockSpec((None, tm, d), lambda bi, i: (bi, i, kgt)),
                  pl.BlockSpec((None, tm, d), row),
                  pl.BlockSpec((None, 1, d), mod),
                  pl.BlockSpec((None, 1, d), mod),
                  pl.BlockSpec((None, 1, d), mod),
                  pl.BlockSpec((1, d), const),
                  pl.BlockSpec(wa.shape, const, pipeline_mode=once),
                  pl.BlockSpec(wr.shape, const, pipeline_mode=once),
                  pl.BlockSpec(wm.shape, const, pipeline_mode=once)],
        out_specs=[pl.BlockSpec((None, tm, d), row), pl.BlockSpec((None, tm, d), row)],
        out_shape=[jax.ShapeDtypeStruct((b, l, d), F32), jax.ShapeDtypeStruct((b, l, d), BF16)],
        compiler_params=_cparams("parallel", "parallel"),
        name="merge",
    )(attn, ret, p, p, x, gate, shift2, scale2, norm2_w.reshape(1, d), wa, wr, wm)


def _run(gen):
    try:
        while True:
            next(gen)
    except StopIteration as stop:
        return stop.value


def _top_rows_gen(s, k):
    n = s.shape[0]
    rid = lax.broadcasted_iota(I32, s.shape, 0)
    vals, ids = [], []
    for _ in range(k):
        m = jnp.max(s, axis=0, keepdims=True)
        sel = jnp.min(jnp.where(s == m, rid, n), axis=0, keepdims=True)
        vals.append(m)
        ids.append(sel)
        s = jnp.where(rid == sel, -jnp.inf, s)
        yield
    return jnp.concatenate(vals, axis=0), jnp.concatenate(ids, axis=0)


def _top_rows(s, k):
    return _run(_top_rows_gen(s, k))


def _pair_candidates(s1, s2):
    k = PEER_TOPK
    t = s1.shape[1]
    r8 = lax.broadcasted_iota(I32, (SUBLANES, t), 0)
    r16 = lax.broadcasted_iota(I32, (k, t), 0)
    neg = -jnp.inf
    sums = [s1[0:1] + s2]
    flat = [r16]
    for a, nb in ((1, 8), (2, 5), (3, 4)):
        sums.append(jnp.where(r8 < nb, s1[a:a + 1] + s2[0:8], neg))
        flat.append(a * k + r8)
    sums.append(s1[8:16] + s2[0:1])
    flat.append((r8 + 8) * k)
    for b_, lo, hi in ((0, 4, 8), (1, 4, 8), (2, 4, 5)):
        sums.append(jnp.where((r8 >= lo) & (r8 < hi), s1[0:8] + s2[b_:b_ + 1], neg))
        flat.append(r8 * k + b_)
    return jnp.concatenate(sums, axis=0), jnp.concatenate(flat, axis=0)


def _pair_topk_gen(s1, n1, s2, n2):
    k = PEER_TOPK
    cand, flat = _pair_candidates(s1, s2)
    tops, picks = [], []
    for _ in range(k):
        m = jnp.max(cand, axis=0, keepdims=True)
        f = jnp.min(jnp.where(cand == m, flat, k * k), axis=0, keepdims=True)
        cand = jnp.where(flat == f, -jnp.inf, cand)
        tops.append(m)
        picks.append(f)
        yield
    top = jnp.concatenate(tops, axis=0)
    pick = jnp.concatenate(picks, axis=0)
    pa = pick // k
    pb = pick - pa * k
    e1 = jnp.zeros_like(pick)
    e2 = jnp.zeros_like(pick)
    for a in range(k):
        e1 = jnp.where(pa == a, n1[a:a + 1], e1)
        e2 = jnp.where(pb == a, n2[a:a + 1], e2)
    ex = jnp.exp(top - top[0:1])
    return e1, e2, ex / jnp.sum(ex, axis=0, keepdims=True)


def _pair_topk(s1, n1, s2, n2):
    return _run(_pair_topk_gen(s1, n1, s2, n2))


def _peer_q_kernel(h_ref, wq_ref, q_ref):
    q_ref[...] = jnp.dot(h_ref[...], wq_ref[...], preferred_element_type=F32).astype(q_ref.dtype)


def _peer_q(h2, wq, tm):
    n, d = h2.shape
    return pl.pallas_call(
        _peer_q_kernel,
        grid=(n // tm,),
        in_specs=[pl.BlockSpec((tm, d), lambda i: (i, 0)),
                  pl.BlockSpec(wq.shape, lambda i: (0, 0), pipeline_mode=pl.Buffered(1))],
        out_specs=pl.BlockSpec((tm, wq.shape[1]), lambda i: (i, 0)),
        out_shape=jax.ShapeDtypeStruct((n, wq.shape[1]), BF16),
        compiler_params=_cparams("parallel"),
        name="peer_q",
    )(h2, wq)


def _peer_topk_kernel(q_s, keys_ref, i1_ref, i2_ref, g_ref, i1_s, i2_s, g_s):
    k = PEER_TOPK

    def head(h, carry):
        c0 = pl.multiple_of(h * 2 * PEER_D_HALF, 2 * PEER_D_HALF)
        qh = q_s[:, pl.ds(c0, 2 * PEER_D_HALF)]
        s1, n1 = _top_rows(_dot_nt(keys_ref[2 * h], qh[:, :PEER_D_HALF]), k)
        s2, n2 = _top_rows(_dot_nt(keys_ref[2 * h + 1], qh[:, PEER_D_HALF:]), k)
        e1, e2, gates = _pair_topk(s1, n1, s2, n2)
        r0 = pl.multiple_of(h * k, k)
        g_s[pl.ds(r0, k), :] = gates
        i1_s[pl.ds(r0, k), :] = e1
        i2_s[pl.ds(r0, k), :] = e2
        return carry

    lax.fori_loop(0, PEER_HEADS, head, 0)
    i1_ref[...] = i1_s[...].T
    i2_ref[...] = i2_s[...].T
    g_ref[...] = g_s[...].T


def _peer_topk(q, keys, n, tm):
    slots = PEER_HEADS * PEER_TOPK
    row = lambda i: (i, 0)
    out = jax.ShapeDtypeStruct
    return pl.pallas_call(
        _peer_topk_kernel,
        grid=(n // tm,),
        in_specs=[pl.BlockSpec((tm, q.shape[1]), row),
                  pl.BlockSpec(keys.shape, lambda i: (0, 0, 0))],
        out_specs=[pl.BlockSpec((tm, slots), row)] * 3,
        out_shape=[out((n, slots), I32), out((n, slots), I32), out((n, slots), F32)],
        scratch_shapes=[pltpu.VMEM((slots, tm), I32), pltpu.VMEM((slots, tm), I32), pltpu.VMEM((slots, tm), F32)],
        compiler_params=_cparams("parallel"),
        name="peer_topk",
    )(q, keys)


def _peer_w_kernel(i1_ref, i2_ref, g_ref, w_ref, w3):
    nk = PEER_N_KEYS
    tb = i1_ref.shape[0]
    grp = 2 * SUBLANES
    rid = lax.broadcasted_iota(I32, (nk, i1_ref.shape[1]), 0)
    sub = lax.broadcasted_iota(I32, (SUBLANES, nk), 0)

    def rows_to_tokens(v):
        v = list(v)
        for dist in (4, 2, 1):
            keep = (sub & dist) == 0
            for k in range(SUBLANES):
                if k & dist == 0:
                    a, b_ = v[k], v[k + dist]
                    v[k] = jnp.where(keep, a, pltpu.roll(b_, dist, 0))
                    v[k + dist] = jnp.where(keep, pltpu.roll(a, SUBLANES - dist, 0), b_)
        return v

    def group(gi, carry):
        t0 = pl.multiple_of(gi * grp, grp)
        i1 = i1_ref[pl.ds(t0, grp), :]
        i2 = i2_ref[pl.ds(t0, grp), :]
        g = g_ref[pl.ds(t0, grp), :]
        for k in range(grp):
            left = jnp.where(rid == i1[k:k + 1], g[k:k + 1], 0.0).astype(BF16)
            right = jnp.where(rid == i2[k:k + 1], 1.0, 0.0).astype(BF16)
            w3[k * nk:(k + 1) * nk, :] = _dot_nt(left, right)
        for j in range(nk // SUBLANES):
            halves = []
            for half in range(grp // SUBLANES):
                tiles = [w3[(half * SUBLANES + k) * nk + j * SUBLANES:(half * SUBLANES + k) * nk + (j + 1) * SUBLANES, :]
                         for k in range(SUBLANES)]
                halves.append(rows_to_tokens(tiles))
            for i in range(SUBLANES):
                r = j * SUBLANES + i
                tile = jnp.concatenate([h[i] for h in halves], axis=0)
                w_ref[pl.ds(t0, grp), r * nk:(r + 1) * nk] = tile.astype(w_ref.dtype)
        return carry

    lax.fori_loop(0, tb // grp, group, 0, unroll=4)


def _peer_w(i1, i2, g, tb):
    n, slots = i1.shape
    ne = PEER_N_KEYS * PEER_N_KEYS
    row = lambda i: (i, 0)
    return pl.pallas_call(
        _peer_w_kernel,
        grid=(n // tb,),
        in_specs=[pl.BlockSpec((tb, slots), row)] * 3,
        out_specs=pl.BlockSpec((tb, ne), row),
        out_shape=jax.ShapeDtypeStruct((n, ne), BF16),
        scratch_shapes=[pltpu.VMEM((2 * SUBLANES * PEER_N_KEYS, PEER_N_KEYS), F32)],
        compiler_params=_cparams("parallel"),
        name="peer_w",
    )(i1, i2, g)


def _gelu_tanh(x):
    return 0.5 * x * (1.0 + jnp.tanh(0.7978845608028654 * (x + 0.044715 * (x * x * x))))


def _peer_dense_kernel(*refs, retrieve_next):
    if retrieve_next:
        (h_ref, u_ref, v_ref, w_ref, x_ref, gate_ref, nw_ref, qn_ref, keys_ref,
         o_ref, i1_ref, i2_ref, g_ref, acc, a_buf, s1_s, n1_s, i1_s, i2_s, g_s) = refs
    else:
        h_ref, u_ref, v_ref, w_ref, x_ref, gate_ref, nw_ref, o_ref, acc, a_buf = refs
    j = pl.program_id(1)
    nj = pl.num_programs(1) - 1
    k = PEER_TOPK

    def score(slot):
        a_buf[slot] = jnp.dot(h_ref[...], u_ref[...], preferred_element_type=F32)

    def finish(slot):
        act = (_gelu_tanh(a_buf[slot]) * w_ref[...].astype(F32)).astype(BF16)
        acc[...] += jnp.dot(act, v_ref[...], preferred_element_type=F32)

    def retrieval_chains(half):
        if not retrieve_next:
            return [], 0
        r0 = pl.multiple_of((j // 2) * k, k)

        def chain(t0):
            tl = slice(t0, t0 + LANES)
            s, n = yield from _top_rows_gen(_dot_nt(keys_ref[...], qn_ref[tl, :]), k)
            if half == 0:
                s1_s[:, tl] = s
                n1_s[:, tl] = n
            else:
                e1, e2, gates = yield from _pair_topk_gen(s1_s[:, tl], n1_s[:, tl], s, n)
                i1_s[pl.ds(r0, k), tl] = e1
                i2_s[pl.ds(r0, k), tl] = e2
                g_s[pl.ds(r0, k), tl] = gates

        rounds = (k if half == 0 else 2 * k) + 1
        return [chain(t0) for t0 in range(0, qn_ref.shape[0], LANES)], rounds

    def interleave(main, filler):
        chains, rounds = filler
        units = [c for c in chains for _ in range(rounds)]
        done = 0
        for idx, piece in enumerate(main):
            upto = (idx + 1) * len(units) // len(main)
            for c in units[done:upto]:
                next(c, None)
            done = upto
            piece()

    def score_pieces(slot):
        half_n = a_buf.shape[2] // 2

        def piece(c0):
            a_buf[slot, :, c0:c0 + half_n] = _dot_nt(h_ref[...], u_ref[c0:c0 + half_n, :])
        return [functools.partial(piece, c0) for c0 in (0, half_n)]

    def finish_pieces(slot):
        quarter = acc.shape[1] // 4
        cell = {}

        def piece(c0):
            if "act" not in cell:
                cell["act"] = (_gelu_tanh(a_buf[slot]) * w_ref[...].astype(F32)).astype(BF16)
            acc[:, c0:c0 + quarter] += jnp.dot(cell["act"], v_ref[:, c0:c0 + quarter], preferred_element_type=F32)
        return [functools.partial(piece, c0) for c0 in range(0, acc.shape[1], quarter)]

    @pl.when(j == 0)
    def _():
        acc[...] = jnp.zeros_like(acc)
        interleave(score_pieces(0), retrieval_chains(0))

    for parity in range(2):
        @pl.when((j > 0) & (j < nj) & (j % 2 == parity))
        def _():
            interleave(score_pieces(parity) + finish_pieces(1 - parity), retrieval_chains(parity))

    @pl.when(j == nj)
    def _():
        finish((nj - 1) % 2)
        x2 = x_ref[...] + gate_ref[...] * acc[...]
        o_ref[...] = x2 * lax.rsqrt(jnp.mean(x2 * x2, axis=-1, keepdims=True) + EPS) * nw_ref[...]
        if retrieve_next:
            i1_ref[...] = i1_s[...].T
            i2_ref[...] = i2_s[...].T
            g_ref[...] = g_s[...].T


def _peer_dense(h2, u, v, w, x1, gate, norm_f_w, seq, tm, te, tile0, n_tiles, retrieval=None):
    n, d = h2.shape
    ne = u.shape[0]
    nj = ne // te
    assert nj % 2 == 0
    tiles_per_batch = seq // tm
    row = lambda i, j: (tile0 + i, 0)
    scored = lambda i, j: (jnp.minimum(j, nj - 1), 0)
    finished = lambda j: jnp.maximum(j - 1, 0)
    in_specs = [pl.BlockSpec((tm, d), row),
                pl.BlockSpec((te, d), scored),
                pl.BlockSpec((te, d), lambda i, j: (finished(j), 0)),
                pl.BlockSpec((tm, te), lambda i, j: (i, finished(j))),
                pl.BlockSpec((tm, d), row),
                pl.BlockSpec((None, 1, d), lambda i, j: ((tile0 + i) // tiles_per_batch, 0, 0)),
                pl.BlockSpec((1, d), lambda i, j: (0, 0))]
    args = [h2, u, v, w, x1, gate, norm_f_w.reshape(1, d)]
    out_specs = [pl.BlockSpec((tm, d), row)]
    out_shape = [jax.ShapeDtypeStruct((n, d), F32)]
    scratch = [pltpu.VMEM((tm, d), F32), pltpu.VMEM((2, tm, te), F32)]
    if retrieval is not None:
        q, keys = retrieval
        slots = PEER_HEADS * PEER_TOPK
        assert nj == keys.shape[0]
        half_head = lambda j: jnp.minimum(j, nj - 1)
        in_specs += [pl.BlockSpec((tm, PEER_D_HALF), lambda i, j: (tile0 + n_tiles + i, half_head(j))),
                     pl.BlockSpec((None, PEER_N_KEYS, PEER_D_HALF), lambda i, j: (half_head(j), 0, 0))]
        args += [q, keys]
        out_specs += [pl.BlockSpec((tm, slots), lambda i, j: (i, 0))] * 3
        out_shape += [jax.ShapeDtypeStruct((n_tiles * tm, slots), t) for t in (I32, I32, F32)]
        scratch += [pltpu.VMEM((PEER_TOPK, tm), F32), pltpu.VMEM((PEER_TOPK, tm), I32),
                    pltpu.VMEM((slots, tm), I32), pltpu.VMEM((slots, tm), I32), pltpu.VMEM((slots, tm), F32)]
    outs = pl.pallas_call(
        functools.partial(_peer_dense_kernel, retrieve_next=retrieval is not None),
        grid=(n_tiles, nj + 1),
        in_specs=in_specs,
        out_specs=out_specs,
        out_shape=out_shape,
        scratch_shapes=scratch,
        input_output_aliases={4: 0},
        compiler_params=_cparams("parallel", "arbitrary"),
        name="peer_dense",
    )(*args)
    return outs[0], tuple(outs[1:])


def _rope_tables(length):
    t = jnp.arange(length, dtype=jnp.int32)
    row = (t // GRID_W).astype(F32)
    col = (t % GRID_W).astype(F32)
    n_freq = HEAD_DIM // 4
    inv_freq = ROPE_BASE ** (-jnp.arange(n_freq, dtype=F32) / n_freq)
    ang = jnp.concatenate([row[:, None] * inv_freq, col[:, None] * inv_freq], axis=-1)
    cos, sin = jnp.cos(ang), jnp.sin(ang)
    return jnp.concatenate([cos, cos], axis=-1), jnp.concatenate([-sin, sin], axis=-1)


def _layer(x, ctx, c_rows, rope_c, rope_s, ada_w, ada_b, norm1_w, w_in, q_norm_w, k_norm_w, ret_decay_fwd,
           ret_decay_bwd, ret_norm_w, w_attn_branch, w_ret_branch, w_merge_out, norm2_w, peer_w_q, peer_keys,
           peer_u, peer_v, norm_f_w):
    b, l, d = x.shape
    lc = ctx.shape[1]
    mod = _ada(c_rows, ada_w, ada_b)
    mod_x = [mod[:b, i * d:(i + 1) * d].reshape(b, 1, d) for i in range(6)]
    mod_c = [mod[b:b + 1, i * d:(i + 1) * d].reshape(1, 1, d) for i in range(2)]

    w_in_b = _regroup_w_in(w_in).astype(BF16)
    n_in = w_in.shape[1]
    tn = n_in // 4
    p = _in_proj(x, norm1_w, mod_x[0], mod_x[1], w_in_b, tm=min(512, l), tn=tn)
    pc = _in_proj(ctx, norm1_w, mod_c[0], mod_c[1], w_in_b, tm=min(256, lc), tn=tn)

    attn = _attention(p, pc, rope_c, rope_s, q_norm_w, k_norm_w, tq=min(256, l))
    ret = _retention(p, pc, rope_c, rope_s, ret_decay_fwd, ret_decay_bwd, ret_norm_w)
    x1, h2 = _merge(attn, ret, p, x, mod_x[2], mod_x[3], mod_x[4], norm2_w, w_attn_branch.astype(BF16),
                    w_ret_branch.astype(BF16), w_merge_out.astype(BF16), tm=min(256, l))

    n = b * l
    h2f = h2.reshape(n, d)
    keys = peer_keys.reshape(PEER_HEADS * 2, PEER_N_KEYS, PEER_D_HALF).astype(BF16)
    q = _peer_q(h2f, peer_w_q.astype(BF16), tm=min(1024, l))
    u, v = peer_u.astype(BF16), peer_v.astype(BF16)
    tm = min(512, l)
    n_tiles = n // tm
    n_chunks = next(c for c in (8, 4, 2, 1) if n_tiles % c == 0)
    tiles = n_tiles // n_chunks
    sel = _peer_topk(q, keys, n=tiles * tm, tm=256)
    out = x1.reshape(n, d)
    for chunk in range(n_chunks):
        w = _peer_w(*sel, tb=128)
        retrieval = (q, keys) if chunk + 1 < n_chunks else None
        out, sel = _peer_dense(h2f, u, v, w, out, mod_x[5], norm_f_w, seq=l, tm=tm, te=1024,
                               tile0=chunk * tiles, n_tiles=tiles, retrieval=retrieval)
    return out.reshape(b, l, d)


def kernel(x, c, ctx, c_ctx, ada_w, ada_b, norm1_w, w_in, q_norm_w, k_norm_w, ret_decay_fwd, ret_decay_bwd,
           ret_norm_w, w_attn_branch, w_ret_branch, w_merge_out, norm2_w, peer_w_q, peer_keys, peer_u, peer_v,
           norm_f_w):
    depth = ada_w.shape[0]
    assert depth == 1, "context-stream update between layers is not implemented"
    b, l, d = x.shape
    rows = -(-(b + 1) // SUBLANES) * SUBLANES
    c_rows = jnp.zeros((rows, d), F32).at[:b].set(c).at[b].set(c_ctx)
    rope_c, rope_s = _rope_tables(l)
    return _layer(x, ctx, c_rows, rope_c, rope_s, ada_w[0], ada_b[0], norm1_w[0], w_in[0], q_norm_w[0], k_norm_w[0],
                  ret_decay_fwd[0], ret_decay_bwd[0], ret_norm_w[0], w_attn_branch[0], w_ret_branch[0],
                  w_merge_out[0], norm2_w[0], peer_w_q[0], peer_keys[0], peer_u[0], peer_v[0], norm_f_w)
```

```python
import functools

import jax
import jax.numpy as jnp
from jax import lax
from jax.experimental import pallas as pl
from jax.experimental.pallas import tpu as pltpu

F32 = jnp.float32
BF16 = jnp.bfloat16
I32 = jnp.int32

EPS = 1e-6
GRID_W = 64
ROPE_BASE = 10000.0
ATTN_HEADS = 8
ATTN_KV_HEADS = 2
HEAD_DIM = 128
RET_HEADS = 8
RET_DK = 128
RET_DV = 256
RET_CHUNK = 128
PEER_HEADS = 8
PEER_N_KEYS = 128
PEER_D_HALF = 128
PEER_TOPK = 16

LANES = 128
SUBLANES = 8
VMEM_LIMIT = 60 * 1024 * 1024

ATTN_Q_W = ATTN_HEADS * HEAD_DIM
ATTN_KV_W = ATTN_KV_HEADS * HEAD_DIM
RET_QK_W = RET_HEADS * RET_DK
RET_V_W = RET_HEADS * RET_DV
D_MODEL = 2048
_SRC_ORDER = (("qa", ATTN_Q_W), ("ka", ATTN_KV_W), ("va", ATTN_KV_W), ("qr", RET_QK_W), ("kr", RET_QK_W),
              ("vr", RET_V_W), ("gr", RET_V_W), ("ga", D_MODEL), ("gt", D_MODEL))
_DST_ORDER = ("ga", "gt", "vr", "gr", "qr", "kr", "qa", "ka", "va")


def _offsets(order, widths):
    off, o = {}, 0
    for name in order:
        off[name] = o
        o += widths[name]
    return off


_WIDTH = dict(_SRC_ORDER)
_SRC_OFF = _offsets([n for n, _ in _SRC_ORDER], _WIDTH)
OFF = _offsets(_DST_ORDER, _WIDTH)


def _regroup_w_in(w_in):
    return jnp.concatenate([w_in[:, _SRC_OFF[n]:_SRC_OFF[n] + _WIDTH[n]] for n in _DST_ORDER], axis=1)


def _cparams(*sem):
    return pltpu.CompilerParams(dimension_semantics=sem, vmem_limit_bytes=VMEM_LIMIT)


def _sigmoid(x):
    return 1.0 / (1.0 + jnp.exp(-x))


def _silu(x):
    return x * _sigmoid(x)


def _rot_half(x):
    return pltpu.roll(x, HEAD_DIM // 2, 1)


def _dot_nt(a, b):
    return lax.dot_general(a, b, (((1,), (1,)), ((), ())), preferred_element_type=F32)


def _dot_tn(a, b):
    return lax.dot_general(a, b, (((0,), (0,)), ((), ())), preferred_element_type=F32)


def _ada_kernel(c_ref, w_ref, b_ref, o_ref):
    sc = _silu(c_ref[...]).astype(BF16)
    o_ref[...] = jnp.dot(sc, w_ref[...].astype(BF16), preferred_element_type=F32) + b_ref[...]


def _ada(c_rows, ada_w, ada_b):
    rows, d = c_rows.shape
    n = ada_w.shape[1]
    tn = 1536
    return pl.pallas_call(
        _ada_kernel,
        grid=(n // tn,),
        in_specs=[pl.BlockSpec((rows, d), lambda j: (0, 0)),
                  pl.BlockSpec((d, tn), lambda j: (0, j)),
                  pl.BlockSpec((1, tn), lambda j: (0, j))],
        out_specs=pl.BlockSpec((rows, tn), lambda j: (0, j)),
        out_shape=jax.ShapeDtypeStruct((rows, n), F32),
        compiler_params=_cparams("arbitrary"),
        name="ada",
    )(c_rows, ada_w, ada_b.reshape(1, n))


def _in_proj_kernel(x_ref, nw_ref, shift_ref, scale_ref, w_ref, o_ref, h_ref):
    @pl.when(pl.program_id(2) == 0)
    def _():
        x = x_ref[...]
        y = x * lax.rsqrt(jnp.mean(x * x, axis=-1, keepdims=True) + EPS) * nw_ref[...]
        h_ref[...] = (y * (1.0 + scale_ref[...]) + shift_ref[...]).astype(BF16)

    o_ref[...] = jnp.dot(h_ref[...], w_ref[...], preferred_element_type=F32).astype(o_ref.dtype)


def _in_proj(x, norm_w, shift, scale, w, tm, tn):
    b, l, d = x.shape
    n = w.shape[1]
    per_batch = shift.shape[0] > 1
    mod_map = (lambda bi, i, j: (bi, 0, 0)) if per_batch else (lambda bi, i, j: (0, 0, 0))
    return pl.pallas_call(
        _in_proj_kernel,
        grid=(b, l // tm, n // tn),
        in_specs=[pl.BlockSpec((None, tm, d), lambda bi, i, j: (bi, i, 0)),
                  pl.BlockSpec((1, d), lambda bi, i, j: (0, 0)),
                  pl.BlockSpec((None, 1, d), mod_map),
                  pl.BlockSpec((None, 1, d), mod_map),
                  pl.BlockSpec((d, tn), lambda bi, i, j: (0, j))],
        out_specs=pl.BlockSpec((None, tm, tn), lambda bi, i, j: (bi, i, j)),
        out_shape=jax.ShapeDtypeStruct((b, l, n), BF16),
        scratch_shapes=[pltpu.VMEM((tm, d), BF16)],
        compiler_params=_cparams("parallel", "parallel", "arbitrary"),
        name="in_proj",
    )(x, norm_w.reshape(1, d), shift, scale, w)


def _head_rms(x, w):
    return x * lax.rsqrt(jnp.mean(x * x, axis=-1, keepdims=True) + EPS) * w


def _attn_kernel(q_ref, kc_ref, k_ref, vc_ref, v_ref, cq_ref, sq_ref, ck_ref, sk_ref, qnw_ref, knw_ref,
                 o_ref, k_s, v_s, *, lc, group):
    n_kv = k_s.shape[0]

    @pl.when(pl.program_id(1) == 0)
    def _():
        knw = knw_ref[...]
        for kv in range(n_kv):
            hs = slice(kv * HEAD_DIM, (kv + 1) * HEAD_DIM)
            k_s[kv, 0:lc, :] = _head_rms(kc_ref[:, hs].astype(F32), knw).astype(BF16)
            kn = _head_rms(k_ref[:, hs].astype(F32), knw)
            k_s[kv, lc:, :] = (kn * ck_ref[...] + _rot_half(kn) * sk_ref[...]).astype(BF16)
            v_s[kv, 0:lc, 0:HEAD_DIM] = vc_ref[:, hs]
            v_s[kv, lc:, 0:HEAD_DIM] = v_ref[:, hs]
            v_s[kv, :, HEAD_DIM:] = jnp.ones((v_s.shape[1], HEAD_DIM), BF16)

    c = (HEAD_DIM ** -0.5) * 1.4426950408889634
    qnw = qnw_ref[...]
    cq = cq_ref[...] * c
    sq = sq_ref[...] * c
    def scores(g):
        qn = _head_rms(q_ref[:, g * HEAD_DIM:(g + 1) * HEAD_DIM].astype(F32), qnw)
        qr = (qn * cq + _rot_half(qn) * sq).astype(BF16)
        return _dot_nt(qr, k_s[g // group])

    def output(g, s):
        m = jnp.max(s, axis=-1, keepdims=True)
        p = jnp.exp2((s - m).astype(BF16))
        ol = jnp.dot(p, v_s[g // group], preferred_element_type=F32)
        o_ref[:, g * HEAD_DIM:(g + 1) * HEAD_DIM] = (
            ol[:, :HEAD_DIM] * (1.0 / ol[:, HEAD_DIM:HEAD_DIM + 1])).astype(o_ref.dtype)

    heads = n_kv * group
    s_next = scores(0)
    for g in range(heads):
        s = s_next
        if g + 1 < heads:
            s_next = scores(g + 1)
        output(g, s)


def _attention(p, pc, rope_c, rope_s, q_norm_w, k_norm_w, tq):
    b, l, _ = p.shape
    lc = pc.shape[1]
    group = ATTN_HEADS // ATTN_KV_HEADS
    kq0 = OFF["qa"] // ATTN_Q_W
    kk0 = OFF["ka"] // ATTN_KV_W
    kv0 = OFF["va"] // ATTN_KV_W
    full = lambda bi, i: (0, 0)
    return pl.pallas_call(
        functools.partial(_attn_kernel, lc=lc, group=group),
        grid=(b, l // tq),
        in_specs=[pl.BlockSpec((None, tq, ATTN_Q_W), lambda bi, i: (bi, i, kq0)),
                  pl.BlockSpec((None, lc, ATTN_KV_W), lambda bi, i: (bi, 0, kk0)),
                  pl.BlockSpec((None, l, ATTN_KV_W), lambda bi, i: (bi, 0, kk0)),
                  pl.BlockSpec((None, lc, ATTN_KV_W), lambda bi, i: (bi, 0, kv0)),
                  pl.BlockSpec((None, l, ATTN_KV_W), lambda bi, i: (bi, 0, kv0)),
                  pl.BlockSpec((tq, HEAD_DIM), lambda bi, i: (i, 0)),
                  pl.BlockSpec((tq, HEAD_DIM), lambda bi, i: (i, 0)),
                  pl.BlockSpec((l, HEAD_DIM), full),
                  pl.BlockSpec((l, HEAD_DIM), full),
                  pl.BlockSpec((1, HEAD_DIM), full),
                  pl.BlockSpec((1, HEAD_DIM), full)],
        out_specs=pl.BlockSpec((None, tq, ATTN_Q_W), lambda bi, i: (bi, i, 0)),
        out_shape=jax.ShapeDtypeStruct((b, l, ATTN_Q_W), BF16),
        scratch_shapes=[pltpu.VMEM((ATTN_KV_HEADS, lc + l, HEAD_DIM), BF16),
                        pltpu.VMEM((ATTN_KV_HEADS, lc + l, 2 * HEAD_DIM), BF16)],
        compiler_params=_cparams("parallel", "arbitrary"),
        name="attention",
    )(p, pc, p, pc, p, rope_c, rope_s, rope_c, rope_s, q_norm_w.reshape(1, HEAD_DIM), k_norm_w.reshape(1, HEAD_DIM))


def _ret_kernel(q_ref, k_ref, v_ref, g_ref, kc_ref, vc_ref, cos_ref, sin_ref, z_ref, nw_ref, o_ref,
                q_s, k_s, acc, st_f, st_b, *, nc, lc):
    ch = RET_CHUNK
    assert nc % 2 == 0
    scale = RET_DK ** -0.5
    lg_f = -jnp.exp(z_ref[0])
    lg_b = -jnp.exp(z_ref[1])
    row = lax.broadcasted_iota(I32, (ch, ch), 0).astype(F32)
    col = lax.broadcasted_iota(I32, (ch, ch), 1).astype(F32)
    d = row - col
    intra_f = jnp.where(d >= 0, jnp.exp(lg_f * jnp.maximum(d, 0.0)), 0.0)
    intra_b = jnp.where(d <= 0, jnp.exp(lg_b * jnp.maximum(-d, 0.0)), 0.0)

    def wide(a):
        return jnp.concatenate([a] * (RET_DV // LANES), axis=1)

    qd_f = wide(jnp.exp(lg_f * (row + 1.0)))
    kd_f = jnp.exp(lg_f * (ch - 1.0 - row))
    cd_f = wide(jnp.exp(lg_f * float(ch)))
    qd_b = wide(jnp.exp(lg_b * (ch - row)))
    kd_b = jnp.exp(lg_b * row)
    cd_b = wide(jnp.exp(lg_b * float(ch)))

    j = lax.broadcasted_iota(I32, (lc, RET_DK), 0).astype(F32)
    kc = kc_ref[...].astype(F32) * scale
    vc = vc_ref[...]
    st_f[...] = _dot_tn((kc * jnp.exp(lg_f * (lc - 1.0 - j))).astype(BF16), vc)
    st_b[...] = _dot_tn((kc * jnp.exp(lg_b * j)).astype(BF16), vc)

    def scan_chunk(c, st, intra, qd, kd, cd, first_visit):
        r0 = pl.multiple_of(c * ch, ch)
        if first_visit:
            cs = cos_ref[pl.ds(r0, ch), :]
            sn = sin_ref[pl.ds(r0, ch), :]
            qf = q_ref[pl.ds(r0, ch), :].astype(F32)
            kf = k_ref[pl.ds(r0, ch), :].astype(F32)
            qb = (qf * cs + _rot_half(qf) * sn).astype(BF16)
            kr = (kf * cs + _rot_half(kf) * sn) * scale
            q_s[pl.ds(r0, ch), :] = qb
            k_s[pl.ds(r0, ch), :] = kr
        else:
            qb = q_s[pl.ds(r0, ch), :]
            kr = k_s[pl.ds(r0, ch), :]
        vb = v_ref[pl.ds(r0, ch), :]
        state = st[...]
        s = _dot_nt(qb, kr.astype(BF16)) * intra
        o = jnp.dot(s.astype(BF16), vb, preferred_element_type=F32)
        o = o + jnp.dot(qb, state.astype(BF16), preferred_element_type=F32) * qd
        st[...] = state * cd + _dot_tn((kr * kd).astype(BF16), vb)
        return r0, o

    nw = nw_ref[...]

    def finish(r0, o):
        mu = jnp.mean(o, axis=-1, keepdims=True)
        oc = o - mu
        var = jnp.mean(oc * oc, axis=-1, keepdims=True)
        hn = oc * lax.rsqrt(var + EPS) * nw
        gate = g_ref[pl.ds(r0, ch), :].astype(F32)
        o_ref[pl.ds(r0, ch), :] = (_silu(gate) * hn).astype(o_ref.dtype)

    def first_half(i, carry):
        r0, o = scan_chunk(i, st_f, intra_f, qd_f, kd_f, cd_f, True)
        acc[pl.ds(r0, ch), :] = o
        r1, o1 = scan_chunk(nc - 1 - i, st_b, intra_b, qd_b, kd_b, cd_b, True)
        acc[pl.ds(r1, ch), :] = o1
        return carry

    def second_half(i, carry):
        r0, o = scan_chunk(i, st_f, intra_f, qd_f, kd_f, cd_f, False)
        finish(r0, acc[pl.ds(r0, ch), :] + o)
        r1, o1 = scan_chunk(nc - 1 - i, st_b, intra_b, qd_b, kd_b, cd_b, False)
        finish(r1, acc[pl.ds(r1, ch), :] + o1)
        return carry

    lax.fori_loop(0, nc // 2, first_half, 0, unroll=4)
    lax.fori_loop(nc // 2, nc, second_half, 0, unroll=4)


def _retention(p, pc, rope_c, rope_s, decay_fwd, decay_bwd, ret_norm_w):
    b, l, _ = p.shape
    lc = pc.shape[1]
    nc = l // RET_CHUNK
    z = jnp.broadcast_to(jnp.stack([decay_fwd, decay_bwd])[:, :, None, None], (2, RET_HEADS, 1, LANES))
    kq0, kk0 = OFF["qr"] // RET_DK, OFF["kr"] // RET_DK
    kv0, kg0 = OFF["vr"] // RET_DV, OFF["gr"] // RET_DV
    full = lambda bi, h: (0, 0)
    return pl.pallas_call(
        functools.partial(_ret_kernel, nc=nc, lc=lc),
        grid=(b, RET_HEADS),
        in_specs=[pl.BlockSpec((None, l, RET_DK), lambda bi, h: (bi, 0, kq0 + h)),
                  pl.BlockSpec((None, l, RET_DK), lambda bi, h: (bi, 0, kk0 + h)),
                  pl.BlockSpec((None, l, RET_DV), lambda bi, h: (bi, 0, kv0 + h)),
                  pl.BlockSpec((None, l, RET_DV), lambda bi, h: (bi, 0, kg0 + h)),
                  pl.BlockSpec((None, lc, RET_DK), lambda bi, h: (bi, 0, kk0 + h)),
                  pl.BlockSpec((None, lc, RET_DV), lambda bi, h: (bi, 0, kv0 + h)),
                  pl.BlockSpec((l, RET_DK), full),
                  pl.BlockSpec((l, RET_DK), full),
                  pl.BlockSpec((2, None, 1, LANES), lambda bi, h: (0, h, 0, 0)),
                  pl.BlockSpec((1, RET_DV), lambda bi, h: (0, h))],
        out_specs=pl.BlockSpec((None, l, RET_DV), lambda bi, h: (bi, 0, h)),
        out_shape=jax.ShapeDtypeStruct((b, l, RET_V_W), BF16),
        scratch_shapes=[pltpu.VMEM((l, RET_DK), BF16), pltpu.VMEM((l, RET_DK), F32), pltpu.VMEM((l, RET_DV), F32),
                        pltpu.VMEM((RET_DK, RET_DV), F32), pltpu.VMEM((RET_DK, RET_DV), F32)],
        compiler_params=_cparams("parallel", "parallel"),
        name="retention",
    )(p, p, p, p, pc, pc, rope_c, rope_s, z, ret_norm_w.reshape(1, RET_V_W))


def _merge_kernel(attn_ref, ret_ref, ga_ref, gr_ref, x_ref, gate_ref, shift_ref, scale_ref, nw_ref,
                  wa_ref, wr_ref, wm_ref, x1_ref, h2_ref):
    a = jnp.dot(attn_ref[...], wa_ref[...], preferred_element_type=F32)
    r = jnp.dot(ret_ref[...], wr_ref[...], preferred_element_type=F32)
    mixed = _sigmoid(ga_ref[...].astype(F32)) * a + _sigmoid(gr_ref[...].astype(F32)) * r
    y = jnp.dot(mixed.astype(BF16), wm_ref[...], preferred_element_type=F32)
    x1 = x_ref[...] + gate_ref[...] * y
    x1_ref[...] = x1
    hn = x1 * lax.rsqrt(jnp.mean(x1 * x1, axis=-1, keepdims=True) + EPS) * nw_ref[...]
    h2_ref[...] = (hn * (1.0 + scale_ref[...]) + shift_ref[...]).astype(BF16)


def _merge(attn, ret, p, x, gate, shift2, scale2, norm2_w, wa, wr, wm, tm):
    b, l, d = x.shape
    assert d == D_MODEL
    kga, kgt = OFF["ga"] // d, OFF["gt"] // d
    row = lambda bi, i: (bi, i, 0)
    mod = lambda bi, i: (bi, 0, 0)
    const = lambda bi, i: (0, 0)
    once = pl.Buffered(1)
    return pl.pallas_call(
        _merge_kernel,
        grid=(b, l // tm),
        in_specs=[pl.BlockSpec((None, tm, ATTN_Q_W), row),
                  pl.BlockSpec((None, tm, RET_V_W), row),
                  pl.BlockSpec((None, tm, d), lambda bi, i: (bi, i, kga)),
                  pl.BlockSpec((None, tm, d), lambda bi, i: (bi, i, kgt)),
                  pl.BlockSpec((None, tm, d), row),
                  pl.BlockSpec((None, 1, d), mod),
                  pl.BlockSpec((None, 1, d), mod),
                  pl.BlockSpec((None, 1, d), mod),
                  pl.BlockSpec((1, d), const),
                  pl.BlockSpec(wa.shape, const, pipeline_mode=once),
                  pl.BlockSpec(wr.shape, const, pipeline_mode=once),
                  pl.BlockSpec(wm.shape, const, pipeline_mode=once)],
        out_specs=[pl.BlockSpec((None, tm, d), row), pl.BlockSpec((None, tm, d), row)],
        out_shape=[jax.ShapeDtypeStruct((b, l, d), F32), jax.ShapeDtypeStruct((b, l, d), BF16)],
        compiler_params=_cparams("parallel", "parallel"),
        name="merge",
    )(attn, ret, p, p, x, gate, shift2, scale2, norm2_w.reshape(1, d), wa, wr, wm)


def _run(gen):
    try:
        while True:
            next(gen)
    except StopIteration as stop:
        return stop.value


def _top_rows_gen(s, k):
    n = s.shape[0]
    rid = lax.broadcasted_iota(I32, s.shape, 0)
    vals, ids = [], []
    for _ in range(k):
        m = jnp.max(s, axis=0, keepdims=True)
        sel = jnp.min(jnp.where(s == m, rid, n), axis=0, keepdims=True)
        vals.append(m)
        ids.append(sel)
        s = jnp.where(rid == sel, -jnp.inf, s)
        yield
    return jnp.concatenate(vals, axis=0), jnp.concatenate(ids, axis=0)


def _top_rows(s, k):
    return _run(_top_rows_gen(s, k))


def _pair_candidates(s1, s2):
    k = PEER_TOPK
    t = s1.shape[1]
    r8 = lax.broadcasted_iota(I32, (SUBLANES, t), 0)
    r16 = lax.broadcasted_iota(I32, (k, t), 0)
    neg = -jnp.inf
    sums = [s1[0:1] + s2]
    flat = [r16]
    for a, nb in ((1, 8), (2, 5), (3, 4)):
        sums.append(jnp.where(r8 < nb, s1[a:a + 1] + s2[0:8], neg))
        flat.append(a * k + r8)
    sums.append(s1[8:16] + s2[0:1])
    flat.append((r8 + 8) * k)
    for b_, lo, hi in ((0, 4, 8), (1, 4, 8), (2, 4, 5)):
        sums.append(jnp.where((r8 >= lo) & (r8 < hi), s1[0:8] + s2[b_:b_ + 1], neg))
        flat.append(r8 * k + b_)
    return jnp.concatenate(sums, axis=0), jnp.concatenate(flat, axis=0)


def _pair_topk_gen(s1, n1, s2, n2):
    k = PEER_TOPK
    cand, flat = _pair_candidates(s1, s2)
    tops, picks = [], []
    for _ in range(k):
        m = jnp.max(cand, axis=0, keepdims=True)
        f = jnp.min(jnp.where(cand == m, flat, k * k), axis=0, keepdims=True)
        cand = jnp.where(flat == f, -jnp.inf, cand)
        tops.append(m)
        picks.append(f)
        yield
    top = jnp.concatenate(tops, axis=0)
    pick = jnp.concatenate(picks, axis=0)
    pa = pick // k
    pb = pick - pa * k
    e1 = jnp.zeros_like(pick)
    e2 = jnp.zeros_like(pick)
    for a in range(k):
        e1 = jnp.where(pa == a, n1[a:a + 1], e1)
        e2 = jnp.where(pb == a, n2[a:a + 1], e2)
    ex = jnp.exp(top - top[0:1])
    return e1, e2, ex / jnp.sum(ex, axis=0, keepdims=True)


def _pair_topk(s1, n1, s2, n2):
    return _run(_pair_topk_gen(s1, n1, s2, n2))


def _peer_q_kernel(h_ref, wq_ref, q_ref):
    q_ref[...] = jnp.dot(h_ref[...], wq_ref[...], preferred_element_type=F32).astype(q_ref.dtype)


def _peer_q(h2, wq, tm):
    n, d = h2.shape
    return pl.pallas_call(
        _peer_q_kernel,
        grid=(n // tm,),
        in_specs=[pl.BlockSpec((tm, d), lambda i: (i, 0)),
                  pl.BlockSpec(wq.shape, lambda i: (0, 0), pipeline_mode=pl.Buffered(1))],
        out_specs=pl.BlockSpec((tm, wq.shape[1]), lambda i: (i, 0)),
        out_shape=jax.ShapeDtypeStruct((n, wq.shape[1]), BF16),
        compiler_params=_cparams("parallel"),
        name="peer_q",
    )(h2, wq)


def _peer_topk_kernel(q_s, keys_ref, i1_ref, i2_ref, g_ref, i1_s, i2_s, g_s):
    k = PEER_TOPK

    def head(h, carry):
        c0 = pl.multiple_of(h * 2 * PEER_D_HALF, 2 * PEER_D_HALF)
        qh = q_s[:, pl.ds(c0, 2 * PEER_D_HALF)]
        s1, n1 = _top_rows(_dot_nt(keys_ref[2 * h], qh[:, :PEER_D_HALF]), k)
        s2, n2 = _top_rows(_dot_nt(keys_ref[2 * h + 1], qh[:, PEER_D_HALF:]), k)
        e1, e2, gates = _pair_topk(s1, n1, s2, n2)
        r0 = pl.multiple_of(h * k, k)
        g_s[pl.ds(r0, k), :] = gates
        i1_s[pl.ds(r0, k), :] = e1
        i2_s[pl.ds(r0, k), :] = e2
        return carry

    lax.fori_loop(0, PEER_HEADS, head, 0)
    i1_ref[...] = i1_s[...].T
    i2_ref[...] = i2_s[...].T
    g_ref[...] = g_s[...].T


def _peer_topk(q, keys, n, tm):
    slots = PEER_HEADS * PEER_TOPK
    row = lambda i: (i, 0)
    out = jax.ShapeDtypeStruct
    return pl.pallas_call(
        _peer_topk_kernel,
        grid=(n // tm,),
        in_specs=[pl.BlockSpec((tm, q.shape[1]), row),
                  pl.BlockSpec(keys.shape, lambda i: (0, 0, 0))],
        out_specs=[pl.BlockSpec((tm, slots), row)] * 3,
        out_shape=[out((n, slots), I32), out((n, slots), I32), out((n, slots), F32)],
        scratch_shapes=[pltpu.VMEM((slots, tm), I32), pltpu.VMEM((slots, tm), I32), pltpu.VMEM((slots, tm), F32)],
        compiler_params=_cparams("parallel"),
        name="peer_topk",
    )(q, keys)


def _peer_w_kernel(i1_ref, i2_ref, g_ref, w_ref, w3):
    nk = PEER_N_KEYS
    tb = i1_ref.shape[0]
    grp = 2 * SUBLANES
    rid = lax.broadcasted_iota(I32, (nk, i1_ref.shape[1]), 0)
    sub = lax.broadcasted_iota(I32, (SUBLANES, nk), 0)

    def rows_to_tokens(v):
        v = list(v)
        for dist in (4, 2, 1):
            keep = (sub & dist) == 0
            for k in range(SUBLANES):
                if k & dist == 0:
                    a, b_ = v[k], v[k + dist]
                    v[k] = jnp.where(keep, a, pltpu.roll(b_, dist, 0))
                    v[k + dist] = jnp.where(keep, pltpu.roll(a, SUBLANES - dist, 0), b_)
        return v

    def group(gi, carry):
        t0 = pl.multiple_of(gi * grp, grp)
        i1 = i1_ref[pl.ds(t0, grp), :]
        i2 = i2_ref[pl.ds(t0, grp), :]
        g = g_ref[pl.ds(t0, grp), :]
        for k in range(grp):
            left = jnp.where(rid == i1[k:k + 1], g[k:k + 1], 0.0).astype(BF16)
            right = jnp.where(rid == i2[k:k + 1], 1.0, 0.0).astype(BF16)
            w3[k * nk:(k + 1) * nk, :] = _dot_nt(left, right)
        for j in range(nk // SUBLANES):
            halves = []
            for half in range(grp // SUBLANES):
                tiles = [w3[(half * SUBLANES + k) * nk + j * SUBLANES:(half * SUBLANES + k) * nk + (j + 1) * SUBLANES, :]
                         for k in range(SUBLANES)]
                halves.append(rows_to_tokens(tiles))
            for i in range(SUBLANES):
                r = j * SUBLANES + i
                tile = jnp.concatenate([h[i] for h in halves], axis=0)
                w_ref[pl.ds(t0, grp), r * nk:(r + 1) * nk] = tile.astype(w_ref.dtype)
        return carry

    lax.fori_loop(0, tb // grp, group, 0, unroll=4)


def _peer_w(i1, i2, g, tb):
    n, slots = i1.shape
    ne = PEER_N_KEYS * PEER_N_KEYS
    row = lambda i: (i, 0)
    return pl.pallas_call(
        _peer_w_kernel,
        grid=(n // tb,),
        in_specs=[pl.BlockSpec((tb, slots), row)] * 3,
        out_specs=pl.BlockSpec((tb, ne), row),
        out_shape=jax.ShapeDtypeStruct((n, ne), BF16),
        scratch_shapes=[pltpu.VMEM((2 * SUBLANES * PEER_N_KEYS, PEER_N_KEYS), F32)],
        compiler_params=_cparams("parallel"),
        name="peer_w",
    )(i1, i2, g)


def _gelu_tanh(x):
    return 0.5 * x * (1.0 + jnp.tanh(0.7978845608028654 * (x + 0.044715 * (x * x * x))))


def _peer_dense_kernel(*refs, retrieve_next):
    if retrieve_next:
        (h_ref, u_ref, v_ref, w_ref, x_ref, gate_ref, nw_ref, qn_ref, keys_ref,
         o_ref, i1_ref, i2_ref, g_ref, acc, a_buf, s1_s, n1_s, i1_s, i2_s, g_s) = refs
    else:
        h_ref, u_ref, v_ref, w_ref, x_ref, gate_ref, nw_ref, o_ref, acc, a_buf = refs
    j = pl.program_id(1)
    nj = pl.num_programs(1) - 1
    k = PEER_TOPK

    def score(slot):
        a_buf[slot] = jnp.dot(h_ref[...], u_ref[...], preferred_element_type=F32)

    def finish(slot):
        act = (_gelu_tanh(a_buf[slot]) * w_ref[...].astype(F32)).astype(BF16)
        acc[...] += jnp.dot(act, v_ref[...], preferred_element_type=F32)

    def retrieval_chains(half):
        if not retrieve_next:
            return [], 0
        r0 = pl.multiple_of((j // 2) * k, k)

        def chain(t0):
            tl = slice(t0, t0 + LANES)
            s, n = yield from _top_rows_gen(_dot_nt(keys_ref[...], qn_ref[tl, :]), k)
            if half == 0:
                s1_s[:, tl] = s
                n1_s[:, tl] = n
            else:
                e1, e2, gates = yield from _pair_topk_gen(s1_s[:, tl], n1_s[:, tl], s, n)
                i1_s[pl.ds(r0, k), tl] = e1
                i2_s[pl.ds(r0, k), tl] = e2
                g_s[pl.ds(r0, k), tl] = gates

        rounds = (k if half == 0 else 2 * k) + 1
        return [chain(t0) for t0 in range(0, qn_ref.shape[0], LANES)], rounds

    def interleave(main, filler):
        chains, rounds = filler
        units = [c for c in chains for _ in range(rounds)]
        total = sum(w for _, w in main)
        done, seen = 0, 0.0
        for piece, weight in main:
            seen += weight
            upto = len(units) if seen >= total else int(seen * len(units) / total)
            for c in units[done:upto]:
                next(c, None)
            done = upto
            piece()

    def score_pieces(slot):
        half_n = a_buf.shape[2] // 2

        def piece(c0):
            a_buf[slot, :, c0:c0 + half_n] = _dot_nt(h_ref[...], u_ref[c0:c0 + half_n, :])
        return [(functools.partial(piece, c0), 1.0) for c0 in (0, half_n)]

    def finish_pieces(slot):
        quarter = acc.shape[1] // 4
        cell = {}

        def piece(c0):
            if "act" not in cell:
                cell["act"] = (_gelu_tanh(a_buf[slot]) * w_ref[...].astype(F32)).astype(BF16)
            acc[:, c0:c0 + quarter] += jnp.dot(cell["act"], v_ref[:, c0:c0 + quarter], preferred_element_type=F32)
        return [(functools.partial(piece, c0), 0.25 if c0 == 0 else 1.0) for c0 in range(0, acc.shape[1], quarter)]

    @pl.when(j == 0)
    def _():
        acc[...] = jnp.zeros_like(acc)
        interleave(score_pieces(0), retrieval_chains(0))

    for parity in range(2):
        @pl.when((j > 0) & (j < nj) & (j % 2 == parity))
        def _():
            interleave(score_pieces(parity) + finish_pieces(1 - parity), retrieval_chains(parity))

    @pl.when(j == nj)
    def _():
        finish((nj - 1) % 2)
        x2 = x_ref[...] + gate_ref[...] * acc[...]
        o_ref[...] = x2 * lax.rsqrt(jnp.mean(x2 * x2, axis=-1, keepdims=True) + EPS) * nw_ref[...]
        if retrieve_next:
            i1_ref[...] = i1_s[...].T
            i2_ref[...] = i2_s[...].T
            g_ref[...] = g_s[...].T


def _peer_dense(h2, u, v, w, x1, gate, norm_f_w, seq, tm, te, tile0, n_tiles, retrieval=None):
    n, d = h2.shape
    ne = u.shape[0]
    nj = ne // te
    assert nj % 2 == 0
    tiles_per_batch = seq // tm
    row = lambda i, j: (tile0 + i, 0)
    scored = lambda i, j: (jnp.minimum(j, nj - 1), 0)
    finished = lambda j: jnp.maximum(j - 1, 0)
    in_specs = [pl.BlockSpec((tm, d), row),
                pl.BlockSpec((te, d), scored),
                pl.BlockSpec((te, d), lambda i, j: (finished(j), 0)),
                pl.BlockSpec((tm, te), lambda i, j: (i, finished(j))),
                pl.BlockSpec((tm, d), row),
                pl.BlockSpec((None, 1, d), lambda i, j: ((tile0 + i) // tiles_per_batch, 0, 0)),
                pl.BlockSpec((1, d), lambda i, j: (0, 0))]
    args = [h2, u, v, w, x1, gate, norm_f_w.reshape(1, d)]
    out_specs = [pl.BlockSpec((tm, d), row)]
    out_shape = [jax.ShapeDtypeStruct((n, d), F32)]
    scratch = [pltpu.VMEM((tm, d), F32), pltpu.VMEM((2, tm, te), F32)]
    if retrieval is not None:
        q, keys = retrieval
        slots = PEER_HEADS * PEER_TOPK
        assert nj == keys.shape[0]
        half_head = lambda j: jnp.minimum(j, nj - 1)
        in_specs += [pl.BlockSpec((tm, PEER_D_HALF), lambda i, j: (tile0 + n_tiles + i, half_head(j))),
                     pl.BlockSpec((None, PEER_N_KEYS, PEER_D_HALF), lambda i, j: (half_head(j), 0, 0))]
        args += [q, keys]
        out_specs += [pl.BlockSpec((tm, slots), lambda i, j: (i, 0))] * 3
        out_shape += [jax.ShapeDtypeStruct((n_tiles * tm, slots), t) for t in (I32, I32, F32)]
        scratch += [pltpu.VMEM((PEER_TOPK, tm), F32), pltpu.VMEM((PEER_TOPK, tm), I32),
                    pltpu.VMEM((slots, tm), I32), pltpu.VMEM((slots, tm), I32), pltpu.VMEM((slots, tm), F32)]
    outs = pl.pallas_call(
        functools.partial(_peer_dense_kernel, retrieve_next=retrieval is not None),
        grid=(n_tiles, nj + 1),
        in_specs=in_specs,
        out_specs=out_specs,
        out_shape=out_shape,
        scratch_shapes=scratch,
        input_output_aliases={4: 0},
        compiler_params=_cparams("parallel", "arbitrary"),
        name="peer_dense",
    )(*args)
    return outs[0], tuple(outs[1:])


def _rope_tables(length):
    t = jnp.arange(length, dtype=jnp.int32)
    row = (t // GRID_W).astype(F32)
    col = (t % GRID_W).astype(F32)
    n_freq = HEAD_DIM // 4
    inv_freq = ROPE_BASE ** (-jnp.arange(n_freq, dtype=F32) / n_freq)
    ang = jnp.concatenate([row[:, None] * inv_freq, col[:, None] * inv_freq], axis=-1)
    cos, sin = jnp.cos(ang), jnp.sin(ang)
    return jnp.concatenate([cos, cos], axis=-1), jnp.concatenate([-sin, sin], axis=-1)


def _layer(x, ctx, c_rows, rope_c, rope_s, ada_w, ada_b, norm1_w, w_in, q_norm_w, k_norm_w, ret_decay_fwd,
           ret_decay_bwd, ret_norm_w, w_attn_branch, w_ret_branch, w_merge_out, norm2_w, peer_w_q, peer_keys,
           peer_u, peer_v, norm_f_w):
    b, l, d = x.shape
    lc = ctx.shape[1]
    mod = _ada(c_rows, ada_w, ada_b)
    mod_x = [mod[:b, i * d:(i + 1) * d].reshape(b, 1, d) for i in range(6)]
    mod_c = [mod[b:b + 1, i * d:(i + 1) * d].reshape(1, 1, d) for i in range(2)]

    w_in_b = _regroup_w_in(w_in).astype(BF16)
    n_in = w_in.shape[1]
    tn = n_in // 4
    p = _in_proj(x, norm1_w, mod_x[0], mod_x[1], w_in_b, tm=min(512, l), tn=tn)
    pc = _in_proj(ctx, norm1_w, mod_c[0], mod_c[1], w_in_b, tm=min(256, lc), tn=tn)

    attn = _attention(p, pc, rope_c, rope_s, q_norm_w, k_norm_w, tq=min(256, l))
    ret = _retention(p, pc, rope_c, rope_s, ret_decay_fwd, ret_decay_bwd, ret_norm_w)
    x1, h2 = _merge(attn, ret, p, x, mod_x[2], mod_x[3], mod_x[4], norm2_w, w_attn_branch.astype(BF16),
                    w_ret_branch.astype(BF16), w_merge_out.astype(BF16), tm=min(256, l))

    n = b * l
    h2f = h2.reshape(n, d)
    keys = peer_keys.reshape(PEER_HEADS * 2, PEER_N_KEYS, PEER_D_HALF).astype(BF16)
    q = _peer_q(h2f, peer_w_q.astype(BF16), tm=min(1024, l))
    u, v = peer_u.astype(BF16), peer_v.astype(BF16)
    tm = min(512, l)
    n_tiles = n // tm
    n_chunks = next(c for c in (8, 4, 2, 1) if n_tiles % c == 0)
    tiles = n_tiles // n_chunks
    sel = _peer_topk(q, keys, n=tiles * tm, tm=256)
    out = x1.reshape(n, d)
    for chunk in range(n_chunks):
        w = _peer_w(*sel, tb=128)
        retrieval = (q, keys) if chunk + 1 < n_chunks else None
        out, sel = _peer_dense(h2f, u, v, w, out, mod_x[5], norm_f_w, seq=l, tm=tm, te=1024,
                               tile0=chunk * tiles, n_tiles=tiles, retrieval=retrieval)
    return out.reshape(b, l, d)


def kernel(x, c, ctx, c_ctx, ada_w, ada_b, norm1_w, w_in, q_norm_w, k_norm_w, ret_decay_fwd, ret_decay_bwd,
           ret_norm_w, w_attn_branch, w_ret_branch, w_merge_out, norm2_w, peer_w_q, peer_keys, peer_u, peer_v,
           norm_f_w):
    depth = ada_w.shape[0]
    assert depth == 1, "context-stream update between layers is not implemented"
    b, l, d = x.shape
    rows = -(-(b + 1) // SUBLANES) * SUBLANES
    c_rows = jnp.zeros((rows, d), F32).at[:b].set(c).at[b].set(c_ctx)
    rope_c, rope_s = _rope_tables(l)
    return _layer(x, ctx, c_rows, rope_c, rope_s, ada_w[0], ada_b[0], norm1_w[0], w_in[0], q_norm_w[0], k_norm_w[0],
                  ret_decay_fwd[0], ret_decay_bwd[0], ret_norm_w[0], w_attn_branch[0], w_ret_branch[0],
                  w_merge_out[0], norm2_w[0], peer_w_q[0], peer_keys[0], peer_u[0], peer_v[0], norm_f_w)
```

```python
import functools

import jax
import jax.numpy as jnp
from jax import lax
from jax.experimental import pallas as pl
from jax.experimental.pallas import tpu as pltpu

F32 = jnp.float32
BF16 = jnp.bfloat16
I32 = jnp.int32

EPS = 1e-6
GRID_W = 64
ROPE_BASE = 10000.0
ATTN_HEADS = 8
ATTN_KV_HEADS = 2
HEAD_DIM = 128
RET_HEADS = 8
RET_DK = 128
RET_DV = 256
RET_CHUNK = 128
PEER_HEADS = 8
PEER_N_KEYS = 128
PEER_D_HALF = 128
PEER_TOPK = 16

LANES = 128
SUBLANES = 8
VMEM_LIMIT = 60 * 1024 * 1024

ATTN_Q_W = ATTN_HEADS * HEAD_DIM
ATTN_KV_W = ATTN_KV_HEADS * HEAD_DIM
RET_QK_W = RET_HEADS * RET_DK
RET_V_W = RET_HEADS * RET_DV
D_MODEL = 2048
_SRC_ORDER = (("qa", ATTN_Q_W), ("ka", ATTN_KV_W), ("va", ATTN_KV_W), ("qr", RET_QK_W), ("kr", RET_QK_W),
              ("vr", RET_V_W), ("gr", RET_V_W), ("ga", D_MODEL), ("gt", D_MODEL))
_DST_ORDER = ("ga", "gt", "vr", "gr", "qr", "kr", "qa", "ka", "va")


def _offsets(order, widths):
    off, o = {}, 0
    for name in order:
        off[name] = o
        o += widths[name]
    return off


_WIDTH = dict(_SRC_ORDER)
_SRC_OFF = _offsets([n for n, _ in _SRC_ORDER], _WIDTH)
OFF = _offsets(_DST_ORDER, _WIDTH)


def _regroup_w_in(w_in):
    return jnp.concatenate([w_in[:, _SRC_OFF[n]:_SRC_OFF[n] + _WIDTH[n]] for n in _DST_ORDER], axis=1)


def _cparams(*sem):
    return pltpu.CompilerParams(dimension_semantics=sem, vmem_limit_bytes=VMEM_LIMIT)


def _sigmoid(x):
    return 1.0 / (1.0 + jnp.exp(-x))


def _silu(x):
    return x * _sigmoid(x)


def _rot_half(x):
    return pltpu.roll(x, HEAD_DIM // 2, 1)


def _dot_nt(a, b):
    return lax.dot_general(a, b, (((1,), (1,)), ((), ())), preferred_element_type=F32)


def _dot_tn(a, b):
    return lax.dot_general(a, b, (((0,), (0,)), ((), ())), preferred_element_type=F32)


def _ada_kernel(c_ref, w_ref, b_ref, o_ref):
    sc = _silu(c_ref[...]).astype(BF16)
    o_ref[...] = jnp.dot(sc, w_ref[...].astype(BF16), preferred_element_type=F32) + b_ref[...]


def _ada(c_rows, ada_w, ada_b):
    rows, d = c_rows.shape
    n = ada_w.shape[1]
    tn = 1536
    return pl.pallas_call(
        _ada_kernel,
        grid=(n // tn,),
        in_specs=[pl.BlockSpec((rows, d), lambda j: (0, 0)),
                  pl.BlockSpec((d, tn), lambda j: (0, j)),
                  pl.BlockSpec((1, tn), lambda j: (0, j))],
        out_specs=pl.BlockSpec((rows, tn), lambda j: (0, j)),
        out_shape=jax.ShapeDtypeStruct((rows, n), F32),
        compiler_params=_cparams("arbitrary"),
        name="ada",
    )(c_rows, ada_w, ada_b.reshape(1, n))


def _in_proj_kernel(x_ref, nw_ref, shift_ref, scale_ref, w_ref, o_ref, h_ref):
    @pl.when(pl.program_id(2) == 0)
    def _():
        x = x_ref[...]
        y = x * lax.rsqrt(jnp.mean(x * x, axis=-1, keepdims=True) + EPS) * nw_ref[...]
        h_ref[...] = (y * (1.0 + scale_ref[...]) + shift_ref[...]).astype(BF16)

    o_ref[...] = jnp.dot(h_ref[...], w_ref[...], preferred_element_type=F32).astype(o_ref.dtype)


def _in_proj(x, norm_w, shift, scale, w, tm, tn):
    b, l, d = x.shape
    n = w.shape[1]
    per_batch = shift.shape[0] > 1
    mod_map = (lambda bi, i, j: (bi, 0, 0)) if per_batch else (lambda bi, i, j: (0, 0, 0))
    return pl.pallas_call(
        _in_proj_kernel,
        grid=(b, l // tm, n // tn),
        in_specs=[pl.BlockSpec((None, tm, d), lambda bi, i, j: (bi, i, 0)),
                  pl.BlockSpec((1, d), lambda bi, i, j: (0, 0)),
                  pl.BlockSpec((None, 1, d), mod_map),
                  pl.BlockSpec((None, 1, d), mod_map),
                  pl.BlockSpec((d, tn), lambda bi, i, j: (0, j))],
        out_specs=pl.BlockSpec((None, tm, tn), lambda bi, i, j: (bi, i, j)),
        out_shape=jax.ShapeDtypeStruct((b, l, n), BF16),
        scratch_shapes=[pltpu.VMEM((tm, d), BF16)],
        compiler_params=_cparams("parallel", "parallel", "arbitrary"),
        name="in_proj",
    )(x, norm_w.reshape(1, d), shift, scale, w)


def _head_rms(x, w):
    return x * lax.rsqrt(jnp.mean(x * x, axis=-1, keepdims=True) + EPS) * w


def _attn_kernel(q_ref, kc_ref, k_ref, vc_ref, v_ref, cq_ref, sq_ref, ck_ref, sk_ref, qnw_ref, knw_ref,
                 o_ref, k_s, v_s, *, lc, group):
    n_kv = k_s.shape[0]

    @pl.when(pl.program_id(1) == 0)
    def _():
        knw = knw_ref[...]
        for kv in range(n_kv):
            hs = slice(kv * HEAD_DIM, (kv + 1) * HEAD_DIM)
            k_s[kv, 0:lc, :] = _head_rms(kc_ref[:, hs].astype(F32), knw).astype(BF16)
            kn = _head_rms(k_ref[:, hs].astype(F32), knw)
            k_s[kv, lc:, :] = (kn * ck_ref[...] + _rot_half(kn) * sk_ref[...]).astype(BF16)
            v_s[kv, 0:lc, 0:HEAD_DIM] = vc_ref[:, hs]
            v_s[kv, lc:, 0:HEAD_DIM] = v_ref[:, hs]
            v_s[kv, :, HEAD_DIM:] = jnp.ones((v_s.shape[1], HEAD_DIM), BF16)

    c = (HEAD_DIM ** -0.5) * 1.4426950408889634
    qnw = qnw_ref[...]
    cq = cq_ref[...] * c
    sq = sq_ref[...] * c
    def scores(g):
        qn = _head_rms(q_ref[:, g * HEAD_DIM:(g + 1) * HEAD_DIM].astype(F32), qnw)
        qr = (qn * cq + _rot_half(qn) * sq).astype(BF16)
        return _dot_nt(qr, k_s[g // group])

    def output(g, s):
        m = jnp.max(s, axis=-1, keepdims=True)
        p = jnp.exp2((s - m).astype(BF16))
        ol = jnp.dot(p, v_s[g // group], preferred_element_type=F32)
        o_ref[:, g * HEAD_DIM:(g + 1) * HEAD_DIM] = (
            ol[:, :HEAD_DIM] * (1.0 / ol[:, HEAD_DIM:HEAD_DIM + 1])).astype(o_ref.dtype)

    heads = n_kv * group
    ahead = 1
    pending = [scores(g) for g in range(min(ahead, heads))]
    for g in range(heads):
        if g + ahead < heads:
            pending.append(scores(g + ahead))
        output(g, pending.pop(0))


def _attention(p, pc, rope_c, rope_s, q_norm_w, k_norm_w, tq):
    b, l, _ = p.shape
    lc = pc.shape[1]
    group = ATTN_HEADS // ATTN_KV_HEADS
    kq0 = OFF["qa"] // ATTN_Q_W
    kk0 = OFF["ka"] // ATTN_KV_W
    kv0 = OFF["va"] // ATTN_KV_W
    full = lambda bi, i: (0, 0)
    return pl.pallas_call(
        functools.partial(_attn_kernel, lc=lc, group=group),
        grid=(b, l // tq),
        in_specs=[pl.BlockSpec((None, tq, ATTN_Q_W), lambda bi, i: (bi, i, kq0)),
                  pl.BlockSpec((None, lc, ATTN_KV_W), lambda bi, i: (bi, 0, kk0)),
                  pl.BlockSpec((None, l, ATTN_KV_W), lambda bi, i: (bi, 0, kk0)),
                  pl.BlockSpec((None, lc, ATTN_KV_W), lambda bi, i: (bi, 0, kv0)),
                  pl.BlockSpec((None, l, ATTN_KV_W), lambda bi, i: (bi, 0, kv0)),
                  pl.BlockSpec((tq, HEAD_DIM), lambda bi, i: (i, 0)),
                  pl.BlockSpec((tq, HEAD_DIM), lambda bi, i: (i, 0)),
                  pl.BlockSpec((l, HEAD_DIM), full),
                  pl.BlockSpec((l, HEAD_DIM), full),
                  pl.BlockSpec((1, HEAD_DIM), full),
                  pl.BlockSpec((1, HEAD_DIM), full)],
        out_specs=pl.BlockSpec((None, tq, ATTN_Q_W), lambda bi, i: (bi, i, 0)),
        out_shape=jax.ShapeDtypeStruct((b, l, ATTN_Q_W), BF16),
        scratch_shapes=[pltpu.VMEM((ATTN_KV_HEADS, lc + l, HEAD_DIM), BF16),
                        pltpu.VMEM((ATTN_KV_HEADS, lc + l, 2 * HEAD_DIM), BF16)],
        compiler_params=_cparams("parallel", "arbitrary"),
        name="attention",
    )(p, pc, p, pc, p, rope_c, rope_s, rope_c, rope_s, q_norm_w.reshape(1, HEAD_DIM), k_norm_w.reshape(1, HEAD_DIM))


def _ret_kernel(q_ref, k_ref, v_ref, g_ref, kc_ref, vc_ref, cos_ref, sin_ref, z_ref, nw_ref, o_ref,
                q_s, k_s, acc, st_f, st_b, *, nc, lc):
    ch = RET_CHUNK
    assert nc % 2 == 0
    scale = RET_DK ** -0.5
    lg_f = -jnp.exp(z_ref[0])
    lg_b = -jnp.exp(z_ref[1])
    row = lax.broadcasted_iota(I32, (ch, ch), 0).astype(F32)
    col = lax.broadcasted_iota(I32, (ch, ch), 1).astype(F32)
    d = row - col
    intra_f = jnp.where(d >= 0, jnp.exp(lg_f * jnp.maximum(d, 0.0)), 0.0)
    intra_b = jnp.where(d <= 0, jnp.exp(lg_b * jnp.maximum(-d, 0.0)), 0.0)

    def wide(a):
        return jnp.concatenate([a] * (RET_DV // LANES), axis=1)

    qd_f = wide(jnp.exp(lg_f * (row + 1.0)))
    kd_f = jnp.exp(lg_f * (ch - 1.0 - row))
    cd_f = wide(jnp.exp(lg_f * float(ch)))
    qd_b = wide(jnp.exp(lg_b * (ch - row)))
    kd_b = jnp.exp(lg_b * row)
    cd_b = wide(jnp.exp(lg_b * float(ch)))

    j = lax.broadcasted_iota(I32, (lc, RET_DK), 0).astype(F32)
    kc = kc_ref[...].astype(F32) * scale
    vc = vc_ref[...]
    st_f[...] = _dot_tn((kc * jnp.exp(lg_f * (lc - 1.0 - j))).astype(BF16), vc)
    st_b[...] = _dot_tn((kc * jnp.exp(lg_b * j)).astype(BF16), vc)

    nw = nw_ref[...]
    fwd = (st_f, intra_f, qd_f, kd_f, cd_f)
    bwd = (st_b, intra_b, qd_b, kd_b, cd_b)

    span = 4 if (nc // 2) % 4 == 0 else (2 if (nc // 2) % 2 == 0 else 1)

    def block(b0, first_visit):
        visits = []
        for u in range(span):
            i = b0 * span + u
            visits.append((pl.multiple_of(i * ch, ch), fwd))
            visits.append((pl.multiple_of((nc - 1 - i) * ch, ch), bwd))
        loaded = []
        for r0, _ in visits:
            rows = pl.ds(r0, ch)
            if first_visit:
                cs, sn = cos_ref[rows, :], sin_ref[rows, :]
                qf, kf = q_ref[rows, :].astype(F32), k_ref[rows, :].astype(F32)
                qb = (qf * cs + _rot_half(qf) * sn).astype(BF16)
                kr = (kf * cs + _rot_half(kf) * sn) * scale
                extra = None
            else:
                qb, kr = q_s[rows, :], k_s[rows, :]
                extra = (acc[rows, :], g_ref[rows, :])
            loaded.append((qb, kr, v_ref[rows, :], extra))
        raw = [_dot_nt(qb, kr.astype(BF16)) for qb, kr, _, _ in loaded]
        upd = [_dot_tn((kr * direction[3]).astype(BF16), vb)
               for (_, direction), (_, kr, vb, _) in zip(visits, loaded)]
        state = {id(fwd): st_f[...], id(bwd): st_b[...]}
        before = []
        for (_, direction), u_ in zip(visits, upd):
            before.append(state[id(direction)])
            state[id(direction)] = state[id(direction)] * direction[4] + u_
        cross = [jnp.dot(qb, s0.astype(BF16), preferred_element_type=F32)
                 for (qb, _, _, _), s0 in zip(loaded, before)]
        outs = []
        for (r0, direction), (qb, kr, vb, extra), s_raw, cr in zip(visits, loaded, raw, cross):
            _, intra, qd, kd, cd = direction
            o = jnp.dot((s_raw * intra).astype(BF16), vb, preferred_element_type=F32) + cr * qd
            if not first_visit:
                o = o + extra[0]
                mu = jnp.mean(o, axis=-1, keepdims=True)
                oc = o - mu
                var = jnp.mean(oc * oc, axis=-1, keepdims=True)
                o = (_silu(extra[1].astype(F32)) * (oc * lax.rsqrt(var + EPS) * nw)).astype(o_ref.dtype)
            outs.append(o)
        for (r0, _), (qb, kr, _, _), o in zip(visits, loaded, outs):
            rows = pl.ds(r0, ch)
            if first_visit:
                q_s[rows, :] = qb
                k_s[rows, :] = kr
                acc[rows, :] = o
            else:
                o_ref[rows, :] = o
        st_f[...] = state[id(fwd)]
        st_b[...] = state[id(bwd)]

    n_blocks = (nc // 2) // span

    def first_half(b0, carry):
        block(b0, True)
        return carry

    def second_half(b0, carry):
        block(b0, False)
        return carry

    lax.fori_loop(0, n_blocks, first_half, 0)
    lax.fori_loop(n_blocks, 2 * n_blocks, second_half, 0)


def _retention(p, pc, rope_c, rope_s, decay_fwd, decay_bwd, ret_norm_w):
    b, l, _ = p.shape
    lc = pc.shape[1]
    nc = l // RET_CHUNK
    z = jnp.broadcast_to(jnp.stack([decay_fwd, decay_bwd])[:, :, None, None], (2, RET_HEADS, 1, LANES))
    kq0, kk0 = OFF["qr"] // RET_DK, OFF["kr"] // RET_DK
    kv0, kg0 = OFF["vr"] // RET_DV, OFF["gr"] // RET_DV
    full = lambda bi, h: (0, 0)
    return pl.pallas_call(
        functools.partial(_ret_kernel, nc=nc, lc=lc),
        grid=(b, RET_HEADS),
        in_specs=[pl.BlockSpec((None, l, RET_DK), lambda bi, h: (bi, 0, kq0 + h)),
                  pl.BlockSpec((None, l, RET_DK), lambda bi, h: (bi, 0, kk0 + h)),
                  pl.BlockSpec((None, l, RET_DV), lambda bi, h: (bi, 0, kv0 + h)),
                  pl.BlockSpec((None, l, RET_DV), lambda bi, h: (bi, 0, kg0 + h)),
                  pl.BlockSpec((None, lc, RET_DK), lambda bi, h: (bi, 0, kk0 + h)),
                  pl.BlockSpec((None, lc, RET_DV), lambda bi, h: (bi, 0, kv0 + h)),
                  pl.BlockSpec((l, RET_DK), full),
                  pl.BlockSpec((l, RET_DK), full),
                  pl.BlockSpec((2, None, 1, LANES), lambda bi, h: (0, h, 0, 0)),
                  pl.BlockSpec((1, RET_DV), lambda bi, h: (0, h))],
        out_specs=pl.BlockSpec((None, l, RET_DV), lambda bi, h: (bi, 0, h)),
        out_shape=jax.ShapeDtypeStruct((b, l, RET_V_W), BF16),
        scratch_shapes=[pltpu.VMEM((l, RET_DK), BF16), pltpu.VMEM((l, RET_DK), F32), pltpu.VMEM((l, RET_DV), F32),
                        pltpu.VMEM((RET_DK, RET_DV), F32), pltpu.VMEM((RET_DK, RET_DV), F32)],
        compiler_params=_cparams("parallel", "parallel"),
        name="retention",
    )(p, p, p, p, pc, pc, rope_c, rope_s, z, ret_norm_w.reshape(1, RET_V_W))


def _merge_kernel(attn_ref, ret_ref, ga_ref, gr_ref, x_ref, gate_ref, shift_ref, scale_ref, nw_ref,
                  wa_ref, wr_ref, wm_ref, x1_ref, h2_ref):
    a = jnp.dot(attn_ref[...], wa_ref[...], preferred_element_type=F32)
    r = jnp.dot(ret_ref[...], wr_ref[...], preferred_element_type=F32)
    mixed = _sigmoid(ga_ref[...].astype(F32)) * a + _sigmoid(gr_ref[...].astype(F32)) * r
    y = jnp.dot(mixed.astype(BF16), wm_ref[...], preferred_element_type=F32)
    x1 = x_ref[...] + gate_ref[...] * y
    x1_ref[...] = x1
    hn = x1 * lax.rsqrt(jnp.mean(x1 * x1, axis=-1, keepdims=True) + EPS) * nw_ref[...]
    h2_ref[...] = (hn * (1.0 + scale_ref[...]) + shift_ref[...]).astype(BF16)


def _merge(attn, ret, p, x, gate, shift2, scale2, norm2_w, wa, wr, wm, tm):
    b, l, d = x.shape
    assert d == D_MODEL
    kga, kgt = OFF["ga"] // d, OFF["gt"] // d
    row = lambda bi, i: (bi, i, 0)
    mod = lambda bi, i: (bi, 0, 0)
    const = lambda bi, i: (0, 0)
    once = pl.Buffered(1)
    return pl.pallas_call(
        _merge_kernel,
        grid=(b, l // tm),
        in_specs=[pl.BlockSpec((None, tm, ATTN_Q_W), row),
                  pl.BlockSpec((None, tm, RET_V_W), row),
                  pl.BlockSpec((None, tm, d), lambda bi, i: (bi, i, kga)),
                  pl.BlockSpec((None, tm, d), lambda bi, i: (bi, i, kgt)),
                  pl.BlockSpec((None, tm, d), row),
                  pl.BlockSpec((None, 1, d), mod),
                  pl.BlockSpec((None, 1, d), mod),
                  pl.BlockSpec((None, 1, d), mod),
                  pl.BlockSpec((1, d), const),
                  pl.BlockSpec(wa.shape, const, pipeline_mode=once),
                  pl.BlockSpec(wr.shape, const, pipeline_mode=once),
                  pl.BlockSpec(wm.shape, const, pipeline_mode=once)],
        out_specs=[pl.BlockSpec((None, tm, d), row), pl.BlockSpec((None, tm, d), row)],
        out_shape=[jax.ShapeDtypeStruct((b, l, d), F32), jax.ShapeDtypeStruct((b, l, d), BF16)],
        compiler_params=_cparams("parallel", "parallel"),
        name="merge",
    )(attn, ret, p, p, x, gate, shift2, scale2, norm2_w.reshape(1, d), wa, wr, wm)


def _run(gen):
    try:
        while True:
            next(gen)
    except StopIteration as stop:
        return stop.value


def _top_rows_gen(s, k):
    n = s.shape[0]
    rid = lax.broadcasted_iota(I32, s.shape, 0)
    vals, ids = [], []
    for _ in range(k):
        m = jnp.max(s, axis=0, keepdims=True)
        sel = jnp.min(jnp.where(s == m, rid, n), axis=0, keepdims=True)
        vals.append(m)
        ids.append(sel)
        s = jnp.where(rid == sel, -jnp.inf, s)
        yield
    return jnp.concatenate(vals, axis=0), jnp.concatenate(ids, axis=0)


def _top_rows(s, k):
    return _run(_top_rows_gen(s, k))


def _pair_candidates(s1, s2):
    k = PEER_TOPK
    t = s1.shape[1]
    r8 = lax.broadcasted_iota(I32, (SUBLANES, t), 0)
    r16 = lax.broadcasted_iota(I32, (k, t), 0)
    neg = -jnp.inf
    sums = [s1[0:1] + s2]
    flat = [r16]
    for a, nb in ((1, 8), (2, 5), (3, 4)):
        sums.append(jnp.where(r8 < nb, s1[a:a + 1] + s2[0:8], neg))
        flat.append(a * k + r8)
    sums.append(s1[8:16] + s2[0:1])
    flat.append((r8 + 8) * k)
    for b_, lo, hi in ((0, 4, 8), (1, 4, 8), (2, 4, 5)):
        sums.append(jnp.where((r8 >= lo) & (r8 < hi), s1[0:8] + s2[b_:b_ + 1], neg))
        flat.append(r8 * k + b_)
    return jnp.concatenate(sums, axis=0), jnp.concatenate(flat, axis=0)


def _pair_topk_gen(s1, n1, s2, n2):
    k = PEER_TOPK
    cand, flat = _pair_candidates(s1, s2)
    tops, picks = [], []
    for _ in range(k):
        m = jnp.max(cand, axis=0, keepdims=True)
        f = jnp.min(jnp.where(cand == m, flat, k * k), axis=0, keepdims=True)
        cand = jnp.where(flat == f, -jnp.inf, cand)
        tops.append(m)
        picks.append(f)
        yield
    top = jnp.concatenate(tops, axis=0)
    pick = jnp.concatenate(picks, axis=0)
    pa = pick // k
    pb = pick - pa * k
    e1 = jnp.zeros_like(pick)
    e2 = jnp.zeros_like(pick)
    for a in range(k):
        e1 = jnp.where(pa == a, n1[a:a + 1], e1)
        e2 = jnp.where(pb == a, n2[a:a + 1], e2)
    ex = jnp.exp(top - top[0:1])
    return e1, e2, ex / jnp.sum(ex, axis=0, keepdims=True)


def _pair_topk(s1, n1, s2, n2):
    return _run(_pair_topk_gen(s1, n1, s2, n2))


def _peer_q_kernel(h_ref, wq_ref, q_ref):
    q_ref[...] = jnp.dot(h_ref[...], wq_ref[...], preferred_element_type=F32).astype(q_ref.dtype)


def _peer_q(h2, wq, tm):
    n, d = h2.shape
    return pl.pallas_call(
        _peer_q_kernel,
        grid=(n // tm,),
        in_specs=[pl.BlockSpec((tm, d), lambda i: (i, 0)),
                  pl.BlockSpec(wq.shape, lambda i: (0, 0), pipeline_mode=pl.Buffered(1))],
        out_specs=pl.BlockSpec((tm, wq.shape[1]), lambda i: (i, 0)),
        out_shape=jax.ShapeDtypeStruct((n, wq.shape[1]), BF16),
        compiler_params=_cparams("parallel"),
        name="peer_q",
    )(h2, wq)


def _peer_topk_kernel(q_s, keys_ref, i1_ref, i2_ref, g_ref, i1_s, i2_s, g_s):
    k = PEER_TOPK

    def head(h, carry):
        c0 = pl.multiple_of(h * 2 * PEER_D_HALF, 2 * PEER_D_HALF)
        qh = q_s[:, pl.ds(c0, 2 * PEER_D_HALF)]
        s1, n1 = _top_rows(_dot_nt(keys_ref[2 * h], qh[:, :PEER_D_HALF]), k)
        s2, n2 = _top_rows(_dot_nt(keys_ref[2 * h + 1], qh[:, PEER_D_HALF:]), k)
        e1, e2, gates = _pair_topk(s1, n1, s2, n2)
        r0 = pl.multiple_of(h * k, k)
        g_s[pl.ds(r0, k), :] = gates
        i1_s[pl.ds(r0, k), :] = e1
        i2_s[pl.ds(r0, k), :] = e2
        return carry

    lax.fori_loop(0, PEER_HEADS, head, 0)
    i1_ref[...] = i1_s[...].T
    i2_ref[...] = i2_s[...].T
    g_ref[...] = g_s[...].T


def _peer_topk(q, keys, n, tm):
    slots = PEER_HEADS * PEER_TOPK
    row = lambda i: (i, 0)
    out = jax.ShapeDtypeStruct
    return pl.pallas_call(
        _peer_topk_kernel,
        grid=(n // tm,),
        in_specs=[pl.BlockSpec((tm, q.shape[1]), row),
                  pl.BlockSpec(keys.shape, lambda i: (0, 0, 0))],
        out_specs=[pl.BlockSpec((tm, slots), row)] * 3,
        out_shape=[out((n, slots), I32), out((n, slots), I32), out((n, slots), F32)],
        scratch_shapes=[pltpu.VMEM((slots, tm), I32), pltpu.VMEM((slots, tm), I32), pltpu.VMEM((slots, tm), F32)],
        compiler_params=_cparams("parallel"),
        name="peer_topk",
    )(q, keys)


def _peer_w_kernel(i1_ref, i2_ref, g_ref, w_ref, w3):
    nk = PEER_N_KEYS
    tb = i1_ref.shape[0]
    grp = 2 * SUBLANES
    rid = lax.broadcasted_iota(I32, (nk, i1_ref.shape[1]), 0)
    sub = lax.broadcasted_iota(I32, (SUBLANES, nk), 0)

    def rows_to_tokens(v):
        v = list(v)
        for dist in (4, 2, 1):
            keep = (sub & dist) == 0
            for k in range(SUBLANES):
                if k & dist == 0:
                    a, b_ = v[k], v[k + dist]
                    v[k] = jnp.where(keep, a, pltpu.roll(b_, dist, 0))
                    v[k + dist] = jnp.where(keep, pltpu.roll(a, SUBLANES - dist, 0), b_)
        return v

    def group(gi, carry):
        t0 = pl.multiple_of(gi * grp, grp)
        i1 = i1_ref[pl.ds(t0, grp), :]
        i2 = i2_ref[pl.ds(t0, grp), :]
        g = g_ref[pl.ds(t0, grp), :]
        for k in range(grp):
            left = jnp.where(rid == i1[k:k + 1], g[k:k + 1], 0.0).astype(BF16)
            right = jnp.where(rid == i2[k:k + 1], 1.0, 0.0).astype(BF16)
            w3[k * nk:(k + 1) * nk, :] = _dot_nt(left, right)
        for j in range(nk // SUBLANES):
            halves = []
            for half in range(grp // SUBLANES):
                tiles = [w3[(half * SUBLANES + k) * nk + j * SUBLANES:(half * SUBLANES + k) * nk + (j + 1) * SUBLANES, :]
                         for k in range(SUBLANES)]
                halves.append(rows_to_tokens(tiles))
            for i in range(SUBLANES):
                r = j * SUBLANES + i
                tile = jnp.concatenate([h[i] for h in halves], axis=0)
                w_ref[pl.ds(t0, grp), r * nk:(r + 1) * nk] = tile.astype(w_ref.dtype)
        return carry

    lax.fori_loop(0, tb // grp, group, 0, unroll=4)


def _peer_w(i1, i2, g, tb):
    n, slots = i1.shape
    ne = PEER_N_KEYS * PEER_N_KEYS
    row = lambda i: (i, 0)
    return pl.pallas_call(
        _peer_w_kernel,
        grid=(n // tb,),
        in_specs=[pl.BlockSpec((tb, slots), row)] * 3,
        out_specs=pl.BlockSpec((tb, ne), row),
        out_shape=jax.ShapeDtypeStruct((n, ne), BF16),
        scratch_shapes=[pltpu.VMEM((2 * SUBLANES * PEER_N_KEYS, PEER_N_KEYS), F32)],
        compiler_params=_cparams("parallel"),
        name="peer_w",
    )(i1, i2, g)


def _gelu_tanh(x):
    return 0.5 * x * (1.0 + jnp.tanh(0.7978845608028654 * (x + 0.044715 * (x * x * x))))


def _peer_dense_kernel(*refs, retrieve_next):
    if retrieve_next:
        (h_ref, u_ref, v_ref, w_ref, x_ref, gate_ref, nw_ref, qn_ref, keys_ref,
         o_ref, i1_ref, i2_ref, g_ref, acc, a_buf, s1_s, n1_s, i1_s, i2_s, g_s) = refs
    else:
        h_ref, u_ref, v_ref, w_ref, x_ref, gate_ref, nw_ref, o_ref, acc, a_buf = refs
    j = pl.program_id(1)
    nj = pl.num_programs(1) - 1
    k = PEER_TOPK

    def score(slot):
        a_buf[slot] = jnp.dot(h_ref[...], u_ref[...], preferred_element_type=F32)

    def finish(slot):
        act = (_gelu_tanh(a_buf[slot]) * w_ref[...].astype(F32)).astype(BF16)
        acc[...] += jnp.dot(act, v_ref[...], preferred_element_type=F32)

    def retrieval_chains(half):
        if not retrieve_next:
            return [], 0
        r0 = pl.multiple_of((j // 2) * k, k)
        groups = range(0, qn_ref.shape[0], LANES)

        def chain(t0):
            tl = slice(t0, t0 + LANES)
            s, n = yield from _top_rows_gen(_dot_nt(keys_ref[...], qn_ref[tl, :]), k)
            if half == 0:
                s1_s[:, tl] = s
                n1_s[:, tl] = n
            else:
                e1, e2, gates = yield from _pair_topk_gen(s1_s[:, tl], n1_s[:, tl], s, n)
                i1_s[pl.ds(r0, k), tl] = e1
                i2_s[pl.ds(r0, k), tl] = e2
                g_s[pl.ds(r0, k), tl] = gates

        rounds = (k if half == 0 else 2 * k) + 1
        return [chain(t0) for t0 in groups], rounds

    def interleave(main, filler):
        chains, _ = filler
        roomy = [idx for idx, (_, weight) in enumerate(main) if weight >= 1.0]
        start = {}
        for ci, chain in enumerate(chains):
            start.setdefault(roomy[ci * len(roomy) // len(chains)], []).append(chain)
        for idx, (piece, _) in enumerate(main):
            for chain in start.get(idx, ()):
                _run(chain)
            piece()

    def score_pieces(slot):
        half_n = a_buf.shape[2] // 2

        def piece(c0):
            a_buf[slot, :, c0:c0 + half_n] = _dot_nt(h_ref[...], u_ref[c0:c0 + half_n, :])
        return [(functools.partial(piece, c0), 1.0) for c0 in (0, half_n)]

    def finish_pieces(slot):
        quarter = acc.shape[1] // 4
        cell = {}

        def piece(c0):
            if "act" not in cell:
                cell["act"] = (_gelu_tanh(a_buf[slot]) * w_ref[...].astype(F32)).astype(BF16)
            acc[:, c0:c0 + quarter] += jnp.dot(cell["act"], v_ref[:, c0:c0 + quarter], preferred_element_type=F32)
        return [(functools.partial(piece, c0), 0.25 if c0 == 0 else 1.0) for c0 in range(0, acc.shape[1], quarter)]

    @pl.when(j == 0)
    def _():
        acc[...] = jnp.zeros_like(acc)
        interleave(score_pieces(0), retrieval_chains(0))

    for parity in range(2):
        @pl.when((j > 0) & (j < nj) & (j % 2 == parity))
        def _():
            interleave(score_pieces(parity) + finish_pieces(1 - parity), retrieval_chains(parity))

    @pl.when(j == nj)
    def _():
        finish((nj - 1) % 2)
        x2 = x_ref[...] + gate_ref[...] * acc[...]
        o_ref[...] = x2 * lax.rsqrt(jnp.mean(x2 * x2, axis=-1, keepdims=True) + EPS) * nw_ref[...]
        if retrieve_next:
            i1_ref[...] = i1_s[...].T
            i2_ref[...] = i2_s[...].T
            g_ref[...] = g_s[...].T


def _peer_dense(h2, u, v, w, x1, gate, norm_f_w, seq, tm, te, tile0, n_tiles, retrieval=None):
    n, d = h2.shape
    ne = u.shape[0]
    nj = ne // te
    assert nj % 2 == 0
    tiles_per_batch = seq // tm
    row = lambda i, j: (tile0 + i, 0)
    scored = lambda i, j: (jnp.minimum(j, nj - 1), 0)
    finished = lambda j: jnp.maximum(j - 1, 0)
    in_specs = [pl.BlockSpec((tm, d), row),
                pl.BlockSpec((te, d), scored),
                pl.BlockSpec((te, d), lambda i, j: (finished(j), 0)),
                pl.BlockSpec((tm, te), lambda i, j: (i, finished(j))),
                pl.BlockSpec((tm, d), row),
                pl.BlockSpec((None, 1, d), lambda i, j: ((tile0 + i) // tiles_per_batch, 0, 0)),
                pl.BlockSpec((1, d), lambda i, j: (0, 0))]
    args = [h2, u, v, w, x1, gate, norm_f_w.reshape(1, d)]
    out_specs = [pl.BlockSpec((tm, d), row)]
    out_shape = [jax.ShapeDtypeStruct((n, d), F32)]
    scratch = [pltpu.VMEM((tm, d), F32), pltpu.VMEM((2, tm, te), F32)]
    if retrieval is not None:
        q, keys = retrieval
        slots = PEER_HEADS * PEER_TOPK
        assert nj == keys.shape[0]
        half_head = lambda j: jnp.minimum(j, nj - 1)
        in_specs += [pl.BlockSpec((tm, PEER_D_HALF), lambda i, j: (tile0 + n_tiles + i, half_head(j))),
                     pl.BlockSpec((None, PEER_N_KEYS, PEER_D_HALF), lambda i, j: (half_head(j), 0, 0))]
        args += [q, keys]
        out_specs += [pl.BlockSpec((tm, slots), lambda i, j: (i, 0))] * 3
        out_shape += [jax.ShapeDtypeStruct((n_tiles * tm, slots), t) for t in (I32, I32, F32)]
        scratch += [pltpu.VMEM((PEER_TOPK, tm), F32), pltpu.VMEM((PEER_TOPK, tm), I32),
                    pltpu.VMEM((slots, tm), I32), pltpu.VMEM((slots, tm), I32), pltpu.VMEM((slots, tm), F32)]
    outs = pl.pallas_call(
        functools.partial(_peer_dense_kernel, retrieve_next=retrieval is not None),
        grid=(n_tiles, nj + 1),
        in_specs=in_specs,
        out_specs=out_specs,
        out_shape=out_shape,
        scratch_shapes=scratch,
        input_output_aliases={4: 0},
        compiler_params=_cparams("parallel", "arbitrary"),
        name="peer_dense",
    )(*args)
    return outs[0], tuple(outs[1:])


def _rope_tables(length):
    t = jnp.arange(length, dtype=jnp.int32)
    row = (t // GRID_W).astype(F32)
    col = (t % GRID_W).astype(F32)
    n_freq = HEAD_DIM // 4
    inv_freq = ROPE_BASE ** (-jnp.arange(n_freq, dtype=F32) / n_freq)
    ang = jnp.concatenate([row[:, None] * inv_freq, col[:, None] * inv_freq], axis=-1)
    cos, sin = jnp.cos(ang), jnp.sin(ang)
    return jnp.concatenate([cos, cos], axis=-1), jnp.concatenate([-sin, sin], axis=-1)


def _layer(x, ctx, c_rows, rope_c, rope_s, ada_w, ada_b, norm1_w, w_in, q_norm_w, k_norm_w, ret_decay_fwd,
           ret_decay_bwd, ret_norm_w, w_attn_branch, w_ret_branch, w_merge_out, norm2_w, peer_w_q, peer_keys,
           peer_u, peer_v, norm_f_w):
    b, l, d = x.shape
    lc = ctx.shape[1]
    mod = _ada(c_rows, ada_w, ada_b)
    mod_x = [mod[:b, i * d:(i + 1) * d].reshape(b, 1, d) for i in range(6)]
    mod_c = [mod[b:b + 1, i * d:(i + 1) * d].reshape(1, 1, d) for i in range(2)]

    w_in_b = _regroup_w_in(w_in).astype(BF16)
    n_in = w_in.shape[1]
    tn = n_in // 4
    p = _in_proj(x, norm1_w, mod_x[0], mod_x[1], w_in_b, tm=min(512, l), tn=tn)
    pc = _in_proj(ctx, norm1_w, mod_c[0], mod_c[1], w_in_b, tm=min(256, lc), tn=tn)

    attn = _attention(p, pc, rope_c, rope_s, q_norm_w, k_norm_w, tq=min(256, l))
    ret = _retention(p, pc, rope_c, rope_s, ret_decay_fwd, ret_decay_bwd, ret_norm_w)
    x1, h2 = _merge(attn, ret, p, x, mod_x[2], mod_x[3], mod_x[4], norm2_w, w_attn_branch.astype(BF16),
                    w_ret_branch.astype(BF16), w_merge_out.astype(BF16), tm=min(256, l))

    n = b * l
    h2f = h2.reshape(n, d)
    keys = peer_keys.reshape(PEER_HEADS * 2, PEER_N_KEYS, PEER_D_HALF).astype(BF16)
    q = _peer_q(h2f, peer_w_q.astype(BF16), tm=min(1024, l))
    u, v = peer_u.astype(BF16), peer_v.astype(BF16)
    tm = min(512, l)
    n_tiles = n // tm
    n_chunks = next(c for c in (8, 4, 2, 1) if n_tiles % c == 0)
    tiles = n_tiles // n_chunks
    sel = _peer_topk(q, keys, n=tiles * tm, tm=256)
    out = x1.reshape(n, d)
    for chunk in range(n_chunks):
        w = _peer_w(*sel, tb=128)
        retrieval = (q, keys) if chunk + 1 < n_chunks else None
        out, sel = _peer_dense(h2f, u, v, w, out, mod_x[5], norm_f_w, seq=l, tm=tm, te=1024,
                               tile0=chunk * tiles, n_tiles=tiles, retrieval=retrieval)
    return out.reshape(b, l, d)


def kernel(x, c, ctx, c_ctx, ada_w, ada_b, norm1_w, w_in, q_norm_w, k_norm_w, ret_decay_fwd, ret_decay_bwd,
           ret_norm_w, w_attn_branch, w_ret_branch, w_merge_out, norm2_w, peer_w_q, peer_keys, peer_u, peer_v,
           norm_f_w):
    depth = ada_w.shape[0]
    assert depth == 1, "context-stream update between layers is not implemented"
    b, l, d = x.shape
    rows = -(-(b + 1) // SUBLANES) * SUBLANES
    c_rows = jnp.zeros((rows, d), F32).at[:b].set(c).at[b].set(c_ctx)
    rope_c, rope_s = _rope_tables(l)
    return _layer(x, ctx, c_rows, rope_c, rope_s, ada_w[0], ada_b[0], norm1_w[0], w_in[0], q_norm_w[0], k_norm_w[0],
                  ret_decay_fwd[0], ret_decay_bwd[0], ret_norm_w[0], w_attn_branch[0], w_ret_branch[0],
                  w_merge_out[0], norm2_w[0], peer_w_q[0], peer_keys[0], peer_u[0], peer_v[0], norm_f_w)
```

```python
import functools

import jax
import jax.numpy as jnp
from jax import lax
from jax.experimental import pallas as pl
from jax.experimental.pallas import tpu as pltpu

F32 = jnp.float32
BF16 = jnp.bfloat16
I32 = jnp.int32

EPS = 1e-6
GRID_W = 64
ROPE_BASE = 10000.0
ATTN_HEADS = 8
ATTN_KV_HEADS = 2
HEAD_DIM = 128
RET_HEADS = 8
RET_DK = 128
RET_DV = 256
RET_CHUNK = 128
PEER_HEADS = 8
PEER_N_KEYS = 128
PEER_D_HALF = 128
PEER_TOPK = 16

LANES = 128
SUBLANES = 8
VMEM_LIMIT = 60 * 1024 * 1024

ATTN_Q_W = ATTN_HEADS * HEAD_DIM
ATTN_KV_W = ATTN_KV_HEADS * HEAD_DIM
RET_QK_W = RET_HEADS * RET_DK
RET_V_W = RET_HEADS * RET_DV
D_MODEL = 2048
_SRC_ORDER = (("qa", ATTN_Q_W), ("ka", ATTN_KV_W), ("va", ATTN_KV_W), ("qr", RET_QK_W), ("kr", RET_QK_W),
              ("vr", RET_V_W), ("gr", RET_V_W), ("ga", D_MODEL), ("gt", D_MODEL))
_DST_ORDER = ("ga", "gt", "vr", "gr", "qr", "kr", "qa", "ka", "va")


def _offsets(order, widths):
    off, o = {}, 0
    for name in order:
        off[name] = o
        o += widths[name]
    return off


_WIDTH = dict(_SRC_ORDER)
_SRC_OFF = _offsets([n for n, _ in _SRC_ORDER], _WIDTH)
OFF = _offsets(_DST_ORDER, _WIDTH)


def _regroup_w_in(w_in):
    return jnp.concatenate([w_in[:, _SRC_OFF[n]:_SRC_OFF[n] + _WIDTH[n]] for n in _DST_ORDER], axis=1)


def _cparams(*sem):
    return pltpu.CompilerParams(dimension_semantics=sem, vmem_limit_bytes=VMEM_LIMIT)


def _sigmoid(x):
    return 1.0 / (1.0 + jnp.exp(-x))


def _silu(x):
    return x * _sigmoid(x)


def _rot_half(x):
    return pltpu.roll(x, HEAD_DIM // 2, 1)


def _dot_nt(a, b):
    return lax.dot_general(a, b, (((1,), (1,)), ((), ())), preferred_element_type=F32)


def _dot_tn(a, b):
    return lax.dot_general(a, b, (((0,), (0,)), ((), ())), preferred_element_type=F32)


def _ada_kernel(c_ref, w_ref, b_ref, o_ref):
    sc = _silu(c_ref[...]).astype(BF16)
    o_ref[...] = jnp.dot(sc, w_ref[...].astype(BF16), preferred_element_type=F32) + b_ref[...]


def _ada(c_rows, ada_w, ada_b):
    rows, d = c_rows.shape
    n = ada_w.shape[1]
    tn = 1536
    return pl.pallas_call(
        _ada_kernel,
        grid=(n // tn,),
        in_specs=[pl.BlockSpec((rows, d), lambda j: (0, 0)),
                  pl.BlockSpec((d, tn), lambda j: (0, j)),
                  pl.BlockSpec((1, tn), lambda j: (0, j))],
        out_specs=pl.BlockSpec((rows, tn), lambda j: (0, j)),
        out_shape=jax.ShapeDtypeStruct((rows, n), F32),
        compiler_params=_cparams("arbitrary"),
        name="ada",
    )(c_rows, ada_w, ada_b.reshape(1, n))


def _in_proj_kernel(x_ref, xn_ref, nw_ref, shift_ref, scale_ref, shiftn_ref, scalen_ref, w_ref, o_ref, h_ref):
    t, j = pl.program_id(0), pl.program_id(1)
    nj = pl.num_programs(1)
    slot = t % 2

    def prepare(x, shift, scale, dst):
        y = x * lax.rsqrt(jnp.mean(x * x, axis=-1, keepdims=True) + EPS) * nw_ref[...]
        h_ref[dst] = (y * (1.0 + scale) + shift).astype(BF16)

    def project():
        o_ref[...] = jnp.dot(h_ref[slot], w_ref[...], preferred_element_type=F32).astype(o_ref.dtype)

    @pl.when((t == 0) & (j == 0))
    def _():
        prepare(x_ref[...], shift_ref[...], scale_ref[...], 0)

    @pl.when(j < nj - 1)
    def _():
        project()

    @pl.when(j == nj - 1)
    def _():
        prepare(xn_ref[...], shiftn_ref[...], scalen_ref[...], 1 - slot)
        project()


def _in_proj(x, norm_w, shift, scale, w, tm, tn):
    b, l, d = x.shape
    n = w.shape[1]
    per_batch = shift.shape[0] > 1
    tiles_per_batch = l // tm
    n_tiles = b * tiles_per_batch
    nxt = lambda t: jnp.minimum(t + 1, n_tiles - 1)
    row = lambda t, j: (0, 0)
    row_next = lambda t, j: (nxt(t), 0)
    mod = lambda t, j: (0, 0, 0)
    mod_next = (lambda t, j: (nxt(t) // tiles_per_batch, 0, 0)) if per_batch else (lambda t, j: (0, 0, 0))
    x2 = x.reshape(b * l, d)
    out = pl.pallas_call(
        _in_proj_kernel,
        grid=(n_tiles, n // tn),
        in_specs=[pl.BlockSpec((tm, d), row),
                  pl.BlockSpec((tm, d), row_next),
                  pl.BlockSpec((1, d), lambda t, j: (0, 0)),
                  pl.BlockSpec((None, 1, d), mod),
                  pl.BlockSpec((None, 1, d), mod),
                  pl.BlockSpec((None, 1, d), mod_next),
                  pl.BlockSpec((None, 1, d), mod_next),
                  pl.BlockSpec((d, tn), lambda t, j: (0, j))],
        out_specs=pl.BlockSpec((tm, tn), lambda t, j: (t, j)),
        out_shape=jax.ShapeDtypeStruct((b * l, n), BF16),
        scratch_shapes=[pltpu.VMEM((2, tm, d), BF16)],
        compiler_params=_cparams("arbitrary", "arbitrary"),
        name="in_proj",
    )(x2, x2, norm_w.reshape(1, d), shift, scale, shift, scale, w)
    return out.reshape(b, l, n)


def _head_rms(x, w):
    return x * lax.rsqrt(jnp.mean(x * x, axis=-1, keepdims=True) + EPS) * w


def _attn_kernel(q_ref, kc_ref, k_ref, vc_ref, v_ref, cq_ref, sq_ref, ck_ref, sk_ref, qnw_ref, knw_ref,
                 o_ref, k_s, v_s, *, lc, group, tq_sub):
    n_kv = k_s.shape[0]

    @pl.when(pl.program_id(1) == 0)
    def _():
        knw = knw_ref[...]
        for kv in range(n_kv):
            hs = slice(kv * HEAD_DIM, (kv + 1) * HEAD_DIM)
            k_s[kv, 0:lc, :] = _head_rms(kc_ref[:, hs].astype(F32), knw).astype(BF16)
            kn = _head_rms(k_ref[:, hs].astype(F32), knw)
            k_s[kv, lc:, :] = (kn * ck_ref[...] + _rot_half(kn) * sk_ref[...]).astype(BF16)
            v_s[kv, 0:lc, 0:HEAD_DIM] = vc_ref[:, hs]
            v_s[kv, lc:, 0:HEAD_DIM] = v_ref[:, hs]
            v_s[kv, :, HEAD_DIM:] = jnp.ones((v_s.shape[1], HEAD_DIM), BF16)

    c = (HEAD_DIM ** -0.5) * 1.4426950408889634
    qnw = qnw_ref[...]
    tq = tq_sub

    def scores(unit):
        r0, g = unit
        rows = slice(r0, r0 + tq)
        qn = _head_rms(q_ref[rows, g * HEAD_DIM:(g + 1) * HEAD_DIM].astype(F32), qnw)
        qr = (qn * (cq_ref[rows, :] * c) + _rot_half(qn) * (sq_ref[rows, :] * c)).astype(BF16)
        return _dot_nt(qr, k_s[g // group])

    def output(unit, s):
        r0, g = unit
        m = jnp.max(s, axis=-1, keepdims=True)
        p = jnp.exp2((s - m).astype(BF16))
        ol = jnp.dot(p, v_s[g // group], preferred_element_type=F32)
        o_ref[r0:r0 + tq, g * HEAD_DIM:(g + 1) * HEAD_DIM] = (
            ol[:, :HEAD_DIM] * (1.0 / ol[:, HEAD_DIM:HEAD_DIM + 1])).astype(o_ref.dtype)

    units = [(r0, g) for r0 in range(0, q_ref.shape[0], tq) for g in range(n_kv * group)]
    s_next = scores(units[0])
    for idx, unit in enumerate(units):
        s = s_next
        if idx + 1 < len(units):
            s_next = scores(units[idx + 1])
        output(unit, s)


def _attention(p, pc, rope_c, rope_s, q_norm_w, k_norm_w, tq):
    b, l, _ = p.shape
    lc = pc.shape[1]
    group = ATTN_HEADS // ATTN_KV_HEADS
    kq0 = OFF["qa"] // ATTN_Q_W
    kk0 = OFF["ka"] // ATTN_KV_W
    kv0 = OFF["va"] // ATTN_KV_W
    full = lambda bi, i: (0, 0)
    return pl.pallas_call(
        functools.partial(_attn_kernel, lc=lc, group=group, tq_sub=min(256, tq)),
        grid=(b, l // tq),
        in_specs=[pl.BlockSpec((None, tq, ATTN_Q_W), lambda bi, i: (bi, i, kq0)),
                  pl.BlockSpec((None, lc, ATTN_KV_W), lambda bi, i: (bi, 0, kk0)),
                  pl.BlockSpec((None, l, ATTN_KV_W), lambda bi, i: (bi, 0, kk0)),
                  pl.BlockSpec((None, lc, ATTN_KV_W), lambda bi, i: (bi, 0, kv0)),
                  pl.BlockSpec((None, l, ATTN_KV_W), lambda bi, i: (bi, 0, kv0)),
                  pl.BlockSpec((tq, HEAD_DIM), lambda bi, i: (i, 0)),
                  pl.BlockSpec((tq, HEAD_DIM), lambda bi, i: (i, 0)),
                  pl.BlockSpec((l, HEAD_DIM), full),
                  pl.BlockSpec((l, HEAD_DIM), full),
                  pl.BlockSpec((1, HEAD_DIM), full),
                  pl.BlockSpec((1, HEAD_DIM), full)],
        out_specs=pl.BlockSpec((None, tq, ATTN_Q_W), lambda bi, i: (bi, i, 0)),
        out_shape=jax.ShapeDtypeStruct((b, l, ATTN_Q_W), BF16),
        scratch_shapes=[pltpu.VMEM((ATTN_KV_HEADS, lc + l, HEAD_DIM), BF16),
                        pltpu.VMEM((ATTN_KV_HEADS, lc + l, 2 * HEAD_DIM), BF16)],
        compiler_params=_cparams("parallel", "arbitrary"),
        name="attention",
    )(p, pc, p, pc, p, rope_c, rope_s, rope_c, rope_s, q_norm_w.reshape(1, HEAD_DIM), k_norm_w.reshape(1, HEAD_DIM))


def _ret_kernel(q_ref, k_ref, v_ref, g_ref, kc_ref, vc_ref, cos_ref, sin_ref, z_ref, nw_ref, o_ref,
                q_s, k_s, acc, st_f, st_b, *, nc, lc):
    ch = RET_CHUNK
    assert nc % 2 == 0
    scale = RET_DK ** -0.5
    lg_f = -jnp.exp(z_ref[0])
    lg_b = -jnp.exp(z_ref[1])
    row = lax.broadcasted_iota(I32, (ch, ch), 0).astype(F32)
    col = lax.broadcasted_iota(I32, (ch, ch), 1).astype(F32)
    d = row - col
    intra_f = jnp.where(d >= 0, jnp.exp(lg_f * jnp.maximum(d, 0.0)), 0.0)
    intra_b = jnp.where(d <= 0, jnp.exp(lg_b * jnp.maximum(-d, 0.0)), 0.0)

    def wide(a):
        return jnp.concatenate([a] * (RET_DV // LANES), axis=1)

    qd_f = wide(jnp.exp(lg_f * (row + 1.0)))
    kd_f = jnp.exp(lg_f * (ch - 1.0 - row))
    cd_f = wide(jnp.exp(lg_f * float(ch)))
    qd_b = wide(jnp.exp(lg_b * (ch - row)))
    kd_b = jnp.exp(lg_b * row)
    cd_b = wide(jnp.exp(lg_b * float(ch)))

    j = lax.broadcasted_iota(I32, (lc, RET_DK), 0).astype(F32)
    kc = kc_ref[...].astype(F32) * scale
    vc = vc_ref[...]
    st_f[...] = _dot_tn((kc * jnp.exp(lg_f * (lc - 1.0 - j))).astype(BF16), vc)
    st_b[...] = _dot_tn((kc * jnp.exp(lg_b * j)).astype(BF16), vc)

    nw = nw_ref[...]
    fwd = (st_f, intra_f, qd_f, kd_f, cd_f)
    bwd = (st_b, intra_b, qd_b, kd_b, cd_b)

    span = 4 if (nc // 2) % 4 == 0 else (2 if (nc // 2) % 2 == 0 else 1)

    def block(b0, first_visit):
        visits = []
        for u in range(span):
            i = b0 * span + u
            visits.append((pl.multiple_of(i * ch, ch), fwd))
            visits.append((pl.multiple_of((nc - 1 - i) * ch, ch), bwd))
        loaded = []
        for r0, _ in visits:
            rows = pl.ds(r0, ch)
            if first_visit:
                cs, sn = cos_ref[rows, :], sin_ref[rows, :]
                qf, kf = q_ref[rows, :].astype(F32), k_ref[rows, :].astype(F32)
                qb = (qf * cs + _rot_half(qf) * sn).astype(BF16)
                kr = (kf * cs + _rot_half(kf) * sn) * scale
                extra = None
            else:
                qb, kr = q_s[rows, :], k_s[rows, :]
                extra = (acc[rows, :], g_ref[rows, :])
            loaded.append((qb, kr, v_ref[rows, :], extra))
        raw = [_dot_nt(qb, kr.astype(BF16)) for qb, kr, _, _ in loaded]
        upd = [_dot_tn((kr * direction[3]).astype(BF16), vb)
               for (_, direction), (_, kr, vb, _) in zip(visits, loaded)]
        state = {id(fwd): st_f[...], id(bwd): st_b[...]}
        before = []
        for (_, direction), u_ in zip(visits, upd):
            before.append(state[id(direction)])
            state[id(direction)] = state[id(direction)] * direction[4] + u_
        cross = [jnp.dot(qb, s0.astype(BF16), preferred_element_type=F32)
                 for (qb, _, _, _), s0 in zip(loaded, before)]
        outs = []
        for (r0, direction), (qb, kr, vb, extra), s_raw, cr in zip(visits, loaded, raw, cross):
            _, intra, qd, kd, cd = direction
            o = jnp.dot((s_raw * intra).astype(BF16), vb, preferred_element_type=F32) + cr * qd
            if not first_visit:
                o = o + extra[0]
                mu = jnp.mean(o, axis=-1, keepdims=True)
                oc = o - mu
                var = jnp.mean(oc * oc, axis=-1, keepdims=True)
                o = (_silu(extra[1].astype(F32)) * (oc * lax.rsqrt(var + EPS) * nw)).astype(o_ref.dtype)
            outs.append(o)
        for (r0, _), (qb, kr, _, _), o in zip(visits, loaded, outs):
            rows = pl.ds(r0, ch)
            if first_visit:
                q_s[rows, :] = qb
                k_s[rows, :] = kr
                acc[rows, :] = o
            else:
                o_ref[rows, :] = o
        st_f[...] = state[id(fwd)]
        st_b[...] = state[id(bwd)]

    n_blocks = (nc // 2) // span

    def first_half(b0, carry):
        block(b0, True)
        return carry

    def second_half(b0, carry):
        block(b0, False)
        return carry

    lax.fori_loop(0, n_blocks, first_half, 0)
    lax.fori_loop(n_blocks, 2 * n_blocks, second_half, 0)


def _retention(p, pc, rope_c, rope_s, decay_fwd, decay_bwd, ret_norm_w):
    b, l, _ = p.shape
    lc = pc.shape[1]
    nc = l // RET_CHUNK
    z = jnp.broadcast_to(jnp.stack([decay_fwd, decay_bwd])[:, :, None, None], (2, RET_HEADS, 1, LANES))
    kq0, kk0 = OFF["qr"] // RET_DK, OFF["kr"] // RET_DK
    kv0, kg0 = OFF["vr"] // RET_DV, OFF["gr"] // RET_DV
    full = lambda bi, h: (0, 0)
    return pl.pallas_call(
        functools.partial(_ret_kernel, nc=nc, lc=lc),
        grid=(b, RET_HEADS),
        in_specs=[pl.BlockSpec((None, l, RET_DK), lambda bi, h: (bi, 0, kq0 + h)),
                  pl.BlockSpec((None, l, RET_DK), lambda bi, h: (bi, 0, kk0 + h)),
                  pl.BlockSpec((None, l, RET_DV), lambda bi, h: (bi, 0, kv0 + h)),
                  pl.BlockSpec((None, l, RET_DV), lambda bi, h: (bi, 0, kg0 + h)),
                  pl.BlockSpec((None, lc, RET_DK), lambda bi, h: (bi, 0, kk0 + h)),
                  pl.BlockSpec((None, lc, RET_DV), lambda bi, h: (bi, 0, kv0 + h)),
                  pl.BlockSpec((l, RET_DK), full),
                  pl.BlockSpec((l, RET_DK), full),
                  pl.BlockSpec((2, None, 1, LANES), lambda bi, h: (0, h, 0, 0)),
                  pl.BlockSpec((1, RET_DV), lambda bi, h: (0, h))],
        out_specs=pl.BlockSpec((None, l, RET_DV), lambda bi, h: (bi, 0, h)),
        out_shape=jax.ShapeDtypeStruct((b, l, RET_V_W), BF16),
        scratch_shapes=[pltpu.VMEM((l, RET_DK), BF16), pltpu.VMEM((l, RET_DK), F32), pltpu.VMEM((l, RET_DV), F32),
                        pltpu.VMEM((RET_DK, RET_DV), F32), pltpu.VMEM((RET_DK, RET_DV), F32)],
        compiler_params=_cparams("parallel", "parallel"),
        name="retention",
    )(p, p, p, p, pc, pc, rope_c, rope_s, z, ret_norm_w.reshape(1, RET_V_W))


def _merge_kernel(attn_ref, ret_ref, ga_ref, gr_ref, x_ref, gate_ref, shift_ref, scale_ref, nw_ref,
                  wa_ref, wr_ref, wm_ref, x1_ref, h2_ref):
    a = jnp.dot(attn_ref[...], wa_ref[...], preferred_element_type=F32)
    r = jnp.dot(ret_ref[...], wr_ref[...], preferred_element_type=F32)
    mixed = _sigmoid(ga_ref[...].astype(F32)) * a + _sigmoid(gr_ref[...].astype(F32)) * r
    y = jnp.dot(mixed.astype(BF16), wm_ref[...], preferred_element_type=F32)
    x1 = x_ref[...] + gate_ref[...] * y
    x1_ref[...] = x1
    hn = x1 * lax.rsqrt(jnp.mean(x1 * x1, axis=-1, keepdims=True) + EPS) * nw_ref[...]
    h2_ref[...] = (hn * (1.0 + scale_ref[...]) + shift_ref[...]).astype(BF16)


def _merge(attn, ret, p, x, gate, shift2, scale2, norm2_w, wa, wr, wm, tm):
    b, l, d = x.shape
    assert d == D_MODEL
    kga, kgt = OFF["ga"] // d, OFF["gt"] // d
    row = lambda bi, i: (bi, i, 0)
    mod = lambda bi, i: (bi, 0, 0)
    const = lambda bi, i: (0, 0)
    once = pl.Buffered(1)
    return pl.pallas_call(
        _merge_kernel,
        grid=(b, l // tm),
        in_specs=[pl.BlockSpec((None, tm, ATTN_Q_W), row),
                  pl.BlockSpec((None, tm, RET_V_W), row),
                  pl.BlockSpec((None, tm, d), lambda bi, i: (bi, i, kga)),
                  pl.BlockSpec((None, tm, d), lambda bi, i: (bi, i, kgt)),
                  pl.BlockSpec((None, tm, d), row),
                  pl.BlockSpec((None, 1, d), mod),
                  pl.BlockSpec((None, 1, d), mod),
                  pl.BlockSpec((None, 1, d), mod),
                  pl.BlockSpec((1, d), const),
                  pl.BlockSpec(wa.shape, const, pipeline_mode=once),
                  pl.BlockSpec(wr.shape, const, pipeline_mode=once),
                  pl.BlockSpec(wm.shape, const, pipeline_mode=once)],
        out_specs=[pl.BlockSpec((None, tm, d), row), pl.BlockSpec((None, tm, d), row)],
        out_shape=[jax.ShapeDtypeStruct((b, l, d), F32), jax.ShapeDtypeStruct((b, l, d), BF16)],
        compiler_params=_cparams("parallel", "parallel"),
        name="merge",
    )(attn, ret, p, p, x, gate, shift2, scale2, norm2_w.reshape(1, d), wa, wr, wm)


def _run(gen):
    try:
        while True:
            next(gen)
    except StopIteration as stop:
        return stop.value


def _top_rows_gen(s, k):
    n = s.shape[0]
    rid = lax.broadcasted_iota(I32, s.shape, 0)
    vals, ids = [], []
    for _ in range(k):
        m = jnp.max(s, axis=0, keepdims=True)
        sel = jnp.min(jnp.where(s == m, rid, n), axis=0, keepdims=True)
        vals.append(m)
        ids.append(sel)
        s = jnp.where(rid == sel, -jnp.inf, s)
        yield
    return jnp.concatenate(vals, axis=0), jnp.concatenate(ids, axis=0)


def _top_rows(s, k):
    return _run(_top_rows_gen(s, k))


def _pair_candidates(s1, s2):
    k = PEER_TOPK
    t = s1.shape[1]
    r8 = lax.broadcasted_iota(I32, (SUBLANES, t), 0)
    r16 = lax.broadcasted_iota(I32, (k, t), 0)
    neg = -jnp.inf
    sums = [s1[0:1] + s2]
    flat = [r16]
    for a, nb in ((1, 8), (2, 5), (3, 4)):
        sums.append(jnp.where(r8 < nb, s1[a:a + 1] + s2[0:8], neg))
        flat.append(a * k + r8)
    sums.append(s1[8:16] + s2[0:1])
    flat.append((r8 + 8) * k)
    for b_, lo, hi in ((0, 4, 8), (1, 4, 8), (2, 4, 5)):
        sums.append(jnp.where((r8 >= lo) & (r8 < hi), s1[0:8] + s2[b_:b_ + 1], neg))
        flat.append(r8 * k + b_)
    return jnp.concatenate(sums, axis=0), jnp.concatenate(flat, axis=0)


def _pair_topk_gen(s1, n1, s2, n2):
    k = PEER_TOPK
    cand, flat = _pair_candidates(s1, s2)
    tops, picks = [], []
    for _ in range(k):
        m = jnp.max(cand, axis=0, keepdims=True)
        f = jnp.min(jnp.where(cand == m, flat, k * k), axis=0, keepdims=True)
        cand = jnp.where(flat == f, -jnp.inf, cand)
        tops.append(m)
        picks.append(f)
        yield
    top = jnp.concatenate(tops, axis=0)
    pick = jnp.concatenate(picks, axis=0)
    pa = pick // k
    pb = pick - pa * k
    e1 = jnp.zeros_like(pick)
    e2 = jnp.zeros_like(pick)
    for a in range(k):
        e1 = jnp.where(pa == a, n1[a:a + 1], e1)
        e2 = jnp.where(pb == a, n2[a:a + 1], e2)
    ex = jnp.exp(top - top[0:1])
    return e1, e2, ex / jnp.sum(ex, axis=0, keepdims=True)


def _pair_topk(s1, n1, s2, n2):
    return _run(_pair_topk_gen(s1, n1, s2, n2))


def _peer_q_kernel(h_ref, wq_ref, q_ref):
    q_ref[...] = jnp.dot(h_ref[...], wq_ref[...], preferred_element_type=F32).astype(q_ref.dtype)


def _peer_q(h2, wq, tm):
    n, d = h2.shape
    return pl.pallas_call(
        _peer_q_kernel,
        grid=(n // tm,),
        in_specs=[pl.BlockSpec((tm, d), lambda i: (i, 0)),
                  pl.BlockSpec(wq.shape, lambda i: (0, 0), pipeline_mode=pl.Buffered(1))],
        out_specs=pl.BlockSpec((tm, wq.shape[1]), lambda i: (i, 0)),
        out_shape=jax.ShapeDtypeStruct((n, wq.shape[1]), BF16),
        compiler_params=_cparams("parallel"),
        name="peer_q",
    )(h2, wq)


def _peer_topk_kernel(q_s, keys_ref, i1_ref, i2_ref, g_ref, i1_s, i2_s, g_s):
    k = PEER_TOPK

    def head(h, carry):
        c0 = pl.multiple_of(h * 2 * PEER_D_HALF, 2 * PEER_D_HALF)
        qh = q_s[:, pl.ds(c0, 2 * PEER_D_HALF)]
        s1, n1 = _top_rows(_dot_nt(keys_ref[2 * h], qh[:, :PEER_D_HALF]), k)
        s2, n2 = _top_rows(_dot_nt(keys_ref[2 * h + 1], qh[:, PEER_D_HALF:]), k)
        e1, e2, gates = _pair_topk(s1, n1, s2, n2)
        r0 = pl.multiple_of(h * k, k)
        g_s[pl.ds(r0, k), :] = gates
        i1_s[pl.ds(r0, k), :] = e1
        i2_s[pl.ds(r0, k), :] = e2
        return carry

    lax.fori_loop(0, PEER_HEADS, head, 0)
    i1_ref[...] = i1_s[...].T
    i2_ref[...] = i2_s[...].T
    g_ref[...] = g_s[...].T


def _peer_topk(q, keys, n, tm):
    slots = PEER_HEADS * PEER_TOPK
    row = lambda i: (i, 0)
    out = jax.ShapeDtypeStruct
    return pl.pallas_call(
        _peer_topk_kernel,
        grid=(n // tm,),
        in_specs=[pl.BlockSpec((tm, q.shape[1]), row),
                  pl.BlockSpec(keys.shape, lambda i: (0, 0, 0))],
        out_specs=[pl.BlockSpec((tm, slots), row)] * 3,
        out_shape=[out((n, slots), I32), out((n, slots), I32), out((n, slots), F32)],
        scratch_shapes=[pltpu.VMEM((slots, tm), I32), pltpu.VMEM((slots, tm), I32), pltpu.VMEM((slots, tm), F32)],
        compiler_params=_cparams("parallel"),
        name="peer_topk",
    )(q, keys)


def _peer_w_kernel(i1_ref, i2_ref, g_ref, w_ref, w3):
    nk = PEER_N_KEYS
    tb = i1_ref.shape[0]
    grp = 2 * SUBLANES
    rid = lax.broadcasted_iota(I32, (nk, i1_ref.shape[1]), 0)
    sub = lax.broadcasted_iota(I32, (SUBLANES, nk), 0)

    def rows_to_tokens(v):
        v = list(v)
        for dist in (4, 2, 1):
            keep = (sub & dist) == 0
            for k in range(SUBLANES):
                if k & dist == 0:
                    a, b_ = v[k], v[k + dist]
                    v[k] = jnp.where(keep, a, pltpu.roll(b_, dist, 0))
                    v[k + dist] = jnp.where(keep, pltpu.roll(a, SUBLANES - dist, 0), b_)
        return v

    def group(gi, carry):
        t0 = pl.multiple_of(gi * grp, grp)
        i1 = i1_ref[pl.ds(t0, grp), :]
        i2 = i2_ref[pl.ds(t0, grp), :]
        g = g_ref[pl.ds(t0, grp), :]
        for k in range(grp):
            left = jnp.where(rid == i1[k:k + 1], g[k:k + 1], 0.0).astype(BF16)
            right = jnp.where(rid == i2[k:k + 1], 1.0, 0.0).astype(BF16)
            w3[k * nk:(k + 1) * nk, :] = _dot_nt(left, right)
        for j in range(nk // SUBLANES):
            halves = []
            for half in range(grp // SUBLANES):
                tiles = [w3[(half * SUBLANES + k) * nk + j * SUBLANES:(half * SUBLANES + k) * nk + (j + 1) * SUBLANES, :]
                         for k in range(SUBLANES)]
                halves.append(rows_to_tokens(tiles))
            for i in range(SUBLANES):
                r = j * SUBLANES + i
                tile = jnp.concatenate([h[i] for h in halves], axis=0)
                w_ref[pl.ds(t0, grp), r * nk:(r + 1) * nk] = tile.astype(w_ref.dtype)
        return carry

    lax.fori_loop(0, tb // grp, group, 0, unroll=4)


def _peer_w(i1, i2, g, tb):
    n, slots = i1.shape
    ne = PEER_N_KEYS * PEER_N_KEYS
    row = lambda i: (i, 0)
    return pl.pallas_call(
        _peer_w_kernel,
        grid=(n // tb,),
        in_specs=[pl.BlockSpec((tb, slots), row)] * 3,
        out_specs=pl.BlockSpec((tb, ne), row),
        out_shape=jax.ShapeDtypeStruct((n, ne), BF16),
        scratch_shapes=[pltpu.VMEM((2 * SUBLANES * PEER_N_KEYS, PEER_N_KEYS), F32)],
        compiler_params=_cparams("parallel"),
        name="peer_w",
    )(i1, i2, g)


def _gelu_tanh(x):
    return 0.5 * x * (1.0 + jnp.tanh(0.7978845608028654 * (x + 0.044715 * (x * x * x))))


def _peer_dense_kernel(*refs, retrieve_next):
    if retrieve_next:
        (h_ref, u_ref, v_ref, w_ref, x_ref, gate_ref, nw_ref, qn_ref, keys_ref,
         o_ref, i1_ref, i2_ref, g_ref, acc, a_buf, s1_s, n1_s, i1_s, i2_s, g_s) = refs
    else:
        h_ref, u_ref, v_ref, w_ref, x_ref, gate_ref, nw_ref, o_ref, acc, a_buf = refs
    j = pl.program_id(1)
    nj = pl.num_programs(1) - 1
    k = PEER_TOPK

    def score(slot):
        a_buf[slot] = jnp.dot(h_ref[...], u_ref[...], preferred_element_type=F32)

    def finish(slot):
        act = (_gelu_tanh(a_buf[slot]) * w_ref[...].astype(F32)).astype(BF16)
        acc[...] += jnp.dot(act, v_ref[...], preferred_element_type=F32)

    def retrieval_chains(half):
        if not retrieve_next:
            return [], 0
        r0 = pl.multiple_of((j // 2) * k, k)
        groups = range(0, qn_ref.shape[0], LANES)

        def chain(t0):
            tl = slice(t0, t0 + LANES)
            s, n = yield from _top_rows_gen(_dot_nt(keys_ref[...], qn_ref[tl, :]), k)
            if half == 0:
                s1_s[:, tl] = s
                n1_s[:, tl] = n
            else:
                e1, e2, gates = yield from _pair_topk_gen(s1_s[:, tl], n1_s[:, tl], s, n)
                i1_s[pl.ds(r0, k), tl] = e1
                i2_s[pl.ds(r0, k), tl] = e2
                g_s[pl.ds(r0, k), tl] = gates

        rounds = (k if half == 0 else 2 * k) + 1
        return [chain(t0) for t0 in groups], rounds

    def interleave(main, filler):
        chains, _ = filler
        roomy = [idx for idx, (_, weight) in enumerate(main) if weight >= 1.0]
        start = {}
        for ci, chain in enumerate(chains):
            start.setdefault(roomy[ci * len(roomy) // len(chains)], []).append(chain)
        for idx, (piece, _) in enumerate(main):
            for chain in start.get(idx, ()):
                _run(chain)
            piece()

    def score_pieces(slot):
        half_n = a_buf.shape[2] // 2

        def piece(c0):
            a_buf[slot, :, c0:c0 + half_n] = _dot_nt(h_ref[...], u_ref[c0:c0 + half_n, :])
        return [(functools.partial(piece, c0), 1.0) for c0 in (0, half_n)]

    def finish_pieces(slot):
        quarter = acc.shape[1] // 4
        cell = {}

        def piece(c0):
            if "act" not in cell:
                cell["act"] = (_gelu_tanh(a_buf[slot]) * w_ref[...].astype(F32)).astype(BF16)
            acc[:, c0:c0 + quarter] += jnp.dot(cell["act"], v_ref[:, c0:c0 + quarter], preferred_element_type=F32)
        return [(functools.partial(piece, c0), 0.25 if c0 == 0 else 1.0) for c0 in range(0, acc.shape[1], quarter)]

    @pl.when(j == 0)
    def _():
        acc[...] = jnp.zeros_like(acc)
        interleave(score_pieces(0), retrieval_chains(0))

    for parity in range(2):
        @pl.when((j > 0) & (j < nj) & (j % 2 == parity))
        def _():
            interleave(score_pieces(parity) + finish_pieces(1 - parity), retrieval_chains(parity))

    @pl.when(j == nj)
    def _():
        finish((nj - 1) % 2)
        x2 = x_ref[...] + gate_ref[...] * acc[...]
        o_ref[...] = x2 * lax.rsqrt(jnp.mean(x2 * x2, axis=-1, keepdims=True) + EPS) * nw_ref[...]
        if retrieve_next:
            i1_ref[...] = i1_s[...].T
            i2_ref[...] = i2_s[...].T
            g_ref[...] = g_s[...].T


def _peer_dense(h2, u, v, w, x1, gate, norm_f_w, seq, tm, te, tile0, n_tiles, retrieval=None):
    n, d = h2.shape
    ne = u.shape[0]
    nj = ne // te
    assert nj % 2 == 0
    tiles_per_batch = seq // tm
    row = lambda i, j: (tile0 + i, 0)
    scored = lambda i, j: (jnp.minimum(j, nj - 1), 0)
    finished = lambda j: jnp.maximum(j - 1, 0)
    in_specs = [pl.BlockSpec((tm, d), row),
                pl.BlockSpec((te, d), scored),
                pl.BlockSpec((te, d), lambda i, j: (finished(j), 0)),
                pl.BlockSpec((tm, te), lambda i, j: (i, finished(j))),
                pl.BlockSpec((tm, d), row),
                pl.BlockSpec((None, 1, d), lambda i, j: ((tile0 + i) // tiles_per_batch, 0, 0)),
                pl.BlockSpec((1, d), lambda i, j: (0, 0))]
    args = [h2, u, v, w, x1, gate, norm_f_w.reshape(1, d)]
    out_specs = [pl.BlockSpec((tm, d), row)]
    out_shape = [jax.ShapeDtypeStruct((n, d), F32)]
    scratch = [pltpu.VMEM((tm, d), F32), pltpu.VMEM((2, tm, te), F32)]
    if retrieval is not None:
        q, keys = retrieval
        slots = PEER_HEADS * PEER_TOPK
        assert nj == keys.shape[0]
        half_head = lambda j: jnp.minimum(j, nj - 1)
        in_specs += [pl.BlockSpec((tm, PEER_D_HALF), lambda i, j: (tile0 + n_tiles + i, half_head(j))),
                     pl.BlockSpec((None, PEER_N_KEYS, PEER_D_HALF), lambda i, j: (half_head(j), 0, 0))]
        args += [q, keys]
        out_specs += [pl.BlockSpec((tm, slots), lambda i, j: (i, 0))] * 3
        out_shape += [jax.ShapeDtypeStruct((n_tiles * tm, slots), t) for t in (I32, I32, F32)]
        scratch += [pltpu.VMEM((PEER_TOPK, tm), F32), pltpu.VMEM((PEER_TOPK, tm), I32),
                    pltpu.VMEM((slots, tm), I32), pltpu.VMEM((slots, tm), I32), pltpu.VMEM((slots, tm), F32)]
    outs = pl.pallas_call(
        functools.partial(_peer_dense_kernel, retrieve_next=retrieval is not None),
        grid=(n_tiles, nj + 1),
        in_specs=in_specs,
        out_specs=out_specs,
        out_shape=out_shape,
        scratch_shapes=scratch,
        input_output_aliases={4: 0},
        compiler_params=_cparams("parallel", "arbitrary"),
        name="peer_dense",
    )(*args)
    return outs[0], tuple(outs[1:])


def _rope_tables(length):
    t = jnp.arange(length, dtype=jnp.int32)
    row = (t // GRID_W).astype(F32)
    col = (t % GRID_W).astype(F32)
    n_freq = HEAD_DIM // 4
    inv_freq = ROPE_BASE ** (-jnp.arange(n_freq, dtype=F32) / n_freq)
    ang = jnp.concatenate([row[:, None] * inv_freq, col[:, None] * inv_freq], axis=-1)
    cos, sin = jnp.cos(ang), jnp.sin(ang)
    return jnp.concatenate([cos, cos], axis=-1), jnp.concatenate([-sin, sin], axis=-1)


def _layer(x, ctx, c_rows, rope_c, rope_s, ada_w, ada_b, norm1_w, w_in, q_norm_w, k_norm_w, ret_decay_fwd,
           ret_decay_bwd, ret_norm_w, w_attn_branch, w_ret_branch, w_merge_out, norm2_w, peer_w_q, peer_keys,
           peer_u, peer_v, norm_f_w):
    b, l, d = x.shape
    lc = ctx.shape[1]
    mod = _ada(c_rows, ada_w, ada_b)
    mod_x = [mod[:b, i * d:(i + 1) * d].reshape(b, 1, d) for i in range(6)]
    mod_c = [mod[b:b + 1, i * d:(i + 1) * d].reshape(1, 1, d) for i in range(2)]

    w_in_b = _regroup_w_in(w_in).astype(BF16)
    n_in = w_in.shape[1]
    tn = n_in // 4
    p = _in_proj(x, norm1_w, mod_x[0], mod_x[1], w_in_b, tm=min(512, l), tn=tn)
    pc = _in_proj(ctx, norm1_w, mod_c[0], mod_c[1], w_in_b, tm=min(256, lc), tn=tn)

    attn = _attention(p, pc, rope_c, rope_s, q_norm_w, k_norm_w, tq=min(512, l))
    ret = _retention(p, pc, rope_c, rope_s, ret_decay_fwd, ret_decay_bwd, ret_norm_w)
    x1, h2 = _merge(attn, ret, p, x, mod_x[2], mod_x[3], mod_x[4], norm2_w, w_attn_branch.astype(BF16),
                    w_ret_branch.astype(BF16), w_merge_out.astype(BF16), tm=min(256, l))

    n = b * l
    h2f = h2.reshape(n, d)
    keys = peer_keys.reshape(PEER_HEADS * 2, PEER_N_KEYS, PEER_D_HALF).astype(BF16)
    q = _peer_q(h2f, peer_w_q.astype(BF16), tm=min(1024, l))
    u, v = peer_u.astype(BF16), peer_v.astype(BF16)
    tm = min(512, l)
    n_tiles = n // tm
    n_chunks = next(c for c in (8, 4, 2, 1) if n_tiles % c == 0)
    tiles = n_tiles // n_chunks
    sel = _peer_topk(q, keys, n=tiles * tm, tm=256)
    out = x1.reshape(n, d)
    for chunk in range(n_chunks):
        w = _peer_w(*sel, tb=128)
        retrieval = (q, keys) if chunk + 1 < n_chunks else None
        out, sel = _peer_dense(h2f, u, v, w, out, mod_x[5], norm_f_w, seq=l, tm=tm, te=1024,
                               tile0=chunk * tiles, n_tiles=tiles, retrieval=retrieval)
    return out.reshape(b, l, d)


def kernel(x, c, ctx, c_ctx, ada_w, ada_b, norm1_w, w_in, q_norm_w, k_norm_w, ret_decay_fwd, ret_decay_bwd,
           ret_norm_w, w_attn_branch, w_ret_branch, w_merge_out, norm2_w, peer_w_q, peer_keys, peer_u, peer_v,
           norm_f_w):
    depth = ada_w.shape[0]
    assert depth == 1, "context-stream update between layers is not implemented"
    b, l, d = x.shape
    rows = -(-(b + 1) // SUBLANES) * SUBLANES
    c_rows = jnp.zeros((rows, d), F32).at[:b].set(c).at[b].set(c_ctx)
    rope_c, rope_s = _rope_tables(l)
    return _layer(x, ctx, c_rows, rope_c, rope_s, ada_w[0], ada_b[0], norm1_w[0], w_in[0], q_norm_w[0], k_norm_w[0],
                  ret_decay_fwd[0], ret_decay_bwd[0], ret_norm_w[0], w_attn_branch[0], w_ret_branch[0],
                  w_merge_out[0], norm2_w[0], peer_w_q[0], peer_keys[0], peer_u[0], peer_v[0], norm_f_w)
```

```python
import functools

import jax
import jax.numpy as jnp
from jax import lax
from jax.experimental import pallas as pl
from jax.experimental.pallas import tpu as pltpu

F32 = jnp.float32
BF16 = jnp.bfloat16
I32 = jnp.int32

EPS = 1e-6
GRID_W = 64
ROPE_BASE = 10000.0
ATTN_HEADS = 8
ATTN_KV_HEADS = 2
HEAD_DIM = 128
RET_HEADS = 8
RET_DK = 128
RET_DV = 256
RET_CHUNK = 128
PEER_HEADS = 8
PEER_N_KEYS = 128
PEER_D_HALF = 128
PEER_TOPK = 16

LANES = 128
SUBLANES = 8
VMEM_LIMIT = 60 * 1024 * 1024

ATTN_Q_W = ATTN_HEADS * HEAD_DIM
ATTN_KV_W = ATTN_KV_HEADS * HEAD_DIM
RET_QK_W = RET_HEADS * RET_DK
RET_V_W = RET_HEADS * RET_DV
D_MODEL = 2048
_IN_SPLITS = (("qa", ATTN_Q_W), ("ka", ATTN_KV_W), ("va", ATTN_KV_W), ("qr", RET_QK_W), ("kr", RET_QK_W),
              ("vr", RET_V_W), ("gr", RET_V_W), ("ga", D_MODEL), ("gt", D_MODEL))
OFF = {}
for _name, _width in _IN_SPLITS:
    OFF[_name] = sum(w for n, w in _IN_SPLITS[:len(OFF)])
GATE_BLOCK = 512


def _cparams(*sem):
    return pltpu.CompilerParams(dimension_semantics=sem, vmem_limit_bytes=VMEM_LIMIT)


def _sigmoid(x):
    return 1.0 / (1.0 + jnp.exp(-x))


def _silu(x):
    return x * _sigmoid(x)


def _rot_half(x):
    return pltpu.roll(x, HEAD_DIM // 2, 1)


def _dot_nt(a, b):
    return lax.dot_general(a, b, (((1,), (1,)), ((), ())), preferred_element_type=F32)


def _dot_tn(a, b):
    return lax.dot_general(a, b, (((0,), (0,)), ((), ())), preferred_element_type=F32)


def _ada_kernel(c_ref, w_ref, b_ref, o_ref):
    sc = _silu(c_ref[...]).astype(BF16)
    o_ref[...] = jnp.dot(sc, w_ref[...].astype(BF16), preferred_element_type=F32) + b_ref[...]


def _ada(c_rows, ada_w, ada_b):
    rows, d = c_rows.shape
    n = ada_w.shape[1]
    tn = 1536
    return pl.pallas_call(
        _ada_kernel,
        grid=(n // tn,),
        in_specs=[pl.BlockSpec((rows, d), lambda j: (0, 0)),
                  pl.BlockSpec((d, tn), lambda j: (0, j)),
                  pl.BlockSpec((1, tn), lambda j: (0, j))],
        out_specs=pl.BlockSpec((rows, tn), lambda j: (0, j)),
        out_shape=jax.ShapeDtypeStruct((rows, n), F32),
        compiler_params=_cparams("arbitrary"),
        name="ada",
    )(c_rows, ada_w, ada_b.reshape(1, n))


def _in_proj_kernel(x_ref, xn_ref, nw_ref, shift_ref, scale_ref, shiftn_ref, scalen_ref, w_ref, o_ref, h_ref):
    t, j = pl.program_id(0), pl.program_id(1)
    nj = pl.num_programs(1)
    slot = t % 2

    def prepare(x, shift, scale, dst):
        y = x * lax.rsqrt(jnp.mean(x * x, axis=-1, keepdims=True) + EPS) * nw_ref[...]
        h_ref[dst] = (y * (1.0 + scale) + shift).astype(BF16)

    def project():
        o_ref[...] = jnp.dot(h_ref[slot], w_ref[...], preferred_element_type=F32).astype(o_ref.dtype)

    @pl.when((t == 0) & (j == 0))
    def _():
        prepare(x_ref[...], shift_ref[...], scale_ref[...], 0)

    @pl.when(j < nj - 1)
    def _():
        project()

    @pl.when(j == nj - 1)
    def _():
        prepare(xn_ref[...], shiftn_ref[...], scalen_ref[...], 1 - slot)
        project()


def _in_proj(x, norm_w, shift, scale, w, tm, tn):
    b, l, d = x.shape
    n = w.shape[1]
    per_batch = shift.shape[0] > 1
    tiles_per_batch = l // tm
    n_tiles = b * tiles_per_batch
    nxt = lambda t: jnp.minimum(t + 1, n_tiles - 1)
    row = lambda t, j: (0, 0)
    row_next = lambda t, j: (nxt(t), 0)
    mod = lambda t, j: (0, 0, 0)
    mod_next = (lambda t, j: (nxt(t) // tiles_per_batch, 0, 0)) if per_batch else (lambda t, j: (0, 0, 0))
    x2 = x.reshape(b * l, d)
    out = pl.pallas_call(
        _in_proj_kernel,
        grid=(n_tiles, n // tn),
        in_specs=[pl.BlockSpec((tm, d), row),
                  pl.BlockSpec((tm, d), row_next),
                  pl.BlockSpec((1, d), lambda t, j: (0, 0)),
                  pl.BlockSpec((None, 1, d), mod),
                  pl.BlockSpec((None, 1, d), mod),
                  pl.BlockSpec((None, 1, d), mod_next),
                  pl.BlockSpec((None, 1, d), mod_next),
                  pl.BlockSpec((d, tn), lambda t, j: (0, j))],
        out_specs=pl.BlockSpec((tm, tn), lambda t, j: (t, j)),
        out_shape=jax.ShapeDtypeStruct((b * l, n), BF16),
        scratch_shapes=[pltpu.VMEM((2, tm, d), BF16)],
        compiler_params=_cparams("arbitrary", "arbitrary"),
        name="in_proj",
    )(x2, x2, norm_w.reshape(1, d), shift, scale, shift, scale, w)
    return out.reshape(b, l, n)


def _head_rms(x, w):
    return x * lax.rsqrt(jnp.mean(x * x, axis=-1, keepdims=True) + EPS) * w


def _attn_kernel(q_ref, kc_ref, k_ref, vc_ref, v_ref, cq_ref, sq_ref, ck_ref, sk_ref, qnw_ref, knw_ref,
                 o_ref, k_s, v_s, *, lc, group, tq_sub):
    n_kv = k_s.shape[0]

    @pl.when(pl.program_id(1) == 0)
    def _():
        knw = knw_ref[...]
        for kv in range(n_kv):
            hs = slice(kv * HEAD_DIM, (kv + 1) * HEAD_DIM)
            k_s[kv, 0:lc, :] = _head_rms(kc_ref[:, hs].astype(F32), knw).astype(BF16)
            kn = _head_rms(k_ref[:, hs].astype(F32), knw)
            k_s[kv, lc:, :] = (kn * ck_ref[...] + _rot_half(kn) * sk_ref[...]).astype(BF16)
            v_s[kv, 0:lc, 0:HEAD_DIM] = vc_ref[:, hs]
            v_s[kv, lc:, 0:HEAD_DIM] = v_ref[:, hs]
            v_s[kv, :, HEAD_DIM:] = jnp.ones((v_s.shape[1], HEAD_DIM), BF16)

    c = (HEAD_DIM ** -0.5) * 1.4426950408889634
    qnw = qnw_ref[...]
    tq = tq_sub

    def scores(unit):
        r0, g = unit
        rows = slice(r0, r0 + tq)
        qn = _head_rms(q_ref[rows, g * HEAD_DIM:(g + 1) * HEAD_DIM].astype(F32), qnw)
        qr = (qn * (cq_ref[rows, :] * c) + _rot_half(qn) * (sq_ref[rows, :] * c)).astype(BF16)
        return _dot_nt(qr, k_s[g // group])

    def output(unit, s):
        r0, g = unit
        m = jnp.max(s, axis=-1, keepdims=True)
        p = jnp.exp2((s - m).astype(BF16))
        ol = jnp.dot(p, v_s[g // group], preferred_element_type=F32)
        o_ref[r0:r0 + tq, g * HEAD_DIM:(g + 1) * HEAD_DIM] = (
            ol[:, :HEAD_DIM] * (1.0 / ol[:, HEAD_DIM:HEAD_DIM + 1])).astype(o_ref.dtype)

    units = [(r0, g) for r0 in range(0, q_ref.shape[0], tq) for g in range(n_kv * group)]
    s_next = scores(units[0])
    for idx, unit in enumerate(units):
        s = s_next
        if idx + 1 < len(units):
            s_next = scores(units[idx + 1])
        output(unit, s)


def _attention(p, pc, rope_c, rope_s, q_norm_w, k_norm_w, tq):
    b, l, _ = p.shape
    lc = pc.shape[1]
    group = ATTN_HEADS // ATTN_KV_HEADS
    kq0 = OFF["qa"] // ATTN_Q_W
    kk0 = OFF["ka"] // ATTN_KV_W
    kv0 = OFF["va"] // ATTN_KV_W
    full = lambda bi, i: (0, 0)
    return pl.pallas_call(
        functools.partial(_attn_kernel, lc=lc, group=group, tq_sub=min(256, tq)),
        grid=(b, l // tq),
        in_specs=[pl.BlockSpec((None, tq, ATTN_Q_W), lambda bi, i: (bi, i, kq0)),
                  pl.BlockSpec((None, lc, ATTN_KV_W), lambda bi, i: (bi, 0, kk0)),
                  pl.BlockSpec((None, l, ATTN_KV_W), lambda bi, i: (bi, 0, kk0)),
                  pl.BlockSpec((None, lc, ATTN_KV_W), lambda bi, i: (bi, 0, kv0)),
                  pl.BlockSpec((None, l, ATTN_KV_W), lambda bi, i: (bi, 0, kv0)),
                  pl.BlockSpec((tq, HEAD_DIM), lambda bi, i: (i, 0)),
                  pl.BlockSpec((tq, HEAD_DIM), lambda bi, i: (i, 0)),
                  pl.BlockSpec((l, HEAD_DIM), full),
                  pl.BlockSpec((l, HEAD_DIM), full),
                  pl.BlockSpec((1, HEAD_DIM), full),
                  pl.BlockSpec((1, HEAD_DIM), full)],
        out_specs=pl.BlockSpec((None, tq, ATTN_Q_W), lambda bi, i: (bi, i, 0)),
        out_shape=jax.ShapeDtypeStruct((b, l, ATTN_Q_W), BF16),
        scratch_shapes=[pltpu.VMEM((ATTN_KV_HEADS, lc + l, HEAD_DIM), BF16),
                        pltpu.VMEM((ATTN_KV_HEADS, lc + l, 2 * HEAD_DIM), BF16)],
        compiler_params=_cparams("parallel", "arbitrary"),
        name="attention",
    )(p, pc, p, pc, p, rope_c, rope_s, rope_c, rope_s, q_norm_w.reshape(1, HEAD_DIM), k_norm_w.reshape(1, HEAD_DIM))


def _ret_kernel(q_ref, k_ref, v_ref, g_ref, kc_ref, vc_ref, cos_ref, sin_ref, z_ref, nw_ref, o_ref,
                q_s, k_s, acc, st_f, st_b, *, nc, lc):
    ch = RET_CHUNK
    assert nc % 2 == 0
    scale = RET_DK ** -0.5
    lg_f = -jnp.exp(z_ref[0])
    lg_b = -jnp.exp(z_ref[1])
    row = lax.broadcasted_iota(I32, (ch, ch), 0).astype(F32)
    col = lax.broadcasted_iota(I32, (ch, ch), 1).astype(F32)
    d = row - col
    intra_f = jnp.where(d >= 0, jnp.exp(lg_f * jnp.maximum(d, 0.0)), 0.0)
    intra_b = jnp.where(d <= 0, jnp.exp(lg_b * jnp.maximum(-d, 0.0)), 0.0)

    def wide(a):
        return jnp.concatenate([a] * (RET_DV // LANES), axis=1)

    qd_f = wide(jnp.exp(lg_f * (row + 1.0)))
    kd_f = jnp.exp(lg_f * (ch - 1.0 - row))
    cd_f = wide(jnp.exp(lg_f * float(ch)))
    qd_b = wide(jnp.exp(lg_b * (ch - row)))
    kd_b = jnp.exp(lg_b * row)
    cd_b = wide(jnp.exp(lg_b * float(ch)))

    j = lax.broadcasted_iota(I32, (lc, RET_DK), 0).astype(F32)
    kc = kc_ref[...].astype(F32) * scale
    vc = vc_ref[...]
    st_f[...] = _dot_tn((kc * jnp.exp(lg_f * (lc - 1.0 - j))).astype(BF16), vc)
    st_b[...] = _dot_tn((kc * jnp.exp(lg_b * j)).astype(BF16), vc)

    nw = nw_ref[...]
    fwd = (st_f, intra_f, qd_f, kd_f, cd_f)
    bwd = (st_b, intra_b, qd_b, kd_b, cd_b)

    span = 4 if (nc // 2) % 4 == 0 else (2 if (nc // 2) % 2 == 0 else 1)

    def block(b0, first_visit):
        visits = []
        for u in range(span):
            i = b0 * span + u
            visits.append((pl.multiple_of(i * ch, ch), fwd))
            visits.append((pl.multiple_of((nc - 1 - i) * ch, ch), bwd))
        loaded = []
        for r0, _ in visits:
            rows = pl.ds(r0, ch)
            if first_visit:
                cs, sn = cos_ref[rows, :], sin_ref[rows, :]
                qf, kf = q_ref[rows, :].astype(F32), k_ref[rows, :].astype(F32)
                qb = (qf * cs + _rot_half(qf) * sn).astype(BF16)
                kr = (kf * cs + _rot_half(kf) * sn) * scale
                extra = None
            else:
                qb, kr = q_s[rows, :], k_s[rows, :]
                extra = (acc[rows, :], g_ref[rows, :])
            loaded.append((qb, kr, v_ref[rows, :], extra))
        raw = [_dot_nt(qb, kr.astype(BF16)) for qb, kr, _, _ in loaded]
        upd = [_dot_tn((kr * direction[3]).astype(BF16), vb)
               for (_, direction), (_, kr, vb, _) in zip(visits, loaded)]
        state = {id(fwd): st_f[...], id(bwd): st_b[...]}
        before = []
        for (_, direction), u_ in zip(visits, upd):
            before.append(state[id(direction)])
            state[id(direction)] = state[id(direction)] * direction[4] + u_
        cross = [jnp.dot(qb, s0.astype(BF16), preferred_element_type=F32)
                 for (qb, _, _, _), s0 in zip(loaded, before)]
        outs = []
        for (r0, direction), (qb, kr, vb, extra), s_raw, cr in zip(visits, loaded, raw, cross):
            _, intra, qd, kd, cd = direction
            o = jnp.dot((s_raw * intra).astype(BF16), vb, preferred_element_type=F32) + cr * qd
            if not first_visit:
                o = o + extra[0]
                mu = jnp.mean(o, axis=-1, keepdims=True)
                oc = o - mu
                var = jnp.mean(oc * oc, axis=-1, keepdims=True)
                o = (_silu(extra[1].astype(F32)) * (oc * lax.rsqrt(var + EPS) * nw)).astype(o_ref.dtype)
            outs.append(o)
        for (r0, _), (qb, kr, _, _), o in zip(visits, loaded, outs):
            rows = pl.ds(r0, ch)
            if first_visit:
                q_s[rows, :] = qb
                k_s[rows, :] = kr
                acc[rows, :] = o
            else:
                o_ref[rows, :] = o
        st_f[...] = state[id(fwd)]
        st_b[...] = state[id(bwd)]

    n_blocks = (nc // 2) // span

    def first_half(b0, carry):
        block(b0, True)
        return carry

    def second_half(b0, carry):
        block(b0, False)
        return carry

    lax.fori_loop(0, n_blocks, first_half, 0)
    lax.fori_loop(n_blocks, 2 * n_blocks, second_half, 0)


def _retention(p, pc, rope_c, rope_s, decay_fwd, decay_bwd, ret_norm_w):
    b, l, _ = p.shape
    lc = pc.shape[1]
    nc = l // RET_CHUNK
    z = jnp.broadcast_to(jnp.stack([decay_fwd, decay_bwd])[:, :, None, None], (2, RET_HEADS, 1, LANES))
    kq0, kk0 = OFF["qr"] // RET_DK, OFF["kr"] // RET_DK
    kv0, kg0 = OFF["vr"] // RET_DV, OFF["gr"] // RET_DV
    full = lambda bi, h: (0, 0)
    return pl.pallas_call(
        functools.partial(_ret_kernel, nc=nc, lc=lc),
        grid=(b, RET_HEADS),
        in_specs=[pl.BlockSpec((None, l, RET_DK), lambda bi, h: (bi, 0, kq0 + h)),
                  pl.BlockSpec((None, l, RET_DK), lambda bi, h: (bi, 0, kk0 + h)),
                  pl.BlockSpec((None, l, RET_DV), lambda bi, h: (bi, 0, kv0 + h)),
                  pl.BlockSpec((None, l, RET_DV), lambda bi, h: (bi, 0, kg0 + h)),
                  pl.BlockSpec((None, lc, RET_DK), lambda bi, h: (bi, 0, kk0 + h)),
                  pl.BlockSpec((None, lc, RET_DV), lambda bi, h: (bi, 0, kv0 + h)),
                  pl.BlockSpec((l, RET_DK), full),
                  pl.BlockSpec((l, RET_DK), full),
                  pl.BlockSpec((2, None, 1, LANES), lambda bi, h: (0, h, 0, 0)),
                  pl.BlockSpec((1, RET_DV), lambda bi, h: (0, h))],
        out_specs=pl.BlockSpec((None, l, RET_DV), lambda bi, h: (bi, 0, h)),
        out_shape=jax.ShapeDtypeStruct((b, l, RET_V_W), BF16),
        scratch_shapes=[pltpu.VMEM((l, RET_DK), BF16), pltpu.VMEM((l, RET_DK), F32), pltpu.VMEM((l, RET_DV), F32),
                        pltpu.VMEM((RET_DK, RET_DV), F32), pltpu.VMEM((RET_DK, RET_DV), F32)],
        compiler_params=_cparams("parallel", "parallel"),
        name="retention",
    )(p, p, p, p, pc, pc, rope_c, rope_s, z, ret_norm_w.reshape(1, RET_V_W))


def _merge_kernel(attn_ref, ret_ref, *refs, n_gate_blocks):
    gate_blocks = refs[:2 * n_gate_blocks]
    x_ref, gate_ref, shift_ref, scale_ref, nw_ref, wa_ref, wr_ref, wm_ref, x1_ref, h2_ref = refs[2 * n_gate_blocks:]
    ga = jnp.concatenate([g[...] for g in gate_blocks[:n_gate_blocks]], axis=1)
    gr = jnp.concatenate([g[...] for g in gate_blocks[n_gate_blocks:]], axis=1)
    a = jnp.dot(attn_ref[...], wa_ref[...], preferred_element_type=F32)
    r = jnp.dot(ret_ref[...], wr_ref[...], preferred_element_type=F32)
    mixed = _sigmoid(ga.astype(F32)) * a + _sigmoid(gr.astype(F32)) * r
    y = jnp.dot(mixed.astype(BF16), wm_ref[...], preferred_element_type=F32)
    x1 = x_ref[...] + gate_ref[...] * y
    x1_ref[...] = x1
    hn = x1 * lax.rsqrt(jnp.mean(x1 * x1, axis=-1, keepdims=True) + EPS) * nw_ref[...]
    h2_ref[...] = (hn * (1.0 + scale_ref[...]) + shift_ref[...]).astype(BF16)


def _merge(attn, ret, p, x, gate, shift2, scale2, norm2_w, wa, wr, wm, tm):
    b, l, d = x.shape
    assert d == D_MODEL
    n_gate_blocks = d // GATE_BLOCK
    gate_block0 = OFF["ga"] // GATE_BLOCK
    assert OFF["ga"] % GATE_BLOCK == 0 and OFF["gt"] == OFF["ga"] + d
    row = lambda bi, i: (bi, i, 0)
    mod = lambda bi, i: (bi, 0, 0)
    const = lambda bi, i: (0, 0)
    once = pl.Buffered(1)
    gate_specs = [pl.BlockSpec((None, tm, GATE_BLOCK), functools.partial(lambda kb, bi, i: (bi, i, kb), gate_block0 + kb))
                  for kb in range(2 * n_gate_blocks)]
    return pl.pallas_call(
        functools.partial(_merge_kernel, n_gate_blocks=n_gate_blocks),
        grid=(b, l // tm),
        in_specs=[pl.BlockSpec((None, tm, ATTN_Q_W), row),
                  pl.BlockSpec((None, tm, RET_V_W), row),
                  *gate_specs,
                  pl.BlockSpec((None, tm, d), row),
                  pl.BlockSpec((None, 1, d), mod),
                  pl.BlockSpec((None, 1, d), mod),
                  pl.BlockSpec((None, 1, d), mod),
                  pl.BlockSpec((1, d), const),
                  pl.BlockSpec(wa.shape, const, pipeline_mode=once),
                  pl.BlockSpec(wr.shape, const, pipeline_mode=once),
                  pl.BlockSpec(wm.shape, const, pipeline_mode=once)],
        out_specs=[pl.BlockSpec((None, tm, d), row), pl.BlockSpec((None, tm, d), row)],
        out_shape=[jax.ShapeDtypeStruct((b, l, d), F32), jax.ShapeDtypeStruct((b, l, d), BF16)],
        compiler_params=_cparams("parallel", "parallel"),
        name="merge",
    )(attn, ret, *([p] * (2 * n_gate_blocks)), x, gate, shift2, scale2, norm2_w.reshape(1, d), wa, wr, wm)


def _run(gen):
    try:
        while True:
            next(gen)
    except StopIteration as stop:
        return stop.value


def _top_rows_gen(s, k):
    n = s.shape[0]
    rid = lax.broadcasted_iota(I32, s.shape, 0)
    vals, ids = [], []
    for _ in range(k):
        m = jnp.max(s, axis=0, keepdims=True)
        sel = jnp.min(jnp.where(s == m, rid, n), axis=0, keepdims=True)
        vals.append(m)
        ids.append(sel)
        s = jnp.where(rid == sel, -jnp.inf, s)
        yield
    return jnp.concatenate(vals, axis=0), jnp.concatenate(ids, axis=0)


def _top_rows(s, k):
    return _run(_top_rows_gen(s, k))


def _pair_candidates(s1, s2):
    k = PEER_TOPK
    t = s1.shape[1]
    r8 = lax.broadcasted_iota(I32, (SUBLANES, t), 0)
    r16 = lax.broadcasted_iota(I32, (k, t), 0)
    neg = -jnp.inf
    sums = [s1[0:1] + s2]
    flat = [r16]
    for a, nb in ((1, 8), (2, 5), (3, 4)):
        sums.append(jnp.where(r8 < nb, s1[a:a + 1] + s2[0:8], neg))
        flat.append(a * k + r8)
    sums.append(s1[8:16] + s2[0:1])
    flat.append((r8 + 8) * k)
    for b_, lo, hi in ((0, 4, 8), (1, 4, 8), (2, 4, 5)):
        sums.append(jnp.where((r8 >= lo) & (r8 < hi), s1[0:8] + s2[b_:b_ + 1], neg))
        flat.append(r8 * k + b_)
    return jnp.concatenate(sums, axis=0), jnp.concatenate(flat, axis=0)


def _pair_topk_gen(s1, n1, s2, n2):
    k = PEER_TOPK
    cand, flat = _pair_candidates(s1, s2)
    tops, picks = [], []
    for _ in range(k):
        m = jnp.max(cand, axis=0, keepdims=True)
        f = jnp.min(jnp.where(cand == m, flat, k * k), axis=0, keepdims=True)
        cand = jnp.where(flat == f, -jnp.inf, cand)
        tops.append(m)
        picks.append(f)
        yield
    top = jnp.concatenate(tops, axis=0)
    pick = jnp.concatenate(picks, axis=0)
    pa = pick // k
    pb = pick - pa * k
    e1 = jnp.zeros_like(pick)
    e2 = jnp.zeros_like(pick)
    for a in range(k):
        e1 = jnp.where(pa == a, n1[a:a + 1], e1)
        e2 = jnp.where(pb == a, n2[a:a + 1], e2)
    ex = jnp.exp(top - top[0:1])
    return e1, e2, ex / jnp.sum(ex, axis=0, keepdims=True)


def _pair_topk(s1, n1, s2, n2):
    return _run(_pair_topk_gen(s1, n1, s2, n2))


def _peer_q_kernel(h_ref, wq_ref, q_ref):
    q_ref[...] = jnp.dot(h_ref[...], wq_ref[...], preferred_element_type=F32).astype(q_ref.dtype)


def _peer_q(h2, wq, tm):
    n, d = h2.shape
    return pl.pallas_call(
        _peer_q_kernel,
        grid=(n // tm,),
        in_specs=[pl.BlockSpec((tm, d), lambda i: (i, 0)),
                  pl.BlockSpec(wq.shape, lambda i: (0, 0), pipeline_mode=pl.Buffered(1))],
        out_specs=pl.BlockSpec((tm, wq.shape[1]), lambda i: (i, 0)),
        out_shape=jax.ShapeDtypeStruct((n, wq.shape[1]), BF16),
        compiler_params=_cparams("parallel"),
        name="peer_q",
    )(h2, wq)


def _peer_topk_kernel(q_s, keys_ref, i1_ref, i2_ref, g_ref, i1_s, i2_s, g_s):
    k = PEER_TOPK

    def head(h, carry):
        c0 = pl.multiple_of(h * 2 * PEER_D_HALF, 2 * PEER_D_HALF)
        qh = q_s[:, pl.ds(c0, 2 * PEER_D_HALF)]
        s1, n1 = _top_rows(_dot_nt(keys_ref[2 * h], qh[:, :PEER_D_HALF]), k)
        s2, n2 = _top_rows(_dot_nt(keys_ref[2 * h + 1], qh[:, PEER_D_HALF:]), k)
        e1, e2, gates = _pair_topk(s1, n1, s2, n2)
        r0 = pl.multiple_of(h * k, k)
        g_s[pl.ds(r0, k), :] = gates
        i1_s[pl.ds(r0, k), :] = e1
        i2_s[pl.ds(r0, k), :] = e2
        return carry

    lax.fori_loop(0, PEER_HEADS, head, 0)
    i1_ref[...] = i1_s[...].T
    i2_ref[...] = i2_s[...].T
    g_ref[...] = g_s[...].T


def _peer_topk(q, keys, n, tm):
    slots = PEER_HEADS * PEER_TOPK
    row = lambda i: (i, 0)
    out = jax.ShapeDtypeStruct
    return pl.pallas_call(
        _peer_topk_kernel,
        grid=(n // tm,),
        in_specs=[pl.BlockSpec((tm, q.shape[1]), row),
                  pl.BlockSpec(keys.shape, lambda i: (0, 0, 0))],
        out_specs=[pl.BlockSpec((tm, slots), row)] * 3,
        out_shape=[out((n, slots), I32), out((n, slots), I32), out((n, slots), F32)],
        scratch_shapes=[pltpu.VMEM((slots, tm), I32), pltpu.VMEM((slots, tm), I32), pltpu.VMEM((slots, tm), F32)],
        compiler_params=_cparams("parallel"),
        name="peer_topk",
    )(q, keys)


def _peer_w_kernel(i1_ref, i2_ref, g_ref, w_ref, w3):
    nk = PEER_N_KEYS
    tb = i1_ref.shape[0]
    grp = 2 * SUBLANES
    rid = lax.broadcasted_iota(I32, (nk, i1_ref.shape[1]), 0)
    sub = lax.broadcasted_iota(I32, (SUBLANES, nk), 0)

    def rows_to_tokens(v):
        v = list(v)
        for dist in (4, 2, 1):
            keep = (sub & dist) == 0
            for k in range(SUBLANES):
                if k & dist == 0:
                    a, b_ = v[k], v[k + dist]
                    v[k] = jnp.where(keep, a, pltpu.roll(b_, dist, 0))
                    v[k + dist] = jnp.where(keep, pltpu.roll(a, SUBLANES - dist, 0), b_)
        return v

    def group(gi, carry):
        t0 = pl.multiple_of(gi * grp, grp)
        i1 = i1_ref[pl.ds(t0, grp), :]
        i2 = i2_ref[pl.ds(t0, grp), :]
        g = g_ref[pl.ds(t0, grp), :]
        for k in range(grp):
            left = jnp.where(rid == i1[k:k + 1], g[k:k + 1], 0.0).astype(BF16)
            right = jnp.where(rid == i2[k:k + 1], 1.0, 0.0).astype(BF16)
            w3[k * nk:(k + 1) * nk, :] = _dot_nt(left, right)
        for j in range(nk // SUBLANES):
            halves = []
            for half in range(grp // SUBLANES):
                tiles = [w3[(half * SUBLANES + k) * nk + j * SUBLANES:(half * SUBLANES + k) * nk + (j + 1) * SUBLANES, :]
                         for k in range(SUBLANES)]
                halves.append(rows_to_tokens(tiles))
            for i in range(SUBLANES):
                r = j * SUBLANES + i
                tile = jnp.concatenate([h[i] for h in halves], axis=0)
                w_ref[pl.ds(t0, grp), r * nk:(r + 1) * nk] = tile.astype(w_ref.dtype)
        return carry

    lax.fori_loop(0, tb // grp, group, 0, unroll=4)


def _peer_w(i1, i2, g, tb):
    n, slots = i1.shape
    ne = PEER_N_KEYS * PEER_N_KEYS
    row = lambda i: (i, 0)
    return pl.pallas_call(
        _peer_w_kernel,
        grid=(n // tb,),
        in_specs=[pl.BlockSpec((tb, slots), row)] * 3,
        out_specs=pl.BlockSpec((tb, ne), row),
        out_shape=jax.ShapeDtypeStruct((n, ne), BF16),
        scratch_shapes=[pltpu.VMEM((2 * SUBLANES * PEER_N_KEYS, PEER_N_KEYS), F32)],
        compiler_params=_cparams("parallel"),
        name="peer_w",
    )(i1, i2, g)


def _gelu_tanh(x):
    return 0.5 * x * (1.0 + jnp.tanh(0.7978845608028654 * (x + 0.044715 * (x * x * x))))


def _peer_dense_kernel(*refs, retrieve_next):
    if retrieve_next:
        (h_ref, u_ref, v_ref, w_ref, x_ref, gate_ref, nw_ref, qn_ref, keys_ref,
         o_ref, i1_ref, i2_ref, g_ref, acc, a_buf, s1_s, n1_s, i1_s, i2_s, g_s) = refs
    else:
        h_ref, u_ref, v_ref, w_ref, x_ref, gate_ref, nw_ref, o_ref, acc, a_buf = refs
    j = pl.program_id(1)
    nj = pl.num_programs(1) - 1
    k = PEER_TOPK

    def score(slot):
        a_buf[slot] = jnp.dot(h_ref[...], u_ref[...], preferred_element_type=F32)

    def finish(slot):
        act = (_gelu_tanh(a_buf[slot]) * w_ref[...].astype(F32)).astype(BF16)
        acc[...] += jnp.dot(act, v_ref[...], preferred_element_type=F32)

    def retrieval_chains(half):
        if not retrieve_next:
            return [], 0
        r0 = pl.multiple_of((j // 2) * k, k)
        groups = range(0, qn_ref.shape[0], LANES)

        def chain(t0):
            tl = slice(t0, t0 + LANES)
            s, n = yield from _top_rows_gen(_dot_nt(keys_ref[...], qn_ref[tl, :]), k)
            if half == 0:
                s1_s[:, tl] = s
                n1_s[:, tl] = n
            else:
                e1, e2, gates = yield from _pair_topk_gen(s1_s[:, tl], n1_s[:, tl], s, n)
                i1_s[pl.ds(r0, k), tl] = e1
                i2_s[pl.ds(r0, k), tl] = e2
                g_s[pl.ds(r0, k), tl] = gates

        rounds = (k if half == 0 else 2 * k) + 1
        return [chain(t0) for t0 in groups], rounds

    def interleave(main, filler):
        chains, _ = filler
        roomy = [idx for idx, (_, weight) in enumerate(main) if weight >= 1.0]
        start = {}
        for ci, chain in enumerate(chains):
            start.setdefault(roomy[ci * len(roomy) // len(chains)], []).append(chain)
        for idx, (piece, _) in enumerate(main):
            for chain in start.get(idx, ()):
                _run(chain)
            piece()

    def score_pieces(slot):
        half_n = a_buf.shape[2] // 2

        def piece(c0):
            a_buf[slot, :, c0:c0 + half_n] = _dot_nt(h_ref[...], u_ref[c0:c0 + half_n, :])
        return [(functools.partial(piece, c0), 1.0) for c0 in (0, half_n)]

    def finish_pieces(slot):
        quarter = acc.shape[1] // 4
        cell = {}

        def piece(c0):
            if "act" not in cell:
                cell["act"] = (_gelu_tanh(a_buf[slot]) * w_ref[...].astype(F32)).astype(BF16)
            acc[:, c0:c0 + quarter] += jnp.dot(cell["act"], v_ref[:, c0:c0 + quarter], preferred_element_type=F32)
        return [(functools.partial(piece, c0), 0.25 if c0 == 0 else 1.0) for c0 in range(0, acc.shape[1], quarter)]

    @pl.when(j == 0)
    def _():
        acc[...] = jnp.zeros_like(acc)
        interleave(score_pieces(0), retrieval_chains(0))

    for parity in range(2):
        @pl.when((j > 0) & (j < nj) & (j % 2 == parity))
        def _():
            interleave(score_pieces(parity) + finish_pieces(1 - parity), retrieval_chains(parity))

    @pl.when(j == nj)
    def _():
        finish((nj - 1) % 2)
        x2 = x_ref[...] + gate_ref[...] * acc[...]
        o_ref[...] = x2 * lax.rsqrt(jnp.mean(x2 * x2, axis=-1, keepdims=True) + EPS) * nw_ref[...]
        if retrieve_next:
            i1_ref[...] = i1_s[...].T
            i2_ref[...] = i2_s[...].T
            g_ref[...] = g_s[...].T


def _peer_dense(h2, u, v, w, x1, gate, norm_f_w, seq, tm, te, tile0, n_tiles, retrieval=None):
    n, d = h2.shape
    ne = u.shape[0]
    nj = ne // te
    assert nj % 2 == 0
    tiles_per_batch = seq // tm
    row = lambda i, j: (tile0 + i, 0)
    scored = lambda i, j: (jnp.minimum(j, nj - 1), 0)
    finished = lambda j: jnp.maximum(j - 1, 0)
    in_specs = [pl.BlockSpec((tm, d), row),
                pl.BlockSpec((te, d), scored),
                pl.BlockSpec((te, d), lambda i, j: (finished(j), 0)),
                pl.BlockSpec((tm, te), lambda i, j: (i, finished(j))),
                pl.BlockSpec((tm, d), row),
                pl.BlockSpec((None, 1, d), lambda i, j: ((tile0 + i) // tiles_per_batch, 0, 0)),
                pl.BlockSpec((1, d), lambda i, j: (0, 0))]
    args = [h2, u, v, w, x1, gate, norm_f_w.reshape(1, d)]
    out_specs = [pl.BlockSpec((tm, d), row)]
    out_shape = [jax.ShapeDtypeStruct((n, d), F32)]
    scratch = [pltpu.VMEM((tm, d), F32), pltpu.VMEM((2, tm, te), F32)]
    if retrieval is not None:
        q, keys = retrieval
        slots = PEER_HEADS * PEER_TOPK
        assert nj == keys.shape[0]
        half_head = lambda j: jnp.minimum(j, nj - 1)
        in_specs += [pl.BlockSpec((tm, PEER_D_HALF), lambda i, j: (tile0 + n_tiles + i, half_head(j))),
                     pl.BlockSpec((None, PEER_N_KEYS, PEER_D_HALF), lambda i, j: (half_head(j), 0, 0))]
        args += [q, keys]
        out_specs += [pl.BlockSpec((tm, slots), lambda i, j: (i, 0))] * 3
        out_shape += [jax.ShapeDtypeStruct((n_tiles * tm, slots), t) for t in (I32, I32, F32)]
        scratch += [pltpu.VMEM((PEER_TOPK, tm), F32), pltpu.VMEM((PEER_TOPK, tm), I32),
                    pltpu.VMEM((slots, tm), I32), pltpu.VMEM((slots, tm), I32), pltpu.VMEM((slots, tm), F32)]
    outs = pl.pallas_call(
        functools.partial(_peer_dense_kernel, retrieve_next=retrieval is not None),
        grid=(n_tiles, nj + 1),
        in_specs=in_specs,
        out_specs=out_specs,
        out_shape=out_shape,
        scratch_shapes=scratch,
        input_output_aliases={4: 0},
        compiler_params=_cparams("parallel", "arbitrary"),
        name="peer_dense",
    )(*args)
    return outs[0], tuple(outs[1:])


def _rope_tables(length):
    t = jnp.arange(length, dtype=jnp.int32)
    row = (t // GRID_W).astype(F32)
    col = (t % GRID_W).astype(F32)
    n_freq = HEAD_DIM // 4
    inv_freq = ROPE_BASE ** (-jnp.arange(n_freq, dtype=F32) / n_freq)
    ang = jnp.concatenate([row[:, None] * inv_freq, col[:, None] * inv_freq], axis=-1)
    cos, sin = jnp.cos(ang), jnp.sin(ang)
    return jnp.concatenate([cos, cos], axis=-1), jnp.concatenate([-sin, sin], axis=-1)


def _layer(x, ctx, c_rows, rope_c, rope_s, ada_w, ada_b, norm1_w, w_in, q_norm_w, k_norm_w, ret_decay_fwd,
           ret_decay_bwd, ret_norm_w, w_attn_branch, w_ret_branch, w_merge_out, norm2_w, peer_w_q, peer_keys,
           peer_u, peer_v, norm_f_w):
    b, l, d = x.shape
    lc = ctx.shape[1]
    mod = _ada(c_rows, ada_w, ada_b)
    mod_x = [mod[:b, i * d:(i + 1) * d].reshape(b, 1, d) for i in range(6)]
    mod_c = [mod[b:b + 1, i * d:(i + 1) * d].reshape(1, 1, d) for i in range(2)]

    w_in_b = w_in.astype(BF16)
    n_in = w_in.shape[1]
    tn = n_in // 4
    p = _in_proj(x, norm1_w, mod_x[0], mod_x[1], w_in_b, tm=min(512, l), tn=tn)
    pc = _in_proj(ctx, norm1_w, mod_c[0], mod_c[1], w_in_b, tm=min(256, lc), tn=tn)

    attn = _attention(p, pc, rope_c, rope_s, q_norm_w, k_norm_w, tq=min(512, l))
    ret = _retention(p, pc, rope_c, rope_s, ret_decay_fwd, ret_decay_bwd, ret_norm_w)
    x1, h2 = _merge(attn, ret, p, x, mod_x[2], mod_x[3], mod_x[4], norm2_w, w_attn_branch.astype(BF16),
                    w_ret_branch.astype(BF16), w_merge_out.astype(BF16), tm=min(256, l))

    n = b * l
    h2f = h2.reshape(n, d)
    keys = peer_keys.reshape(PEER_HEADS * 2, PEER_N_KEYS, PEER_D_HALF).astype(BF16)
    q = _peer_q(h2f, peer_w_q.astype(BF16), tm=min(1024, l))
    u, v = peer_u.astype(BF16), peer_v.astype(BF16)
    tm = min(512, l)
    n_tiles = n // tm
    n_chunks = next(c for c in (8, 4, 2, 1) if n_tiles % c == 0)
    tiles = n_tiles // n_chunks
    sel = _peer_topk(q, keys, n=tiles * tm, tm=256)
    out = x1.reshape(n, d)
    for chunk in range(n_chunks):
        w = _peer_w(*sel, tb=128)
        retrieval = (q, keys) if chunk + 1 < n_chunks else None
        out, sel = _peer_dense(h2f, u, v, w, out, mod_x[5], norm_f_w, seq=l, tm=tm, te=1024,
                               tile0=chunk * tiles, n_tiles=tiles, retrieval=retrieval)
    return out.reshape(b, l, d)


def kernel(x, c, ctx, c_ctx, ada_w, ada_b, norm1_w, w_in, q_norm_w, k_norm_w, ret_decay_fwd, ret_decay_bwd,
           ret_norm_w, w_attn_branch, w_ret_branch, w_merge_out, norm2_w, peer_w_q, peer_keys, peer_u, peer_v,
           norm_f_w):
    depth = ada_w.shape[0]
    assert depth == 1, "context-stream update between layers is not implemented"
    b, l, d = x.shape
    rows = -(-(b + 1) // SUBLANES) * SUBLANES
    c_rows = jnp.zeros((rows, d), F32).at[:b].set(c).at[b].set(c_ctx)
    rope_c, rope_s = _rope_tables(l)
    return _layer(x, ctx, c_rows, rope_c, rope_s, ada_w[0], ada_b[0], norm1_w[0], w_in[0], q_norm_w[0], k_norm_w[0],
                  ret_decay_fwd[0], ret_decay_bwd[0], ret_norm_w[0], w_attn_branch[0], w_ret_branch[0],
                  w_merge_out[0], norm2_w[0], peer_w_q[0], peer_keys[0], peer_u[0], peer_v[0], norm_f_w)
```

```python
import functools

import jax
import jax.numpy as jnp
from jax import lax
from jax.experimental import pallas as pl
from jax.experimental.pallas import tpu as pltpu

F32 = jnp.float32
BF16 = jnp.bfloat16
I32 = jnp.int32

EPS = 1e-6
GRID_W = 64
ROPE_BASE = 10000.0
ATTN_HEADS = 8
ATTN_KV_HEADS = 2
HEAD_DIM = 128
RET_HEADS = 8
RET_DK = 128
RET_DV = 256
RET_CHUNK = 128
PEER_HEADS = 8
PEER_N_KEYS = 128
PEER_D_HALF = 128
PEER_TOPK = 16

LANES = 128
SUBLANES = 8
VMEM_LIMIT = 60 * 1024 * 1024

ATTN_Q_W = ATTN_HEADS * HEAD_DIM
ATTN_KV_W = ATTN_KV_HEADS * HEAD_DIM
RET_QK_W = RET_HEADS * RET_DK
RET_V_W = RET_HEADS * RET_DV
D_MODEL = 2048
_IN_SPLITS = (("qa", ATTN_Q_W), ("ka", ATTN_KV_W), ("va", ATTN_KV_W), ("qr", RET_QK_W), ("kr", RET_QK_W),
              ("vr", RET_V_W), ("gr", RET_V_W), ("ga", D_MODEL), ("gt", D_MODEL))
OFF = {}
for _name, _width in _IN_SPLITS:
    OFF[_name] = sum(w for n, w in _IN_SPLITS[:len(OFF)])
GATE_BLOCK = 512


def _cparams(*sem):
    return pltpu.CompilerParams(dimension_semantics=sem, vmem_limit_bytes=VMEM_LIMIT)


def _sigmoid(x):
    return 1.0 / (1.0 + jnp.exp(-x))


def _silu(x):
    return x * _sigmoid(x)


def _rot_half(x):
    return pltpu.roll(x, HEAD_DIM // 2, 1)


def _dot_nt(a, b):
    return lax.dot_general(a, b, (((1,), (1,)), ((), ())), preferred_element_type=F32)


def _dot_tn(a, b):
    return lax.dot_general(a, b, (((0,), (0,)), ((), ())), preferred_element_type=F32)


def _ada_kernel(c_ref, w_ref, b_ref, o_ref):
    sc = _silu(c_ref[...]).astype(BF16)
    o_ref[...] = jnp.dot(sc, w_ref[...].astype(BF16), preferred_element_type=F32) + b_ref[...]


def _ada(c_rows, ada_w, ada_b):
    rows, d = c_rows.shape
    n = ada_w.shape[1]
    tn = 1536
    return pl.pallas_call(
        _ada_kernel,
        grid=(n // tn,),
        in_specs=[pl.BlockSpec((rows, d), lambda j: (0, 0)),
                  pl.BlockSpec((d, tn), lambda j: (0, j)),
                  pl.BlockSpec((1, tn), lambda j: (0, j))],
        out_specs=pl.BlockSpec((rows, tn), lambda j: (0, j)),
        out_shape=jax.ShapeDtypeStruct((rows, n), F32),
        compiler_params=_cparams("arbitrary"),
        name="ada",
    )(c_rows, ada_w, ada_b.reshape(1, n))


def _in_proj_kernel(x_ref, xn_ref, nw_ref, shift_ref, scale_ref, shiftn_ref, scalen_ref, w_ref, o_ref, h_ref):
    t, j = pl.program_id(0), pl.program_id(1)
    nj = pl.num_programs(1)
    slot = t % 2

    def prepare(x, shift, scale, dst):
        y = x * lax.rsqrt(jnp.mean(x * x, axis=-1, keepdims=True) + EPS) * nw_ref[...]
        h_ref[dst] = (y * (1.0 + scale) + shift).astype(BF16)

    def project():
        o_ref[...] = jnp.dot(h_ref[slot], w_ref[...], preferred_element_type=F32).astype(o_ref.dtype)

    @pl.when((t == 0) & (j == 0))
    def _():
        prepare(x_ref[...], shift_ref[...], scale_ref[...], 0)

    @pl.when(j < nj - 1)
    def _():
        project()

    @pl.when(j == nj - 1)
    def _():
        prepare(xn_ref[...], shiftn_ref[...], scalen_ref[...], 1 - slot)
        project()


def _in_proj(x, norm_w, shift, scale, w, tm, tn):
    b, l, d = x.shape
    n = w.shape[1]
    per_batch = shift.shape[0] > 1
    tiles_per_batch = l // tm
    n_tiles = b * tiles_per_batch
    nxt = lambda t: jnp.minimum(t + 1, n_tiles - 1)
    row = lambda t, j: (0, 0)
    row_next = lambda t, j: (nxt(t), 0)
    mod = lambda t, j: (0, 0, 0)
    mod_next = (lambda t, j: (nxt(t) // tiles_per_batch, 0, 0)) if per_batch else (lambda t, j: (0, 0, 0))
    x2 = x.reshape(b * l, d)
    out = pl.pallas_call(
        _in_proj_kernel,
        grid=(n_tiles, n // tn),
        in_specs=[pl.BlockSpec((tm, d), row),
                  pl.BlockSpec((tm, d), row_next),
                  pl.BlockSpec((1, d), lambda t, j: (0, 0)),
                  pl.BlockSpec((None, 1, d), mod),
                  pl.BlockSpec((None, 1, d), mod),
                  pl.BlockSpec((None, 1, d), mod_next),
                  pl.BlockSpec((None, 1, d), mod_next),
                  pl.BlockSpec((d, tn), lambda t, j: (0, j))],
        out_specs=pl.BlockSpec((tm, tn), lambda t, j: (t, j)),
        out_shape=jax.ShapeDtypeStruct((b * l, n), BF16),
        scratch_shapes=[pltpu.VMEM((2, tm, d), BF16)],
        compiler_params=_cparams("arbitrary", "arbitrary"),
        name="in_proj",
    )(x2, x2, norm_w.reshape(1, d), shift, scale, shift, scale, w)
    return out.reshape(b, l, n)


def _head_rms(x, w):
    return x * lax.rsqrt(jnp.mean(x * x, axis=-1, keepdims=True) + EPS) * w


def _attn_kernel(q_ref, kc_ref, k_ref, vc_ref, v_ref, cq_ref, sq_ref, ck_ref, sk_ref, qnw_ref, knw_ref,
                 o_ref, k_s, v_s, *, lc, group, tq_sub):
    n_kv = k_s.shape[0]

    @pl.when(pl.program_id(1) == 0)
    def _():
        knw = knw_ref[...]
        for kv in range(n_kv):
            hs = slice(kv * HEAD_DIM, (kv + 1) * HEAD_DIM)
            k_s[kv, 0:lc, :] = _head_rms(kc_ref[:, hs].astype(F32), knw).astype(BF16)
            kn = _head_rms(k_ref[:, hs].astype(F32), knw)
            k_s[kv, lc:, :] = (kn * ck_ref[...] + _rot_half(kn) * sk_ref[...]).astype(BF16)
            v_s[kv, 0:lc, 0:HEAD_DIM] = vc_ref[:, hs]
            v_s[kv, lc:, 0:HEAD_DIM] = v_ref[:, hs]
            v_s[kv, :, HEAD_DIM:] = jnp.ones((v_s.shape[1], HEAD_DIM), BF16)

    c = (HEAD_DIM ** -0.5) * 1.4426950408889634
    qnw = qnw_ref[...]
    tq = tq_sub

    def scores(unit):
        r0, g = unit
        rows = slice(r0, r0 + tq)
        qn = _head_rms(q_ref[rows, g * HEAD_DIM:(g + 1) * HEAD_DIM].astype(F32), qnw)
        qr = (qn * (cq_ref[rows, :] * c) + _rot_half(qn) * (sq_ref[rows, :] * c)).astype(BF16)
        return _dot_nt(qr, k_s[g // group])

    def output(unit, s):
        r0, g = unit
        m = jnp.max(s, axis=-1, keepdims=True)
        p = jnp.exp2((s - m).astype(BF16))
        ol = jnp.dot(p, v_s[g // group], preferred_element_type=F32)
        o_ref[r0:r0 + tq, g * HEAD_DIM:(g + 1) * HEAD_DIM] = (
            ol[:, :HEAD_DIM] * (1.0 / ol[:, HEAD_DIM:HEAD_DIM + 1])).astype(o_ref.dtype)

    units = [(r0, g) for r0 in range(0, q_ref.shape[0], tq) for g in range(n_kv * group)]
    s_next = scores(units[0])
    for idx, unit in enumerate(units):
        s = s_next
        if idx + 1 < len(units):
            s_next = scores(units[idx + 1])
        output(unit, s)


def _attention(p, pc, rope_c, rope_s, q_norm_w, k_norm_w, tq):
    b, l, _ = p.shape
    lc = pc.shape[1]
    group = ATTN_HEADS // ATTN_KV_HEADS
    kq0 = OFF["qa"] // ATTN_Q_W
    kk0 = OFF["ka"] // ATTN_KV_W
    kv0 = OFF["va"] // ATTN_KV_W
    full = lambda bi, i: (0, 0)
    return pl.pallas_call(
        functools.partial(_attn_kernel, lc=lc, group=group, tq_sub=min(256, tq)),
        grid=(b, l // tq),
        in_specs=[pl.BlockSpec((None, tq, ATTN_Q_W), lambda bi, i: (bi, i, kq0)),
                  pl.BlockSpec((None, lc, ATTN_KV_W), lambda bi, i: (bi, 0, kk0)),
                  pl.BlockSpec((None, l, ATTN_KV_W), lambda bi, i: (bi, 0, kk0)),
                  pl.BlockSpec((None, lc, ATTN_KV_W), lambda bi, i: (bi, 0, kv0)),
                  pl.BlockSpec((None, l, ATTN_KV_W), lambda bi, i: (bi, 0, kv0)),
                  pl.BlockSpec((tq, HEAD_DIM), lambda bi, i: (i, 0)),
                  pl.BlockSpec((tq, HEAD_DIM), lambda bi, i: (i, 0)),
                  pl.BlockSpec((l, HEAD_DIM), full),
                  pl.BlockSpec((l, HEAD_DIM), full),
                  pl.BlockSpec((1, HEAD_DIM), full),
                  pl.BlockSpec((1, HEAD_DIM), full)],
        out_specs=pl.BlockSpec((None, tq, ATTN_Q_W), lambda bi, i: (bi, i, 0)),
        out_shape=jax.ShapeDtypeStruct((b, l, ATTN_Q_W), BF16),
        scratch_shapes=[pltpu.VMEM((ATTN_KV_HEADS, lc + l, HEAD_DIM), BF16),
                        pltpu.VMEM((ATTN_KV_HEADS, lc + l, 2 * HEAD_DIM), BF16)],
        compiler_params=_cparams("parallel", "arbitrary"),
        name="attention",
    )(p, pc, p, pc, p, rope_c, rope_s, rope_c, rope_s, q_norm_w.reshape(1, HEAD_DIM), k_norm_w.reshape(1, HEAD_DIM))


def _ret_kernel(q_ref, k_ref, v_ref, g_ref, kc_ref, vc_ref, cos_ref, sin_ref, z_ref, nw_ref, o_ref,
                q_s, k_s, acc, st_f, st_b, *, nc, lc):
    ch = RET_CHUNK
    assert nc % 2 == 0
    scale = RET_DK ** -0.5
    lg_f = -jnp.exp(z_ref[0])
    lg_b = -jnp.exp(z_ref[1])
    row = lax.broadcasted_iota(I32, (ch, ch), 0).astype(F32)
    col = lax.broadcasted_iota(I32, (ch, ch), 1).astype(F32)
    d = row - col
    intra_f = jnp.where(d >= 0, jnp.exp(lg_f * jnp.maximum(d, 0.0)), 0.0)
    intra_b = jnp.where(d <= 0, jnp.exp(lg_b * jnp.maximum(-d, 0.0)), 0.0)

    def wide(a):
        return jnp.concatenate([a] * (RET_DV // LANES), axis=1)

    qd_f = wide(jnp.exp(lg_f * (row + 1.0)))
    kd_f = jnp.exp(lg_f * (ch - 1.0 - row))
    cd_f = wide(jnp.exp(lg_f * float(ch)))
    qd_b = wide(jnp.exp(lg_b * (ch - row)))
    kd_b = jnp.exp(lg_b * row)
    cd_b = wide(jnp.exp(lg_b * float(ch)))

    j = lax.broadcasted_iota(I32, (lc, RET_DK), 0).astype(F32)
    kc = kc_ref[...].astype(F32) * scale
    vc = vc_ref[...]
    st_f[...] = _dot_tn((kc * jnp.exp(lg_f * (lc - 1.0 - j))).astype(BF16), vc)
    st_b[...] = _dot_tn((kc * jnp.exp(lg_b * j)).astype(BF16), vc)

    nw = nw_ref[...]
    fwd = (st_f, intra_f, qd_f, kd_f, cd_f)
    bwd = (st_b, intra_b, qd_b, kd_b, cd_b)

    span = 4 if (nc // 2) % 4 == 0 else (2 if (nc // 2) % 2 == 0 else 1)

    def block(b0, first_visit):
        visits = []
        for u in range(span):
            i = b0 * span + u
            visits.append((pl.multiple_of(i * ch, ch), fwd))
            visits.append((pl.multiple_of((nc - 1 - i) * ch, ch), bwd))
        loaded = []
        for r0, _ in visits:
            rows = pl.ds(r0, ch)
            if first_visit:
                cs, sn = cos_ref[rows, :], sin_ref[rows, :]
                qf, kf = q_ref[rows, :].astype(F32), k_ref[rows, :].astype(F32)
                qb = (qf * cs + _rot_half(qf) * sn).astype(BF16)
                kr = (kf * cs + _rot_half(kf) * sn) * scale
                extra = None
            else:
                qb, kr = q_s[rows, :], k_s[rows, :]
                extra = (acc[rows, :], g_ref[rows, :])
            loaded.append((qb, kr, v_ref[rows, :], extra))
        raw = [_dot_nt(qb, kr.astype(BF16)) for qb, kr, _, _ in loaded]
        upd = [_dot_tn((kr * direction[3]).astype(BF16), vb)
               for (_, direction), (_, kr, vb, _) in zip(visits, loaded)]
        state = {id(fwd): st_f[...], id(bwd): st_b[...]}
        before = []
        for (_, direction), u_ in zip(visits, upd):
            before.append(state[id(direction)])
            state[id(direction)] = state[id(direction)] * direction[4] + u_
        cross = [jnp.dot(qb, s0.astype(BF16), preferred_element_type=F32)
                 for (qb, _, _, _), s0 in zip(loaded, before)]
        outs = []
        for (r0, direction), (qb, kr, vb, extra), s_raw, cr in zip(visits, loaded, raw, cross):
            _, intra, qd, kd, cd = direction
            o = jnp.dot((s_raw * intra).astype(BF16), vb, preferred_element_type=F32) + cr * qd
            if not first_visit:
                o = o + extra[0]
                mu = jnp.mean(o, axis=-1, keepdims=True)
                oc = o - mu
                var = jnp.mean(oc * oc, axis=-1, keepdims=True)
                o = (_silu(extra[1].astype(F32)) * (oc * lax.rsqrt(var + EPS) * nw)).astype(o_ref.dtype)
            outs.append(o)
        for (r0, _), (qb, kr, _, _), o in zip(visits, loaded, outs):
            rows = pl.ds(r0, ch)
            if first_visit:
                q_s[rows, :] = qb
                k_s[rows, :] = kr
                acc[rows, :] = o
            else:
                o_ref[rows, :] = o
        st_f[...] = state[id(fwd)]
        st_b[...] = state[id(bwd)]

    n_blocks = (nc // 2) // span

    def first_half(b0, carry):
        block(b0, True)
        return carry

    def second_half(b0, carry):
        block(b0, False)
        return carry

    lax.fori_loop(0, n_blocks, first_half, 0)
    lax.fori_loop(n_blocks, 2 * n_blocks, second_half, 0)


def _retention(p, pc, rope_c, rope_s, decay_fwd, decay_bwd, ret_norm_w):
    b, l, _ = p.shape
    lc = pc.shape[1]
    nc = l // RET_CHUNK
    z = jnp.broadcast_to(jnp.stack([decay_fwd, decay_bwd])[:, :, None, None], (2, RET_HEADS, 1, LANES))
    kq0, kk0 = OFF["qr"] // RET_DK, OFF["kr"] // RET_DK
    kv0, kg0 = OFF["vr"] // RET_DV, OFF["gr"] // RET_DV
    full = lambda bi, h: (0, 0)
    return pl.pallas_call(
        functools.partial(_ret_kernel, nc=nc, lc=lc),
        grid=(b, RET_HEADS),
        in_specs=[pl.BlockSpec((None, l, RET_DK), lambda bi, h: (bi, 0, kq0 + h)),
                  pl.BlockSpec((None, l, RET_DK), lambda bi, h: (bi, 0, kk0 + h)),
                  pl.BlockSpec((None, l, RET_DV), lambda bi, h: (bi, 0, kv0 + h)),
                  pl.BlockSpec((None, l, RET_DV), lambda bi, h: (bi, 0, kg0 + h)),
                  pl.BlockSpec((None, lc, RET_DK), lambda bi, h: (bi, 0, kk0 + h)),
                  pl.BlockSpec((None, lc, RET_DV), lambda bi, h: (bi, 0, kv0 + h)),
                  pl.BlockSpec((l, RET_DK), full),
                  pl.BlockSpec((l, RET_DK), full),
                  pl.BlockSpec((2, None, 1, LANES), lambda bi, h: (0, h, 0, 0)),
                  pl.BlockSpec((1, RET_DV), lambda bi, h: (0, h))],
        out_specs=pl.BlockSpec((None, l, RET_DV), lambda bi, h: (bi, 0, h)),
        out_shape=jax.ShapeDtypeStruct((b, l, RET_V_W), BF16),
        scratch_shapes=[pltpu.VMEM((l, RET_DK), BF16), pltpu.VMEM((l, RET_DK), F32), pltpu.VMEM((l, RET_DV), F32),
                        pltpu.VMEM((RET_DK, RET_DV), F32), pltpu.VMEM((RET_DK, RET_DV), F32)],
        compiler_params=_cparams("parallel", "parallel"),
        name="retention",
    )(p, p, p, p, pc, pc, rope_c, rope_s, z, ret_norm_w.reshape(1, RET_V_W))


def _merge_kernel(attn_ref, ret_ref, *refs, n_gate_blocks):
    gate_blocks = refs[:2 * n_gate_blocks]
    x_ref, gate_ref, shift_ref, scale_ref, nw_ref, wa_ref, wr_ref, wm_ref, x1_ref, h2_ref = refs[2 * n_gate_blocks:]
    ga = jnp.concatenate([g[...] for g in gate_blocks[:n_gate_blocks]], axis=1)
    gr = jnp.concatenate([g[...] for g in gate_blocks[n_gate_blocks:]], axis=1)
    a = jnp.dot(attn_ref[...], wa_ref[...], preferred_element_type=F32)
    r = jnp.dot(ret_ref[...], wr_ref[...], preferred_element_type=F32)
    mixed = _sigmoid(ga.astype(F32)) * a + _sigmoid(gr.astype(F32)) * r
    y = jnp.dot(mixed.astype(BF16), wm_ref[...], preferred_element_type=F32)
    x1 = x_ref[...] + gate_ref[...] * y
    x1_ref[...] = x1
    hn = x1 * lax.rsqrt(jnp.mean(x1 * x1, axis=-1, keepdims=True) + EPS) * nw_ref[...]
    h2_ref[...] = (hn * (1.0 + scale_ref[...]) + shift_ref[...]).astype(BF16)


def _merge(attn, ret, p, x, gate, shift2, scale2, norm2_w, wa, wr, wm, tm):
    b, l, d = x.shape
    assert d == D_MODEL
    n_gate_blocks = d // GATE_BLOCK
    gate_block0 = OFF["ga"] // GATE_BLOCK
    assert OFF["ga"] % GATE_BLOCK == 0 and OFF["gt"] == OFF["ga"] + d
    row = lambda bi, i: (bi, i, 0)
    mod = lambda bi, i: (bi, 0, 0)
    const = lambda bi, i: (0, 0)
    once = pl.Buffered(1)
    gate_specs = [pl.BlockSpec((None, tm, GATE_BLOCK), functools.partial(lambda kb, bi, i: (bi, i, kb), gate_block0 + kb))
                  for kb in range(2 * n_gate_blocks)]
    return pl.pallas_call(
        functools.partial(_merge_kernel, n_gate_blocks=n_gate_blocks),
        grid=(b, l // tm),
        in_specs=[pl.BlockSpec((None, tm, ATTN_Q_W), row),
                  pl.BlockSpec((None, tm, RET_V_W), row),
                  *gate_specs,
                  pl.BlockSpec((None, tm, d), row),
                  pl.BlockSpec((None, 1, d), mod),
                  pl.BlockSpec((None, 1, d), mod),
                  pl.BlockSpec((None, 1, d), mod),
                  pl.BlockSpec((1, d), const),
                  pl.BlockSpec(wa.shape, const, pipeline_mode=once),
                  pl.BlockSpec(wr.shape, const, pipeline_mode=once),
                  pl.BlockSpec(wm.shape, const, pipeline_mode=once)],
        out_specs=[pl.BlockSpec((None, tm, d), row), pl.BlockSpec((None, tm, d), row)],
        out_shape=[jax.ShapeDtypeStruct((b, l, d), F32), jax.ShapeDtypeStruct((b, l, d), BF16)],
        compiler_params=_cparams("parallel", "parallel"),
        name="merge",
    )(attn, ret, *([p] * (2 * n_gate_blocks)), x, gate, shift2, scale2, norm2_w.reshape(1, d), wa, wr, wm)


def _run(gen):
    try:
        while True:
            next(gen)
    except StopIteration as stop:
        return stop.value


def _top_rows_gen(s, k):
    n = s.shape[0]
    rid = lax.broadcasted_iota(I32, s.shape, 0)
    vals, ids = [], []
    for _ in range(k):
        m = jnp.max(s, axis=0, keepdims=True)
        sel = jnp.min(jnp.where(s == m, rid, n), axis=0, keepdims=True)
        vals.append(m)
        ids.append(sel)
        s = jnp.where(rid == sel, -jnp.inf, s)
        yield
    return jnp.concatenate(vals, axis=0), jnp.concatenate(ids, axis=0)


def _top_rows(s, k):
    return _run(_top_rows_gen(s, k))


def _pair_candidates(s1, s2):
    k = PEER_TOPK
    t = s1.shape[1]
    r8 = lax.broadcasted_iota(I32, (SUBLANES, t), 0)
    r16 = lax.broadcasted_iota(I32, (k, t), 0)
    neg = -jnp.inf
    sums = [s1[0:1] + s2]
    flat = [r16]
    for a, nb in ((1, 8), (2, 5), (3, 4)):
        sums.append(jnp.where(r8 < nb, s1[a:a + 1] + s2[0:8], neg))
        flat.append(a * k + r8)
    sums.append(s1[8:16] + s2[0:1])
    flat.append((r8 + 8) * k)
    for b_, lo, hi in ((0, 4, 8), (1, 4, 8), (2, 4, 5)):
        sums.append(jnp.where((r8 >= lo) & (r8 < hi), s1[0:8] + s2[b_:b_ + 1], neg))
        flat.append(r8 * k + b_)
    return jnp.concatenate(sums, axis=0), jnp.concatenate(flat, axis=0)


def _pair_topk_gen(s1, n1, s2, n2):
    k = PEER_TOPK
    cand, flat = _pair_candidates(s1, s2)
    tops, picks = [], []
    for _ in range(k):
        m = jnp.max(cand, axis=0, keepdims=True)
        f = jnp.min(jnp.where(cand == m, flat, k * k), axis=0, keepdims=True)
        cand = jnp.where(flat == f, -jnp.inf, cand)
        tops.append(m)
        picks.append(f)
        yield
    top = jnp.concatenate(tops, axis=0)
    pick = jnp.concatenate(picks, axis=0)
    pa = pick // k
    pb = pick - pa * k
    e1 = jnp.zeros_like(pick)
    e2 = jnp.zeros_like(pick)
    for a in range(k):
        e1 = jnp.where(pa == a, n1[a:a + 1], e1)
        e2 = jnp.where(pb == a, n2[a:a + 1], e2)
    ex = jnp.exp(top - top[0:1])
    return e1, e2, ex / jnp.sum(ex, axis=0, keepdims=True)


def _pair_topk(s1, n1, s2, n2):
    return _run(_pair_topk_gen(s1, n1, s2, n2))


def _peer_q_kernel(h_ref, wq_ref, q_ref):
    q_ref[...] = jnp.dot(h_ref[...], wq_ref[...], preferred_element_type=F32).astype(q_ref.dtype)


def _peer_q(h2, wq, tm):
    n, d = h2.shape
    return pl.pallas_call(
        _peer_q_kernel,
        grid=(n // tm,),
        in_specs=[pl.BlockSpec((tm, d), lambda i: (i, 0)),
                  pl.BlockSpec(wq.shape, lambda i: (0, 0), pipeline_mode=pl.Buffered(1))],
        out_specs=pl.BlockSpec((tm, wq.shape[1]), lambda i: (i, 0)),
        out_shape=jax.ShapeDtypeStruct((n, wq.shape[1]), BF16),
        compiler_params=_cparams("parallel"),
        name="peer_q",
    )(h2, wq)


def _peer_topk_kernel(q_s, keys_ref, i1_ref, i2_ref, g_ref, i1_s, i2_s, g_s):
    k = PEER_TOPK

    def head(h, carry):
        c0 = pl.multiple_of(h * 2 * PEER_D_HALF, 2 * PEER_D_HALF)
        qh = q_s[:, pl.ds(c0, 2 * PEER_D_HALF)]
        s1, n1 = _top_rows(_dot_nt(keys_ref[2 * h], qh[:, :PEER_D_HALF]), k)
        s2, n2 = _top_rows(_dot_nt(keys_ref[2 * h + 1], qh[:, PEER_D_HALF:]), k)
        e1, e2, gates = _pair_topk(s1, n1, s2, n2)
        r0 = pl.multiple_of(h * k, k)
        g_s[pl.ds(r0, k), :] = gates
        i1_s[pl.ds(r0, k), :] = e1
        i2_s[pl.ds(r0, k), :] = e2
        return carry

    lax.fori_loop(0, PEER_HEADS, head, 0)
    i1_ref[...] = i1_s[...].T
    i2_ref[...] = i2_s[...].T
    g_ref[...] = g_s[...].T


def _peer_topk(q, keys, n, tm):
    slots = PEER_HEADS * PEER_TOPK
    row = lambda i: (i, 0)
    out = jax.ShapeDtypeStruct
    return pl.pallas_call(
        _peer_topk_kernel,
        grid=(n // tm,),
        in_specs=[pl.BlockSpec((tm, q.shape[1]), row),
                  pl.BlockSpec(keys.shape, lambda i: (0, 0, 0))],
        out_specs=[pl.BlockSpec((tm, slots), row)] * 3,
        out_shape=[out((n, slots), I32), out((n, slots), I32), out((n, slots), F32)],
        scratch_shapes=[pltpu.VMEM((slots, tm), I32), pltpu.VMEM((slots, tm), I32), pltpu.VMEM((slots, tm), F32)],
        compiler_params=_cparams("parallel"),
        name="peer_topk",
    )(q, keys)


def _peer_w_kernel(i1_ref, i2_ref, g_ref, w_ref, w3):
    nk = PEER_N_KEYS
    tb = i1_ref.shape[0]
    grp = 2 * SUBLANES
    rid = lax.broadcasted_iota(I32, (nk, i1_ref.shape[1]), 0)
    sub = lax.broadcasted_iota(I32, (SUBLANES, nk), 0)

    def rows_to_tokens(v):
        v = list(v)
        for dist in (4, 2, 1):
            keep = (sub & dist) == 0
            for k in range(SUBLANES):
                if k & dist == 0:
                    a, b_ = v[k], v[k + dist]
                    v[k] = jnp.where(keep, a, pltpu.roll(b_, dist, 0))
                    v[k + dist] = jnp.where(keep, pltpu.roll(a, SUBLANES - dist, 0), b_)
        return v

    def group(gi, carry):
        t0 = pl.multiple_of(gi * grp, grp)
        i1 = i1_ref[pl.ds(t0, grp), :]
        i2 = i2_ref[pl.ds(t0, grp), :]
        g = g_ref[pl.ds(t0, grp), :]
        for k in range(grp):
            left = jnp.where(rid == i1[k:k + 1], g[k:k + 1], 0.0).astype(BF16)
            right = jnp.where(rid == i2[k:k + 1], 1.0, 0.0).astype(BF16)
            w3[k * nk:(k + 1) * nk, :] = _dot_nt(left, right)
        for j in range(nk // SUBLANES):
            halves = []
            for half in range(grp // SUBLANES):
                tiles = [w3[(half * SUBLANES + k) * nk + j * SUBLANES:(half * SUBLANES + k) * nk + (j + 1) * SUBLANES, :]
                         for k in range(SUBLANES)]
                halves.append(rows_to_tokens(tiles))
            for i in range(SUBLANES):
                r = j * SUBLANES + i
                tile = jnp.concatenate([h[i] for h in halves], axis=0)
                blk, col = divmod(r * nk, w_ref.shape[2])
                w_ref[blk, pl.ds(t0, grp), col:col + nk] = tile.astype(w_ref.dtype)
        return carry

    lax.fori_loop(0, tb // grp, group, 0, unroll=4)


def _peer_w(i1, i2, g, tb, te):
    n, slots = i1.shape
    ne = PEER_N_KEYS * PEER_N_KEYS
    row = lambda i: (i, 0)
    return pl.pallas_call(
        _peer_w_kernel,
        grid=(n // tb,),
        in_specs=[pl.BlockSpec((tb, slots), row)] * 3,
        out_specs=pl.BlockSpec((ne // te, tb, te), lambda i: (0, i, 0)),
        out_shape=jax.ShapeDtypeStruct((ne // te, n, te), BF16),
        scratch_shapes=[pltpu.VMEM((2 * SUBLANES * PEER_N_KEYS, PEER_N_KEYS), F32)],
        compiler_params=_cparams("parallel"),
        name="peer_w",
    )(i1, i2, g)


def _gelu_tanh(x):
    return 0.5 * x * (1.0 + jnp.tanh(0.7978845608028654 * (x + 0.044715 * (x * x * x))))


def _peer_dense_kernel(*refs, retrieve_next):
    if retrieve_next:
        (h_ref, u_ref, v_ref, w_ref, x_ref, gate_ref, nw_ref, qn_ref, keys_ref,
         o_ref, i1_ref, i2_ref, g_ref, acc, a_buf, s1_s, n1_s, i1_s, i2_s, g_s) = refs
    else:
        h_ref, u_ref, v_ref, w_ref, x_ref, gate_ref, nw_ref, o_ref, acc, a_buf = refs
    j = pl.program_id(1)
    nj = pl.num_programs(1) - 1
    k = PEER_TOPK

    def score(slot):
        a_buf[slot] = jnp.dot(h_ref[...], u_ref[...], preferred_element_type=F32)

    def finish(slot):
        act = (_gelu_tanh(a_buf[slot]) * w_ref[...].astype(F32)).astype(BF16)
        acc[...] += jnp.dot(act, v_ref[...], preferred_element_type=F32)

    def retrieval_chains(half):
        if not retrieve_next:
            return [], 0
        r0 = pl.multiple_of((j // 2) * k, k)
        groups = range(0, qn_ref.shape[0], LANES)

        def chain(t0):
            tl = slice(t0, t0 + LANES)
            s, n = yield from _top_rows_gen(_dot_nt(keys_ref[...], qn_ref[tl, :]), k)
            if half == 0:
                s1_s[:, tl] = s
                n1_s[:, tl] = n
            else:
                e1, e2, gates = yield from _pair_topk_gen(s1_s[:, tl], n1_s[:, tl], s, n)
                i1_s[pl.ds(r0, k), tl] = e1
                i2_s[pl.ds(r0, k), tl] = e2
                g_s[pl.ds(r0, k), tl] = gates

        rounds = (k if half == 0 else 2 * k) + 1
        return [chain(t0) for t0 in groups], rounds

    def interleave(main, filler):
        chains, _ = filler
        roomy = [idx for idx, (_, weight) in enumerate(main) if weight >= 1.0]
        start = {}
        for ci, chain in enumerate(chains):
            start.setdefault(roomy[ci * len(roomy) // len(chains)], []).append(chain)
        for idx, (piece, _) in enumerate(main):
            for chain in start.get(idx, ()):
                _run(chain)
            piece()

    def score_pieces(slot):
        half_n = a_buf.shape[2] // 2

        def piece(c0):
            a_buf[slot, :, c0:c0 + half_n] = _dot_nt(h_ref[...], u_ref[c0:c0 + half_n, :])
        return [(functools.partial(piece, c0), 1.0) for c0 in (0, half_n)]

    def finish_pieces(slot):
        quarter = acc.shape[1] // 4
        cell = {}

        def piece(c0):
            if "act" not in cell:
                cell["act"] = (_gelu_tanh(a_buf[slot]) * w_ref[...].astype(F32)).astype(BF16)
            acc[:, c0:c0 + quarter] += jnp.dot(cell["act"], v_ref[:, c0:c0 + quarter], preferred_element_type=F32)
        return [(functools.partial(piece, c0), 0.25 if c0 == 0 else 1.0) for c0 in range(0, acc.shape[1], quarter)]

    @pl.when(j == 0)
    def _():
        acc[...] = jnp.zeros_like(acc)
        interleave(score_pieces(0), retrieval_chains(0))

    for parity in range(2):
        @pl.when((j > 0) & (j < nj) & (j % 2 == parity))
        def _():
            interleave(score_pieces(parity) + finish_pieces(1 - parity), retrieval_chains(parity))

    @pl.when(j == nj)
    def _():
        finish((nj - 1) % 2)
        x2 = x_ref[...] + gate_ref[...] * acc[...]
        o_ref[...] = x2 * lax.rsqrt(jnp.mean(x2 * x2, axis=-1, keepdims=True) + EPS) * nw_ref[...]
        if retrieve_next:
            i1_ref[...] = i1_s[...].T
            i2_ref[...] = i2_s[...].T
            g_ref[...] = g_s[...].T


def _peer_dense(h2, u, v, w, x1, gate, norm_f_w, seq, tm, te, tile0, n_tiles, retrieval=None):
    n, d = h2.shape
    ne = u.shape[0]
    nj = ne // te
    assert nj % 2 == 0
    tiles_per_batch = seq // tm
    row = lambda i, j: (tile0 + i, 0)
    scored = lambda i, j: (jnp.minimum(j, nj - 1), 0)
    finished = lambda j: jnp.maximum(j - 1, 0)
    in_specs = [pl.BlockSpec((tm, d), row),
                pl.BlockSpec((te, d), scored),
                pl.BlockSpec((te, d), lambda i, j: (finished(j), 0)),
                pl.BlockSpec((None, tm, te), lambda i, j: (finished(j), i, 0)),
                pl.BlockSpec((tm, d), row),
                pl.BlockSpec((None, 1, d), lambda i, j: ((tile0 + i) // tiles_per_batch, 0, 0)),
                pl.BlockSpec((1, d), lambda i, j: (0, 0))]
    args = [h2, u, v, w, x1, gate, norm_f_w.reshape(1, d)]
    out_specs = [pl.BlockSpec((tm, d), row)]
    out_shape = [jax.ShapeDtypeStruct((n, d), F32)]
    scratch = [pltpu.VMEM((tm, d), F32), pltpu.VMEM((2, tm, te), F32)]
    if retrieval is not None:
        q, keys = retrieval
        slots = PEER_HEADS * PEER_TOPK
        assert nj == keys.shape[0]
        half_head = lambda j: jnp.minimum(j, nj - 1)
        in_specs += [pl.BlockSpec((tm, PEER_D_HALF), lambda i, j: (tile0 + n_tiles + i, half_head(j))),
                     pl.BlockSpec((None, PEER_N_KEYS, PEER_D_HALF), lambda i, j: (half_head(j), 0, 0))]
        args += [q, keys]
        out_specs += [pl.BlockSpec((tm, slots), lambda i, j: (i, 0))] * 3
        out_shape += [jax.ShapeDtypeStruct((n_tiles * tm, slots), t) for t in (I32, I32, F32)]
        scratch += [pltpu.VMEM((PEER_TOPK, tm), F32), pltpu.VMEM((PEER_TOPK, tm), I32),
                    pltpu.VMEM((slots, tm), I32), pltpu.VMEM((slots, tm), I32), pltpu.VMEM((slots, tm), F32)]
    outs = pl.pallas_call(
        functools.partial(_peer_dense_kernel, retrieve_next=retrieval is not None),
        grid=(n_tiles, nj + 1),
        in_specs=in_specs,
        out_specs=out_specs,
        out_shape=out_shape,
        scratch_shapes=scratch,
        input_output_aliases={4: 0},
        compiler_params=_cparams("parallel", "arbitrary"),
        name="peer_dense",
    )(*args)
    return outs[0], tuple(outs[1:])


def _rope_tables(length):
    t = jnp.arange(length, dtype=jnp.int32)
    row = (t // GRID_W).astype(F32)
    col = (t % GRID_W).astype(F32)
    n_freq = HEAD_DIM // 4
    inv_freq = ROPE_BASE ** (-jnp.arange(n_freq, dtype=F32) / n_freq)
    ang = jnp.concatenate([row[:, None] * inv_freq, col[:, None] * inv_freq], axis=-1)
    cos, sin = jnp.cos(ang), jnp.sin(ang)
    return jnp.concatenate([cos, cos], axis=-1), jnp.concatenate([-sin, sin], axis=-1)


def _layer(x, ctx, c_rows, rope_c, rope_s, ada_w, ada_b, norm1_w, w_in, q_norm_w, k_norm_w, ret_decay_fwd,
           ret_decay_bwd, ret_norm_w, w_attn_branch, w_ret_branch, w_merge_out, norm2_w, peer_w_q, peer_keys,
           peer_u, peer_v, norm_f_w):
    b, l, d = x.shape
    lc = ctx.shape[1]
    mod = _ada(c_rows, ada_w, ada_b)
    mod_x = [mod[:b, i * d:(i + 1) * d].reshape(b, 1, d) for i in range(6)]
    mod_c = [mod[b:b + 1, i * d:(i + 1) * d].reshape(1, 1, d) for i in range(2)]

    w_in_b = w_in.astype(BF16)
    n_in = w_in.shape[1]
    tn = n_in // 4
    p = _in_proj(x, norm1_w, mod_x[0], mod_x[1], w_in_b, tm=min(512, l), tn=tn)
    pc = _in_proj(ctx, norm1_w, mod_c[0], mod_c[1], w_in_b, tm=min(256, lc), tn=tn)

    attn = _attention(p, pc, rope_c, rope_s, q_norm_w, k_norm_w, tq=min(512, l))
    ret = _retention(p, pc, rope_c, rope_s, ret_decay_fwd, ret_decay_bwd, ret_norm_w)
    x1, h2 = _merge(attn, ret, p, x, mod_x[2], mod_x[3], mod_x[4], norm2_w, w_attn_branch.astype(BF16),
                    w_ret_branch.astype(BF16), w_merge_out.astype(BF16), tm=min(256, l))

    n = b * l
    h2f = h2.reshape(n, d)
    keys = peer_keys.reshape(PEER_HEADS * 2, PEER_N_KEYS, PEER_D_HALF).astype(BF16)
    q = _peer_q(h2f, peer_w_q.astype(BF16), tm=min(1024, l))
    u, v = peer_u.astype(BF16), peer_v.astype(BF16)
    tm = min(512, l)
    te = 1024
    n_tiles = n // tm
    n_chunks = next(c for c in (8, 4, 2, 1) if n_tiles % c == 0)
    tiles = n_tiles // n_chunks
    sel = _peer_topk(q, keys, n=tiles * tm, tm=256)
    out = x1.reshape(n, d)
    for chunk in range(n_chunks):
        w = _peer_w(*sel, tb=128, te=te)
        retrieval = (q, keys) if chunk + 1 < n_chunks else None
        out, sel = _peer_dense(h2f, u, v, w, out, mod_x[5], norm_f_w, seq=l, tm=tm, te=te,
                               tile0=chunk * tiles, n_tiles=tiles, retrieval=retrieval)
    return out.reshape(b, l, d)


def kernel(x, c, ctx, c_ctx, ada_w, ada_b, norm1_w, w_in, q_norm_w, k_norm_w, ret_decay_fwd, ret_decay_bwd,
           ret_norm_w, w_attn_branch, w_ret_branch, w_merge_out, norm2_w, peer_w_q, peer_keys, peer_u, peer_v,
           norm_f_w):
    depth = ada_w.shape[0]
    assert depth == 1, "context-stream update between layers is not implemented"
    b, l, d = x.shape
    rows = -(-(b + 1) // SUBLANES) * SUBLANES
    c_rows = jnp.zeros((rows, d), F32).at[:b].set(c).at[b].set(c_ctx)
    rope_c, rope_s = _rope_tables(l)
    return _layer(x, ctx, c_rows, rope_c, rope_s, ada_w[0], ada_b[0], norm1_w[0], w_in[0], q_norm_w[0], k_norm_w[0],
                  ret_decay_fwd[0], ret_decay_bwd[0], ret_norm_w[0], w_attn_branch[0], w_ret_branch[0],
                  w_merge_out[0], norm2_w[0], peer_w_q[0], peer_keys[0], peer_u[0], peer_v[0], norm_f_w)
```

```python
import functools

import jax
import jax.numpy as jnp
from jax import lax
from jax.experimental import pallas as pl
from jax.experimental.pallas import tpu as pltpu

F32 = jnp.float32
BF16 = jnp.bfloat16
I32 = jnp.int32

EPS = 1e-6
GRID_W = 64
ROPE_BASE = 10000.0
ATTN_HEADS = 8
ATTN_KV_HEADS = 2
HEAD_DIM = 128
RET_HEADS = 8
RET_DK = 128
RET_DV = 256
RET_CHUNK = 128
PEER_HEADS = 8
PEER_N_KEYS = 128
PEER_D_HALF = 128
PEER_TOPK = 16

LANES = 128
SUBLANES = 8
VMEM_LIMIT = 60 * 1024 * 1024

ATTN_Q_W = ATTN_HEADS * HEAD_DIM
ATTN_KV_W = ATTN_KV_HEADS * HEAD_DIM
RET_QK_W = RET_HEADS * RET_DK
RET_V_W = RET_HEADS * RET_DV
D_MODEL = 2048
_IN_SPLITS = (("qa", ATTN_Q_W), ("ka", ATTN_KV_W), ("va", ATTN_KV_W), ("qr", RET_QK_W), ("kr", RET_QK_W),
              ("vr", RET_V_W), ("gr", RET_V_W), ("ga", D_MODEL), ("gt", D_MODEL))
OFF = {}
for _name, _width in _IN_SPLITS:
    OFF[_name] = sum(w for n, w in _IN_SPLITS[:len(OFF)])
GATE_BLOCK = 512


def _cparams(*sem):
    return pltpu.CompilerParams(dimension_semantics=sem, vmem_limit_bytes=VMEM_LIMIT)


def _sigmoid(x):
    return 1.0 / (1.0 + jnp.exp(-x))


def _silu(x):
    return x * _sigmoid(x)


def _rot_half(x):
    return pltpu.roll(x, HEAD_DIM // 2, 1)


def _dot_nt(a, b):
    return lax.dot_general(a, b, (((1,), (1,)), ((), ())), preferred_element_type=F32)


def _dot_tn(a, b):
    return lax.dot_general(a, b, (((0,), (0,)), ((), ())), preferred_element_type=F32)


def _ada_kernel(c_ref, w_ref, b_ref, o_ref):
    sc = _silu(c_ref[...]).astype(BF16)
    o_ref[...] = jnp.dot(sc, w_ref[...].astype(BF16), preferred_element_type=F32) + b_ref[...]


def _ada(c_rows, ada_w, ada_b):
    rows, d = c_rows.shape
    n = ada_w.shape[1]
    tn = 1536
    return pl.pallas_call(
        _ada_kernel,
        grid=(n // tn,),
        in_specs=[pl.BlockSpec((rows, d), lambda j: (0, 0)),
                  pl.BlockSpec((d, tn), lambda j: (0, j)),
                  pl.BlockSpec((1, tn), lambda j: (0, j))],
        out_specs=pl.BlockSpec((rows, tn), lambda j: (0, j)),
        out_shape=jax.ShapeDtypeStruct((rows, n), F32),
        compiler_params=_cparams("arbitrary"),
        name="ada",
    )(c_rows, ada_w, ada_b.reshape(1, n))


def _in_proj_kernel(x_ref, xn_ref, nw_ref, shift_ref, scale_ref, shiftn_ref, scalen_ref, w_ref, o_ref, h_ref):
    t, j = pl.program_id(0), pl.program_id(1)
    nj = pl.num_programs(1)
    slot = t % 2

    def prepare(x, shift, scale, dst):
        y = x * lax.rsqrt(jnp.mean(x * x, axis=-1, keepdims=True) + EPS) * nw_ref[...]
        h_ref[dst] = (y * (1.0 + scale) + shift).astype(BF16)

    def project():
        o_ref[...] = jnp.dot(h_ref[slot], w_ref[...], preferred_element_type=F32).astype(o_ref.dtype)

    @pl.when((t == 0) & (j == 0))
    def _():
        prepare(x_ref[...], shift_ref[...], scale_ref[...], 0)

    @pl.when(j < nj - 1)
    def _():
        project()

    @pl.when(j == nj - 1)
    def _():
        prepare(xn_ref[...], shiftn_ref[...], scalen_ref[...], 1 - slot)
        project()


def _in_proj(x, norm_w, shift, scale, w, tm, tn):
    b, l, d = x.shape
    n = w.shape[1]
    per_batch = shift.shape[0] > 1
    tiles_per_batch = l // tm
    n_tiles = b * tiles_per_batch
    nxt = lambda t: jnp.minimum(t + 1, n_tiles - 1)
    row = lambda t, j: (0, 0)
    row_next = lambda t, j: (nxt(t), 0)
    mod = lambda t, j: (0, 0, 0)
    mod_next = (lambda t, j: (nxt(t) // tiles_per_batch, 0, 0)) if per_batch else (lambda t, j: (0, 0, 0))
    x2 = x.reshape(b * l, d)
    out = pl.pallas_call(
        _in_proj_kernel,
        grid=(n_tiles, n // tn),
        in_specs=[pl.BlockSpec((tm, d), row),
                  pl.BlockSpec((tm, d), row_next),
                  pl.BlockSpec((1, d), lambda t, j: (0, 0)),
                  pl.BlockSpec((None, 1, d), mod),
                  pl.BlockSpec((None, 1, d), mod),
                  pl.BlockSpec((None, 1, d), mod_next),
                  pl.BlockSpec((None, 1, d), mod_next),
                  pl.BlockSpec((d, tn), lambda t, j: (0, j))],
        out_specs=pl.BlockSpec((tm, tn), lambda t, j: (t, j)),
        out_shape=jax.ShapeDtypeStruct((b * l, n), BF16),
        scratch_shapes=[pltpu.VMEM((2, tm, d), BF16)],
        compiler_params=_cparams("arbitrary", "arbitrary"),
        name="in_proj",
    )(x2, x2, norm_w.reshape(1, d), shift, scale, shift, scale, w)
    return out.reshape(b, l, n)


def _head_rms(x, w):
    return x * lax.rsqrt(jnp.mean(x * x, axis=-1, keepdims=True) + EPS) * w


def _attn_kernel(q_ref, kc_ref, k_ref, vc_ref, v_ref, cq_ref, sq_ref, ck_ref, sk_ref, qnw_ref, knw_ref,
                 o_ref, k_s, v_s, *, lc, group, tq_sub):
    n_kv = k_s.shape[0]

    @pl.when(pl.program_id(1) == 0)
    def _():
        knw = knw_ref[...]
        for kv in range(n_kv):
            hs = slice(kv * HEAD_DIM, (kv + 1) * HEAD_DIM)
            k_s[kv, 0:lc, :] = _head_rms(kc_ref[:, hs].astype(F32), knw).astype(BF16)
            kn = _head_rms(k_ref[:, hs].astype(F32), knw)
            k_s[kv, lc:, :] = (kn * ck_ref[...] + _rot_half(kn) * sk_ref[...]).astype(BF16)
            v_s[kv, 0:lc, 0:HEAD_DIM] = vc_ref[:, hs]
            v_s[kv, lc:, 0:HEAD_DIM] = v_ref[:, hs]
            v_s[kv, :, HEAD_DIM:] = jnp.ones((v_s.shape[1], HEAD_DIM), BF16)

    c = (HEAD_DIM ** -0.5) * 1.4426950408889634
    qnw = qnw_ref[...]
    tq = tq_sub

    def scores(unit):
        r0, g = unit
        rows = slice(r0, r0 + tq)
        qn = _head_rms(q_ref[rows, g * HEAD_DIM:(g + 1) * HEAD_DIM].astype(F32), qnw)
        qr = (qn * (cq_ref[rows, :] * c) + _rot_half(qn) * (sq_ref[rows, :] * c)).astype(BF16)
        return _dot_nt(qr, k_s[g // group])

    def output(unit, s):
        r0, g = unit
        m = jnp.max(s, axis=-1, keepdims=True)
        p = jnp.exp2((s - m).astype(BF16))
        ol = jnp.dot(p, v_s[g // group], preferred_element_type=F32)
        o_ref[r0:r0 + tq, g * HEAD_DIM:(g + 1) * HEAD_DIM] = (
            ol[:, :HEAD_DIM] * (1.0 / ol[:, HEAD_DIM:HEAD_DIM + 1])).astype(o_ref.dtype)

    units = [(r0, g) for r0 in range(0, q_ref.shape[0], tq) for g in range(n_kv * group)]
    s_next = scores(units[0])
    for idx, unit in enumerate(units):
        s = s_next
        if idx + 1 < len(units):
            s_next = scores(units[idx + 1])
        output(unit, s)


def _attention(p, pc, rope_c, rope_s, q_norm_w, k_norm_w, tq):
    b, l, _ = p.shape
    lc = pc.shape[1]
    group = ATTN_HEADS // ATTN_KV_HEADS
    kq0 = OFF["qa"] // ATTN_Q_W
    kk0 = OFF["ka"] // ATTN_KV_W
    kv0 = OFF["va"] // ATTN_KV_W
    full = lambda bi, i: (0, 0)
    return pl.pallas_call(
        functools.partial(_attn_kernel, lc=lc, group=group, tq_sub=min(256, tq)),
        grid=(b, l // tq),
        in_specs=[pl.BlockSpec((None, tq, ATTN_Q_W), lambda bi, i: (bi, i, kq0)),
                  pl.BlockSpec((None, lc, ATTN_KV_W), lambda bi, i: (bi, 0, kk0)),
                  pl.BlockSpec((None, l, ATTN_KV_W), lambda bi, i: (bi, 0, kk0)),
                  pl.BlockSpec((None, lc, ATTN_KV_W), lambda bi, i: (bi, 0, kv0)),
                  pl.BlockSpec((None, l, ATTN_KV_W), lambda bi, i: (bi, 0, kv0)),
                  pl.BlockSpec((tq, HEAD_DIM), lambda bi, i: (i, 0)),
                  pl.BlockSpec((tq, HEAD_DIM), lambda bi, i: (i, 0)),
                  pl.BlockSpec((l, HEAD_DIM), full),
                  pl.BlockSpec((l, HEAD_DIM), full),
                  pl.BlockSpec((1, HEAD_DIM), full),
                  pl.BlockSpec((1, HEAD_DIM), full)],
        out_specs=pl.BlockSpec((None, tq, ATTN_Q_W), lambda bi, i: (bi, i, 0)),
        out_shape=jax.ShapeDtypeStruct((b, l, ATTN_Q_W), BF16),
        scratch_shapes=[pltpu.VMEM((ATTN_KV_HEADS, lc + l, HEAD_DIM), BF16),
                        pltpu.VMEM((ATTN_KV_HEADS, lc + l, 2 * HEAD_DIM), BF16)],
        compiler_params=_cparams("parallel", "arbitrary"),
        name="attention",
    )(p, pc, p, pc, p, rope_c, rope_s, rope_c, rope_s, q_norm_w.reshape(1, HEAD_DIM), k_norm_w.reshape(1, HEAD_DIM))


def _ret_kernel(q_ref, k_ref, v_ref, g_ref, kc_ref, vc_ref, cos_ref, sin_ref, z_ref, nw_ref, o_ref,
                q_s, k_s, acc, st_f, st_b, *, nc, lc):
    ch = RET_CHUNK
    assert nc % 2 == 0
    scale = RET_DK ** -0.5
    lg_f = -jnp.exp(z_ref[0])
    lg_b = -jnp.exp(z_ref[1])
    row = lax.broadcasted_iota(I32, (ch, ch), 0).astype(F32)
    col = lax.broadcasted_iota(I32, (ch, ch), 1).astype(F32)
    d = row - col
    intra_f = jnp.where(d >= 0, jnp.exp(lg_f * jnp.maximum(d, 0.0)), 0.0)
    intra_b = jnp.where(d <= 0, jnp.exp(lg_b * jnp.maximum(-d, 0.0)), 0.0)

    def wide(a):
        return jnp.concatenate([a] * (RET_DV // LANES), axis=1)

    qd_f = wide(jnp.exp(lg_f * (row + 1.0)))
    kd_f = jnp.exp(lg_f * (ch - 1.0 - row))
    cd_f = wide(jnp.exp(lg_f * float(ch)))
    qd_b = wide(jnp.exp(lg_b * (ch - row)))
    kd_b = jnp.exp(lg_b * row)
    cd_b = wide(jnp.exp(lg_b * float(ch)))

    j = lax.broadcasted_iota(I32, (lc, RET_DK), 0).astype(F32)
    kc = kc_ref[...].astype(F32) * scale
    vc = vc_ref[...]
    st_f[...] = _dot_tn((kc * jnp.exp(lg_f * (lc - 1.0 - j))).astype(BF16), vc)
    st_b[...] = _dot_tn((kc * jnp.exp(lg_b * j)).astype(BF16), vc)

    nw = nw_ref[...]
    fwd = (st_f, intra_f, qd_f, kd_f, cd_f)
    bwd = (st_b, intra_b, qd_b, kd_b, cd_b)

    span = 4 if (nc // 2) % 4 == 0 else (2 if (nc // 2) % 2 == 0 else 1)

    def block(b0, first_visit):
        visits = []
        for u in range(span):
            i = b0 * span + u
            visits.append((pl.multiple_of(i * ch, ch), fwd))
            visits.append((pl.multiple_of((nc - 1 - i) * ch, ch), bwd))
        loaded = []
        for r0, _ in visits:
            rows = pl.ds(r0, ch)
            if first_visit:
                cs, sn = cos_ref[rows, :], sin_ref[rows, :]
                qf, kf = q_ref[rows, :].astype(F32), k_ref[rows, :].astype(F32)
                qb = (qf * cs + _rot_half(qf) * sn).astype(BF16)
                kr = (kf * cs + _rot_half(kf) * sn) * scale
                extra = None
            else:
                qb, kr = q_s[rows, :], k_s[rows, :]
                extra = (acc[rows, :], g_ref[rows, :])
            loaded.append((qb, kr, v_ref[rows, :], extra))
        raw = [_dot_nt(qb, kr.astype(BF16)) for qb, kr, _, _ in loaded]
        upd = [_dot_tn((kr * direction[3]).astype(BF16), vb)
               for (_, direction), (_, kr, vb, _) in zip(visits, loaded)]
        state = {id(fwd): st_f[...], id(bwd): st_b[...]}
        before = []
        for (_, direction), u_ in zip(visits, upd):
            before.append(state[id(direction)])
            state[id(direction)] = state[id(direction)] * direction[4] + u_
        cross = [jnp.dot(qb, s0.astype(BF16), preferred_element_type=F32)
                 for (qb, _, _, _), s0 in zip(loaded, before)]
        outs = []
        for (r0, direction), (qb, kr, vb, extra), s_raw, cr in zip(visits, loaded, raw, cross):
            _, intra, qd, kd, cd = direction
            o = jnp.dot((s_raw * intra).astype(BF16), vb, preferred_element_type=F32) + cr * qd
            if not first_visit:
                o = o + extra[0]
                mu = jnp.mean(o, axis=-1, keepdims=True)
                oc = o - mu
                var = jnp.mean(oc * oc, axis=-1, keepdims=True)
                o = (_silu(extra[1].astype(F32)) * (oc * lax.rsqrt(var + EPS) * nw)).astype(o_ref.dtype)
            outs.append(o)
        for (r0, _), (qb, kr, _, _), o in zip(visits, loaded, outs):
            rows = pl.ds(r0, ch)
            if first_visit:
                q_s[rows, :] = qb
                k_s[rows, :] = kr
                acc[rows, :] = o
            else:
                o_ref[rows, :] = o
        st_f[...] = state[id(fwd)]
        st_b[...] = state[id(bwd)]

    n_blocks = (nc // 2) // span

    def first_half(b0, carry):
        block(b0, True)
        return carry

    def second_half(b0, carry):
        block(b0, False)
        return carry

    lax.fori_loop(0, n_blocks, first_half, 0)
    lax.fori_loop(n_blocks, 2 * n_blocks, second_half, 0)


def _retention(p, pc, rope_c, rope_s, decay_fwd, decay_bwd, ret_norm_w):
    b, l, _ = p.shape
    lc = pc.shape[1]
    nc = l // RET_CHUNK
    z = jnp.broadcast_to(jnp.stack([decay_fwd, decay_bwd])[:, :, None, None], (2, RET_HEADS, 1, LANES))
    kq0, kk0 = OFF["qr"] // RET_DK, OFF["kr"] // RET_DK
    kv0, kg0 = OFF["vr"] // RET_DV, OFF["gr"] // RET_DV
    full = lambda bi, h: (0, 0)
    return pl.pallas_call(
        functools.partial(_ret_kernel, nc=nc, lc=lc),
        grid=(b, RET_HEADS),
        in_specs=[pl.BlockSpec((None, l, RET_DK), lambda bi, h: (bi, 0, kq0 + h)),
                  pl.BlockSpec((None, l, RET_DK), lambda bi, h: (bi, 0, kk0 + h)),
                  pl.BlockSpec((None, l, RET_DV), lambda bi, h: (bi, 0, kv0 + h)),
                  pl.BlockSpec((None, l, RET_DV), lambda bi, h: (bi, 0, kg0 + h)),
                  pl.BlockSpec((None, lc, RET_DK), lambda bi, h: (bi, 0, kk0 + h)),
                  pl.BlockSpec((None, lc, RET_DV), lambda bi, h: (bi, 0, kv0 + h)),
                  pl.BlockSpec((l, RET_DK), full),
                  pl.BlockSpec((l, RET_DK), full),
                  pl.BlockSpec((2, None, 1, LANES), lambda bi, h: (0, h, 0, 0)),
                  pl.BlockSpec((1, RET_DV), lambda bi, h: (0, h))],
        out_specs=pl.BlockSpec((None, l, RET_DV), lambda bi, h: (bi, 0, h)),
        out_shape=jax.ShapeDtypeStruct((b, l, RET_V_W), BF16),
        scratch_shapes=[pltpu.VMEM((l, RET_DK), BF16), pltpu.VMEM((l, RET_DK), F32), pltpu.VMEM((l, RET_DV), F32),
                        pltpu.VMEM((RET_DK, RET_DV), F32), pltpu.VMEM((RET_DK, RET_DV), F32)],
        compiler_params=_cparams("parallel", "parallel"),
        name="retention",
    )(p, p, p, p, pc, pc, rope_c, rope_s, z, ret_norm_w.reshape(1, RET_V_W))


def _merge_kernel(attn_ref, ret_ref, *refs, n_gate_blocks):
    gate_blocks = refs[:2 * n_gate_blocks]
    x_ref, gate_ref, shift_ref, scale_ref, nw_ref, wa_ref, wr_ref, wm_ref, x1_ref, h2_ref = refs[2 * n_gate_blocks:]
    ga = jnp.concatenate([g[...] for g in gate_blocks[:n_gate_blocks]], axis=1)
    gr = jnp.concatenate([g[...] for g in gate_blocks[n_gate_blocks:]], axis=1)
    a = jnp.dot(attn_ref[...], wa_ref[...], preferred_element_type=F32)
    r = jnp.dot(ret_ref[...], wr_ref[...], preferred_element_type=F32)
    mixed = _sigmoid(ga.astype(F32)) * a + _sigmoid(gr.astype(F32)) * r
    y = jnp.dot(mixed.astype(BF16), wm_ref[...], preferred_element_type=F32)
    x1 = x_ref[...] + gate_ref[...] * y
    x1_ref[...] = x1
    hn = x1 * lax.rsqrt(jnp.mean(x1 * x1, axis=-1, keepdims=True) + EPS) * nw_ref[...]
    h2_ref[...] = (hn * (1.0 + scale_ref[...]) + shift_ref[...]).astype(BF16)


def _merge(attn, ret, p, x, gate, shift2, scale2, norm2_w, wa, wr, wm, tm):
    b, l, d = x.shape
    assert d == D_MODEL
    n_gate_blocks = d // GATE_BLOCK
    gate_block0 = OFF["ga"] // GATE_BLOCK
    assert OFF["ga"] % GATE_BLOCK == 0 and OFF["gt"] == OFF["ga"] + d
    row = lambda bi, i: (bi, i, 0)
    mod = lambda bi, i: (bi, 0, 0)
    const = lambda bi, i: (0, 0)
    once = pl.Buffered(1)
    gate_specs = [pl.BlockSpec((None, tm, GATE_BLOCK), functools.partial(lambda kb, bi, i: (bi, i, kb), gate_block0 + kb))
                  for kb in range(2 * n_gate_blocks)]
    return pl.pallas_call(
        functools.partial(_merge_kernel, n_gate_blocks=n_gate_blocks),
        grid=(b, l // tm),
        in_specs=[pl.BlockSpec((None, tm, ATTN_Q_W), row),
                  pl.BlockSpec((None, tm, RET_V_W), row),
                  *gate_specs,
                  pl.BlockSpec((None, tm, d), row),
                  pl.BlockSpec((None, 1, d), mod),
                  pl.BlockSpec((None, 1, d), mod),
                  pl.BlockSpec((None, 1, d), mod),
                  pl.BlockSpec((1, d), const),
                  pl.BlockSpec(wa.shape, const, pipeline_mode=once),
                  pl.BlockSpec(wr.shape, const, pipeline_mode=once),
                  pl.BlockSpec(wm.shape, const, pipeline_mode=once)],
        out_specs=[pl.BlockSpec((None, tm, d), row), pl.BlockSpec((None, tm, d), row)],
        out_shape=[jax.ShapeDtypeStruct((b, l, d), F32), jax.ShapeDtypeStruct((b, l, d), BF16)],
        compiler_params=_cparams("parallel", "parallel"),
        name="merge",
    )(attn, ret, *([p] * (2 * n_gate_blocks)), x, gate, shift2, scale2, norm2_w.reshape(1, d), wa, wr, wm)


def _run(gen):
    try:
        while True:
            next(gen)
    except StopIteration as stop:
        return stop.value


def _top_rows_gen(s, k):
    n = s.shape[0]
    rid = lax.broadcasted_iota(I32, s.shape, 0)
    vals, ids = [], []
    for _ in range(k):
        m = jnp.max(s, axis=0, keepdims=True)
        sel = jnp.min(jnp.where(s == m, rid, n), axis=0, keepdims=True)
        vals.append(m)
        ids.append(sel)
        s = jnp.where(rid == sel, -jnp.inf, s)
        yield
    return jnp.concatenate(vals, axis=0), jnp.concatenate(ids, axis=0)


def _top_rows(s, k):
    return _run(_top_rows_gen(s, k))


def _pair_candidates(s1, s2):
    k = PEER_TOPK
    t = s1.shape[1]
    r8 = lax.broadcasted_iota(I32, (SUBLANES, t), 0)
    r16 = lax.broadcasted_iota(I32, (k, t), 0)
    neg = -jnp.inf
    sums = [s1[0:1] + s2]
    flat = [r16]
    for a, nb in ((1, 8), (2, 5), (3, 4)):
        sums.append(jnp.where(r8 < nb, s1[a:a + 1] + s2[0:8], neg))
        flat.append(a * k + r8)
    sums.append(s1[8:16] + s2[0:1])
    flat.append((r8 + 8) * k)
    for b_, lo, hi in ((0, 4, 8), (1, 4, 8), (2, 4, 5)):
        sums.append(jnp.where((r8 >= lo) & (r8 < hi), s1[0:8] + s2[b_:b_ + 1], neg))
        flat.append(r8 * k + b_)
    return jnp.concatenate(sums, axis=0), jnp.concatenate(flat, axis=0)


def _pair_topk_gen(s1, n1, s2, n2):
    k = PEER_TOPK
    cand, flat = _pair_candidates(s1, s2)
    tops, picks = [], []
    for _ in range(k):
        m = jnp.max(cand, axis=0, keepdims=True)
        f = jnp.min(jnp.where(cand == m, flat, k * k), axis=0, keepdims=True)
        cand = jnp.where(flat == f, -jnp.inf, cand)
        tops.append(m)
        picks.append(f)
        yield
    top = jnp.concatenate(tops, axis=0)
    pick = jnp.concatenate(picks, axis=0)
    pa = pick // k
    pb = pick - pa * k
    e1 = jnp.zeros_like(pick)
    e2 = jnp.zeros_like(pick)
    for a in range(k):
        e1 = jnp.where(pa == a, n1[a:a + 1], e1)
        e2 = jnp.where(pb == a, n2[a:a + 1], e2)
    ex = jnp.exp(top - top[0:1])
    return e1, e2, ex / jnp.sum(ex, axis=0, keepdims=True)


def _pair_topk(s1, n1, s2, n2):
    return _run(_pair_topk_gen(s1, n1, s2, n2))


def _peer_q_kernel(h_ref, wq_ref, q_ref):
    q_ref[...] = jnp.dot(h_ref[...], wq_ref[...], preferred_element_type=F32).astype(q_ref.dtype)


def _peer_q(h2, wq, tm):
    n, d = h2.shape
    return pl.pallas_call(
        _peer_q_kernel,
        grid=(n // tm,),
        in_specs=[pl.BlockSpec((tm, d), lambda i: (i, 0)),
                  pl.BlockSpec(wq.shape, lambda i: (0, 0), pipeline_mode=pl.Buffered(1))],
        out_specs=pl.BlockSpec((tm, wq.shape[1]), lambda i: (i, 0)),
        out_shape=jax.ShapeDtypeStruct((n, wq.shape[1]), BF16),
        compiler_params=_cparams("parallel"),
        name="peer_q",
    )(h2, wq)


def _peer_topk_kernel(q_s, keys_ref, i1_ref, i2_ref, g_ref, i1_s, i2_s, g_s):
    k = PEER_TOPK

    def head(h, carry):
        c0 = pl.multiple_of(h * 2 * PEER_D_HALF, 2 * PEER_D_HALF)
        qh = q_s[:, pl.ds(c0, 2 * PEER_D_HALF)]
        s1, n1 = _top_rows(_dot_nt(keys_ref[2 * h], qh[:, :PEER_D_HALF]), k)
        s2, n2 = _top_rows(_dot_nt(keys_ref[2 * h + 1], qh[:, PEER_D_HALF:]), k)
        e1, e2, gates = _pair_topk(s1, n1, s2, n2)
        r0 = pl.multiple_of(h * k, k)
        g_s[pl.ds(r0, k), :] = gates
        i1_s[pl.ds(r0, k), :] = e1
        i2_s[pl.ds(r0, k), :] = e2
        return carry

    lax.fori_loop(0, PEER_HEADS, head, 0)
    i1_ref[...] = i1_s[...].T
    i2_ref[...] = i2_s[...].T
    g_ref[...] = g_s[...].T


def _peer_topk(q, keys, n, tm):
    slots = PEER_HEADS * PEER_TOPK
    row = lambda i: (i, 0)
    out = jax.ShapeDtypeStruct
    return pl.pallas_call(
        _peer_topk_kernel,
        grid=(n // tm,),
        in_specs=[pl.BlockSpec((tm, q.shape[1]), row),
                  pl.BlockSpec(keys.shape, lambda i: (0, 0, 0))],
        out_specs=[pl.BlockSpec((tm, slots), row)] * 3,
        out_shape=[out((n, slots), I32), out((n, slots), I32), out((n, slots), F32)],
        scratch_shapes=[pltpu.VMEM((slots, tm), I32), pltpu.VMEM((slots, tm), I32), pltpu.VMEM((slots, tm), F32)],
        compiler_params=_cparams("parallel"),
        name="peer_topk",
    )(q, keys)


def _peer_w_kernel(i1_ref, i2_ref, g_ref, w_ref, w3):
    nk = PEER_N_KEYS
    tb = i1_ref.shape[0]
    grp = 2 * SUBLANES
    rid = lax.broadcasted_iota(I32, (nk, i1_ref.shape[1]), 0)
    sub = lax.broadcasted_iota(I32, (SUBLANES, nk), 0)

    def rows_to_tokens(v):
        v = list(v)
        for dist in (4, 2, 1):
            keep = (sub & dist) == 0
            for k in range(SUBLANES):
                if k & dist == 0:
                    a, b_ = v[k], v[k + dist]
                    v[k] = jnp.where(keep, a, pltpu.roll(b_, dist, 0))
                    v[k + dist] = jnp.where(keep, pltpu.roll(a, SUBLANES - dist, 0), b_)
        return v

    def group(gi, carry):
        t0 = pl.multiple_of(gi * grp, grp)
        i1 = i1_ref[pl.ds(t0, grp), :]
        i2 = i2_ref[pl.ds(t0, grp), :]
        g = g_ref[pl.ds(t0, grp), :]
        for k in range(grp):
            left = jnp.where(rid == i1[k:k + 1], g[k:k + 1], 0.0).astype(BF16)
            right = jnp.where(rid == i2[k:k + 1], 1.0, 0.0).astype(BF16)
            w3[k * nk:(k + 1) * nk, :] = _dot_nt(left, right)
        for j in range(nk // SUBLANES):
            halves = []
            for half in range(grp // SUBLANES):
                tiles = [w3[(half * SUBLANES + k) * nk + j * SUBLANES:(half * SUBLANES + k) * nk + (j + 1) * SUBLANES, :]
                         for k in range(SUBLANES)]
                halves.append(rows_to_tokens(tiles))
            for i in range(SUBLANES):
                r = j * SUBLANES + i
                tile = jnp.concatenate([h[i] for h in halves], axis=0)
                w_ref[pl.ds(t0, grp), r * nk:(r + 1) * nk] = tile.astype(w_ref.dtype)
        return carry

    lax.fori_loop(0, tb // grp, group, 0, unroll=4)


def _peer_w(i1, i2, g, tb):
    n, slots = i1.shape
    ne = PEER_N_KEYS * PEER_N_KEYS
    row = lambda i: (i, 0)
    return pl.pallas_call(
        _peer_w_kernel,
        grid=(n // tb,),
        in_specs=[pl.BlockSpec((tb, slots), row)] * 3,
        out_specs=pl.BlockSpec((tb, ne), row),
        out_shape=jax.ShapeDtypeStruct((n, ne), BF16),
        scratch_shapes=[pltpu.VMEM((2 * SUBLANES * PEER_N_KEYS, PEER_N_KEYS), F32)],
        compiler_params=_cparams("parallel"),
        name="peer_w",
    )(i1, i2, g)


def _gelu_tanh(x):
    return 0.5 * x * (1.0 + jnp.tanh(0.7978845608028654 * (x + 0.044715 * (x * x * x))))


def _peer_dense_kernel(*refs, retrieve_next):
    if retrieve_next:
        (h_ref, u_ref, v_ref, w_ref, x_ref, gate_ref, nw_ref, qn_ref, keys_ref,
         o_ref, i1_ref, i2_ref, g_ref, acc, a_buf, s1_s, n1_s, i1_s, i2_s, g_s) = refs
    else:
        h_ref, u_ref, v_ref, w_ref, x_ref, gate_ref, nw_ref, o_ref, acc, a_buf = refs
    j = pl.program_id(1)
    nj = pl.num_programs(1) - 1
    k = PEER_TOPK

    def score(slot):
        a_buf[slot] = jnp.dot(h_ref[...], u_ref[...], preferred_element_type=F32)

    def finish(slot):
        act = (_gelu_tanh(a_buf[slot]) * w_ref[...].astype(F32)).astype(BF16)
        acc[...] += jnp.dot(act, v_ref[...], preferred_element_type=F32)

    def retrieval_chains(half):
        if not retrieve_next:
            return [], 0
        r0 = pl.multiple_of((j // 2) * k, k)
        groups = range(0, qn_ref.shape[0], LANES)

        def chain(t0):
            tl = slice(t0, t0 + LANES)
            half_head = jnp.minimum(j, nj - 1)
            cols = pl.ds(pl.multiple_of(half_head * PEER_D_HALF, PEER_D_HALF), PEER_D_HALF)
            s, n = yield from _top_rows_gen(_dot_nt(keys_ref[half_head], qn_ref[tl, cols]), k)
            if half == 0:
                s1_s[:, tl] = s
                n1_s[:, tl] = n
            else:
                e1, e2, gates = yield from _pair_topk_gen(s1_s[:, tl], n1_s[:, tl], s, n)
                i1_s[pl.ds(r0, k), tl] = e1
                i2_s[pl.ds(r0, k), tl] = e2
                g_s[pl.ds(r0, k), tl] = gates

        rounds = (k if half == 0 else 2 * k) + 1
        return [chain(t0) for t0 in groups], rounds

    def interleave(main, filler):
        chains, _ = filler
        roomy = [idx for idx, (_, weight) in enumerate(main) if weight >= 1.0]
        start = {}
        for ci, chain in enumerate(chains):
            start.setdefault(roomy[ci * len(roomy) // len(chains)], []).append(chain)
        for idx, (piece, _) in enumerate(main):
            for chain in start.get(idx, ()):
                _run(chain)
            piece()

    def score_pieces(slot):
        half_n = a_buf.shape[2] // 2

        def piece(c0):
            a_buf[slot, :, c0:c0 + half_n] = _dot_nt(h_ref[...], u_ref[c0:c0 + half_n, :])
        return [(functools.partial(piece, c0), 1.0) for c0 in (0, half_n)]

    def finish_pieces(slot):
        quarter = acc.shape[1] // 4
        cell = {}

        def piece(c0):
            if "act" not in cell:
                cell["act"] = (_gelu_tanh(a_buf[slot]) * w_ref[...].astype(F32)).astype(BF16)
            acc[:, c0:c0 + quarter] += jnp.dot(cell["act"], v_ref[:, c0:c0 + quarter], preferred_element_type=F32)
        return [(functools.partial(piece, c0), 0.25 if c0 == 0 else 1.0) for c0 in range(0, acc.shape[1], quarter)]

    @pl.when(j == 0)
    def _():
        acc[...] = jnp.zeros_like(acc)
        interleave(score_pieces(0), retrieval_chains(0))

    for parity in range(2):
        @pl.when((j > 0) & (j < nj) & (j % 2 == parity))
        def _():
            interleave(score_pieces(parity) + finish_pieces(1 - parity), retrieval_chains(parity))

    @pl.when(j == nj)
    def _():
        finish((nj - 1) % 2)
        x2 = x_ref[...] + gate_ref[...] * acc[...]
        o_ref[...] = x2 * lax.rsqrt(jnp.mean(x2 * x2, axis=-1, keepdims=True) + EPS) * nw_ref[...]
        if retrieve_next:
            i1_ref[...] = i1_s[...].T
            i2_ref[...] = i2_s[...].T
            g_ref[...] = g_s[...].T


def _peer_dense(h2, u, v, w, x1, gate, norm_f_w, seq, tm, te, tile0, n_tiles, retrieval=None):
    n, d = h2.shape
    ne = u.shape[0]
    nj = ne // te
    assert nj % 2 == 0
    tiles_per_batch = seq // tm
    row = lambda i, j: (tile0 + i, 0)
    scored = lambda i, j: (jnp.minimum(j, nj - 1), 0)
    finished = lambda j: jnp.maximum(j - 1, 0)
    in_specs = [pl.BlockSpec((tm, d), row),
                pl.BlockSpec((te, d), scored),
                pl.BlockSpec((te, d), lambda i, j: (finished(j), 0)),
                pl.BlockSpec((tm, te), lambda i, j: (i, finished(j))),
                pl.BlockSpec((tm, d), row),
                pl.BlockSpec((None, 1, d), lambda i, j: ((tile0 + i) // tiles_per_batch, 0, 0)),
                pl.BlockSpec((1, d), lambda i, j: (0, 0))]
    args = [h2, u, v, w, x1, gate, norm_f_w.reshape(1, d)]
    out_specs = [pl.BlockSpec((tm, d), row)]
    out_shape = [jax.ShapeDtypeStruct((n, d), F32)]
    scratch = [pltpu.VMEM((tm, d), F32), pltpu.VMEM((2, tm, te), F32)]
    if retrieval is not None:
        q, keys = retrieval
        slots = PEER_HEADS * PEER_TOPK
        assert nj == keys.shape[0]
        in_specs += [pl.BlockSpec((tm, q.shape[1]), lambda i, j: (tile0 + n_tiles + i, 0)),
                     pl.BlockSpec(keys.shape, lambda i, j: (0, 0, 0))]
        args += [q, keys]
        out_specs += [pl.BlockSpec((tm, slots), lambda i, j: (i, 0))] * 3
        out_shape += [jax.ShapeDtypeStruct((n_tiles * tm, slots), t) for t in (I32, I32, F32)]
        scratch += [pltpu.VMEM((PEER_TOPK, tm), F32), pltpu.VMEM((PEER_TOPK, tm), I32),
                    pltpu.VMEM((slots, tm), I32), pltpu.VMEM((slots, tm), I32), pltpu.VMEM((slots, tm), F32)]
    outs = pl.pallas_call(
        functools.partial(_peer_dense_kernel, retrieve_next=retrieval is not None),
        grid=(n_tiles, nj + 1),
        in_specs=in_specs,
        out_specs=out_specs,
        out_shape=out_shape,
        scratch_shapes=scratch,
        input_output_aliases={4: 0},
        compiler_params=_cparams("parallel", "arbitrary"),
        name="peer_dense",
    )(*args)
    return outs[0], tuple(outs[1:])


def _rope_tables(length):
    t = jnp.arange(length, dtype=jnp.int32)
    row = (t // GRID_W).astype(F32)
    col = (t % GRID_W).astype(F32)
    n_freq = HEAD_DIM // 4
    inv_freq = ROPE_BASE ** (-jnp.arange(n_freq, dtype=F32) / n_freq)
    ang = jnp.concatenate([row[:, None] * inv_freq, col[:, None] * inv_freq], axis=-1)
    cos, sin = jnp.cos(ang), jnp.sin(ang)
    return jnp.concatenate([cos, cos], axis=-1), jnp.concatenate([-sin, sin], axis=-1)


def _layer(x, ctx, c_rows, rope_c, rope_s, ada_w, ada_b, norm1_w, w_in, q_norm_w, k_norm_w, ret_decay_fwd,
           ret_decay_bwd, ret_norm_w, w_attn_branch, w_ret_branch, w_merge_out, norm2_w, peer_w_q, peer_keys,
           peer_u, peer_v, norm_f_w):
    b, l, d = x.shape
    lc = ctx.shape[1]
    mod = _ada(c_rows, ada_w, ada_b)
    mod_x = [mod[:b, i * d:(i + 1) * d].reshape(b, 1, d) for i in range(6)]
    mod_c = [mod[b:b + 1, i * d:(i + 1) * d].reshape(1, 1, d) for i in range(2)]

    w_in_b = w_in.astype(BF16)
    n_in = w_in.shape[1]
    tn = n_in // 4
    p = _in_proj(x, norm1_w, mod_x[0], mod_x[1], w_in_b, tm=min(512, l), tn=tn)
    pc = _in_proj(ctx, norm1_w, mod_c[0], mod_c[1], w_in_b, tm=min(256, lc), tn=tn)

    attn = _attention(p, pc, rope_c, rope_s, q_norm_w, k_norm_w, tq=min(512, l))
    ret = _retention(p, pc, rope_c, rope_s, ret_decay_fwd, ret_decay_bwd, ret_norm_w)
    x1, h2 = _merge(attn, ret, p, x, mod_x[2], mod_x[3], mod_x[4], norm2_w, w_attn_branch.astype(BF16),
                    w_ret_branch.astype(BF16), w_merge_out.astype(BF16), tm=min(256, l))

    n = b * l
    h2f = h2.reshape(n, d)
    keys = peer_keys.reshape(PEER_HEADS * 2, PEER_N_KEYS, PEER_D_HALF).astype(BF16)
    q = _peer_q(h2f, peer_w_q.astype(BF16), tm=min(1024, l))
    u, v = peer_u.astype(BF16), peer_v.astype(BF16)
    tm = min(512, l)
    te = 1024
    n_tiles = n // tm
    n_chunks = next(c for c in (8, 4, 2, 1) if n_tiles % c == 0)
    tiles = n_tiles // n_chunks
    sel = _peer_topk(q, keys, n=tiles * tm, tm=256)
    out = x1.reshape(n, d)
    for chunk in range(n_chunks):
        w = _peer_w(*sel, tb=128)
        retrieval = (q, keys) if chunk + 1 < n_chunks else None
        out, sel = _peer_dense(h2f, u, v, w, out, mod_x[5], norm_f_w, seq=l, tm=tm, te=te,
                               tile0=chunk * tiles, n_tiles=tiles, retrieval=retrieval)
    return out.reshape(b, l, d)


def kernel(x, c, ctx, c_ctx, ada_w, ada_b, norm1_w, w_in, q_norm_w, k_norm_w, ret_decay_fwd, ret_decay_bwd,
           ret_norm_w, w_attn_branch, w_ret_branch, w_merge_out, norm2_w, peer_w_q, peer_keys, peer_u, peer_v,
           norm_f_w):
    depth = ada_w.shape[0]
    assert depth == 1, "context-stream update between layers is not implemented"
    b, l, d = x.shape
    rows = -(-(b + 1) // SUBLANES) * SUBLANES
    c_rows = jnp.zeros((rows, d), F32).at[:b].set(c).at[b].set(c_ctx)
    rope_c, rope_s = _rope_tables(l)
    return _layer(x, ctx, c_rows, rope_c, rope_s, ada_w[0], ada_b[0], norm1_w[0], w_in[0], q_norm_w[0], k_norm_w[0],
                  ret_decay_fwd[0], ret_decay_bwd[0], ret_norm_w[0], w_attn_branch[0], w_ret_branch[0],
                  w_merge_out[0], norm2_w[0], peer_w_q[0], peer_keys[0], peer_u[0], peer_v[0], norm_f_w)
```

```python
import functools

import jax
import jax.numpy as jnp
from jax import lax
from jax.experimental import pallas as pl
from jax.experimental.pallas import tpu as pltpu

F32 = jnp.float32
BF16 = jnp.bfloat16
I32 = jnp.int32

EPS = 1e-6
GRID_W = 64
ROPE_BASE = 10000.0
ATTN_HEADS = 8
ATTN_KV_HEADS = 2
HEAD_DIM = 128
RET_HEADS = 8
RET_DK = 128
RET_DV = 256
RET_CHUNK = 128
PEER_HEADS = 8
PEER_N_KEYS = 128
PEER_D_HALF = 128
PEER_TOPK = 16

LANES = 128
SUBLANES = 8
VMEM_LIMIT = 60 * 1024 * 1024

ATTN_Q_W = ATTN_HEADS * HEAD_DIM
ATTN_KV_W = ATTN_KV_HEADS * HEAD_DIM
RET_QK_W = RET_HEADS * RET_DK
RET_V_W = RET_HEADS * RET_DV
D_MODEL = 2048
_IN_SPLITS = (("qa", ATTN_Q_W), ("ka", ATTN_KV_W), ("va", ATTN_KV_W), ("qr", RET_QK_W), ("kr", RET_QK_W),
              ("vr", RET_V_W), ("gr", RET_V_W), ("ga", D_MODEL), ("gt", D_MODEL))
OFF = {}
for _name, _width in _IN_SPLITS:
    OFF[_name] = sum(w for n, w in _IN_SPLITS[:len(OFF)])
GATE_BLOCK = 512


def _cparams(*sem):
    return pltpu.CompilerParams(dimension_semantics=sem, vmem_limit_bytes=VMEM_LIMIT)


def _sigmoid(x):
    return 1.0 / (1.0 + jnp.exp(-x))


def _silu(x):
    return x * _sigmoid(x)


def _rot_half(x):
    return pltpu.roll(x, HEAD_DIM // 2, 1)


def _dot_nt(a, b):
    return lax.dot_general(a, b, (((1,), (1,)), ((), ())), preferred_element_type=F32)


def _dot_tn(a, b):
    return lax.dot_general(a, b, (((0,), (0,)), ((), ())), preferred_element_type=F32)


def _ada_kernel(c_ref, w_ref, b_ref, o_ref):
    sc = _silu(c_ref[...]).astype(BF16)
    o_ref[...] = jnp.dot(sc, w_ref[...].astype(BF16), preferred_element_type=F32) + b_ref[...]


def _ada(c_rows, ada_w, ada_b):
    rows, d = c_rows.shape
    n = ada_w.shape[1]
    tn = 1536
    return pl.pallas_call(
        _ada_kernel,
        grid=(n // tn,),
        in_specs=[pl.BlockSpec((rows, d), lambda j: (0, 0)),
                  pl.BlockSpec((d, tn), lambda j: (0, j)),
                  pl.BlockSpec((1, tn), lambda j: (0, j))],
        out_specs=pl.BlockSpec((rows, tn), lambda j: (0, j)),
        out_shape=jax.ShapeDtypeStruct((rows, n), F32),
        compiler_params=_cparams("arbitrary"),
        name="ada",
    )(c_rows, ada_w, ada_b.reshape(1, n))


def _in_proj_kernel(x_ref, xn_ref, nw_ref, shift_ref, scale_ref, shiftn_ref, scalen_ref, w_ref, o_ref, h_ref):
    t, j = pl.program_id(0), pl.program_id(1)
    nj = pl.num_programs(1)
    slot = t % 2

    def prepare(x, shift, scale, dst):
        y = x * lax.rsqrt(jnp.mean(x * x, axis=-1, keepdims=True) + EPS) * nw_ref[...]
        h_ref[dst] = (y * (1.0 + scale) + shift).astype(BF16)

    def project():
        o_ref[...] = jnp.dot(h_ref[slot], w_ref[...], preferred_element_type=F32).astype(o_ref.dtype)

    @pl.when((t == 0) & (j == 0))
    def _():
        prepare(x_ref[...], shift_ref[...], scale_ref[...], 0)

    @pl.when(j < nj - 1)
    def _():
        project()

    @pl.when(j == nj - 1)
    def _():
        prepare(xn_ref[...], shiftn_ref[...], scalen_ref[...], 1 - slot)
        project()


def _in_proj(x, norm_w, shift, scale, w, tm, tn):
    b, l, d = x.shape
    n = w.shape[1]
    per_batch = shift.shape[0] > 1
    tiles_per_batch = l // tm
    n_tiles = b * tiles_per_batch
    nxt = lambda t: jnp.minimum(t + 1, n_tiles - 1)
    row = lambda t, j: (0, 0)
    row_next = lambda t, j: (nxt(t), 0)
    mod = lambda t, j: (0, 0, 0)
    mod_next = (lambda t, j: (nxt(t) // tiles_per_batch, 0, 0)) if per_batch else (lambda t, j: (0, 0, 0))
    x2 = x.reshape(b * l, d)
    out = pl.pallas_call(
        _in_proj_kernel,
        grid=(n_tiles, n // tn),
        in_specs=[pl.BlockSpec((tm, d), row),
                  pl.BlockSpec((tm, d), row_next),
                  pl.BlockSpec((1, d), lambda t, j: (0, 0)),
                  pl.BlockSpec((None, 1, d), mod),
                  pl.BlockSpec((None, 1, d), mod),
                  pl.BlockSpec((None, 1, d), mod_next),
                  pl.BlockSpec((None, 1, d), mod_next),
                  pl.BlockSpec((d, tn), lambda t, j: (0, j))],
        out_specs=pl.BlockSpec((tm, tn), lambda t, j: (t, j)),
        out_shape=jax.ShapeDtypeStruct((b * l, n), BF16),
        scratch_shapes=[pltpu.VMEM((2, tm, d), BF16)],
        compiler_params=_cparams("arbitrary", "arbitrary"),
        name="in_proj",
    )(x2, x2, norm_w.reshape(1, d), shift, scale, shift, scale, w)
    return out.reshape(b, l, n)


def _head_rms(x, w):
    return x * lax.rsqrt(jnp.mean(x * x, axis=-1, keepdims=True) + EPS) * w


def _attn_kernel(q_ref, kc_ref, k_ref, vc_ref, v_ref, cq_ref, sq_ref, ck_ref, sk_ref, qnw_ref, knw_ref,
                 o_ref, k_s, v_s, *, lc, group, tq_sub):
    n_kv = k_s.shape[0]

    @pl.when(pl.program_id(1) == 0)
    def _():
        knw = knw_ref[...]
        for kv in range(n_kv):
            hs = slice(kv * HEAD_DIM, (kv + 1) * HEAD_DIM)
            k_s[kv, 0:lc, :] = _head_rms(kc_ref[:, hs].astype(F32), knw).astype(BF16)
            kn = _head_rms(k_ref[:, hs].astype(F32), knw)
            k_s[kv, lc:, :] = (kn * ck_ref[...] + _rot_half(kn) * sk_ref[...]).astype(BF16)
            v_s[kv, 0:lc, 0:HEAD_DIM] = vc_ref[:, hs]
            v_s[kv, lc:, 0:HEAD_DIM] = v_ref[:, hs]
            v_s[kv, :, HEAD_DIM:] = jnp.ones((v_s.shape[1], HEAD_DIM), BF16)

    c = (HEAD_DIM ** -0.5) * 1.4426950408889634
    qnw = qnw_ref[...]
    tq = tq_sub

    def scores(unit):
        r0, g = unit
        rows = slice(r0, r0 + tq)
        qn = _head_rms(q_ref[rows, g * HEAD_DIM:(g + 1) * HEAD_DIM].astype(F32), qnw)
        qr = (qn * (cq_ref[rows, :] * c) + _rot_half(qn) * (sq_ref[rows, :] * c)).astype(BF16)
        return _dot_nt(qr, k_s[g // group])

    def output(unit, s):
        r0, g = unit
        m = jnp.max(s, axis=-1, keepdims=True)
        p = jnp.exp2((s - m).astype(BF16))
        ol = jnp.dot(p, v_s[g // group], preferred_element_type=F32)
        o_ref[r0:r0 + tq, g * HEAD_DIM:(g + 1) * HEAD_DIM] = (
            ol[:, :HEAD_DIM] * (1.0 / ol[:, HEAD_DIM:HEAD_DIM + 1])).astype(o_ref.dtype)

    units = [(r0, g) for r0 in range(0, q_ref.shape[0], tq) for g in range(n_kv * group)]
    s_next = scores(units[0])
    for idx, unit in enumerate(units):
        s = s_next
        if idx + 1 < len(units):
            s_next = scores(units[idx + 1])
        output(unit, s)


def _attention(p, pc, rope_c, rope_s, q_norm_w, k_norm_w, tq):
    b, l, _ = p.shape
    lc = pc.shape[1]
    group = ATTN_HEADS // ATTN_KV_HEADS
    kq0 = OFF["qa"] // ATTN_Q_W
    kk0 = OFF["ka"] // ATTN_KV_W
    kv0 = OFF["va"] // ATTN_KV_W
    full = lambda bi, i: (0, 0)
    return pl.pallas_call(
        functools.partial(_attn_kernel, lc=lc, group=group, tq_sub=min(256, tq)),
        grid=(b, l // tq),
        in_specs=[pl.BlockSpec((None, tq, ATTN_Q_W), lambda bi, i: (bi, i, kq0)),
                  pl.BlockSpec((None, lc, ATTN_KV_W), lambda bi, i: (bi, 0, kk0)),
                  pl.BlockSpec((None, l, ATTN_KV_W), lambda bi, i: (bi, 0, kk0)),
                  pl.BlockSpec((None, lc, ATTN_KV_W), lambda bi, i: (bi, 0, kv0)),
                  pl.BlockSpec((None, l, ATTN_KV_W), lambda bi, i: (bi, 0, kv0)),
                  pl.BlockSpec((tq, HEAD_DIM), lambda bi, i: (i, 0)),
                  pl.BlockSpec((tq, HEAD_DIM), lambda bi, i: (i, 0)),
                  pl.BlockSpec((l, HEAD_DIM), full),
                  pl.BlockSpec((l, HEAD_DIM), full),
                  pl.BlockSpec((1, HEAD_DIM), full),
                  pl.BlockSpec((1, HEAD_DIM), full)],
        out_specs=pl.BlockSpec((None, tq, ATTN_Q_W), lambda bi, i: (bi, i, 0)),
        out_shape=jax.ShapeDtypeStruct((b, l, ATTN_Q_W), BF16),
        scratch_shapes=[pltpu.VMEM((ATTN_KV_HEADS, lc + l, HEAD_DIM), BF16),
                        pltpu.VMEM((ATTN_KV_HEADS, lc + l, 2 * HEAD_DIM), BF16)],
        compiler_params=_cparams("parallel", "arbitrary"),
        name="attention",
    )(p, pc, p, pc, p, rope_c, rope_s, rope_c, rope_s, q_norm_w.reshape(1, HEAD_DIM), k_norm_w.reshape(1, HEAD_DIM))


def _ret_kernel(q_ref, k_ref, v_ref, g_ref, kc_ref, vc_ref, cos_ref, sin_ref, z_ref, nw_ref, o_ref,
                q_s, k_s, acc, st_f, st_b, *, nc, lc):
    ch = RET_CHUNK
    assert nc % 2 == 0
    scale = RET_DK ** -0.5
    lg_f = -jnp.exp(z_ref[0])
    lg_b = -jnp.exp(z_ref[1])
    row = lax.broadcasted_iota(I32, (ch, ch), 0).astype(F32)
    col = lax.broadcasted_iota(I32, (ch, ch), 1).astype(F32)
    d = row - col
    intra_f = jnp.where(d >= 0, jnp.exp(lg_f * jnp.maximum(d, 0.0)), 0.0)
    intra_b = jnp.where(d <= 0, jnp.exp(lg_b * jnp.maximum(-d, 0.0)), 0.0)

    def wide(a):
        return jnp.concatenate([a] * (RET_DV // LANES), axis=1)

    qd_f = wide(jnp.exp(lg_f * (row + 1.0)))
    kd_f = jnp.exp(lg_f * (ch - 1.0 - row))
    cd_f = wide(jnp.exp(lg_f * float(ch)))
    qd_b = wide(jnp.exp(lg_b * (ch - row)))
    kd_b = jnp.exp(lg_b * row)
    cd_b = wide(jnp.exp(lg_b * float(ch)))

    j = lax.broadcasted_iota(I32, (lc, RET_DK), 0).astype(F32)
    kc = kc_ref[...].astype(F32) * scale
    vc = vc_ref[...]
    st_f[...] = _dot_tn((kc * jnp.exp(lg_f * (lc - 1.0 - j))).astype(BF16), vc)
    st_b[...] = _dot_tn((kc * jnp.exp(lg_b * j)).astype(BF16), vc)

    nw = nw_ref[...]
    fwd = (st_f, intra_f, qd_f, kd_f, cd_f)
    bwd = (st_b, intra_b, qd_b, kd_b, cd_b)

    span = 4 if (nc // 2) % 4 == 0 else (2 if (nc // 2) % 2 == 0 else 1)

    def block(b0, first_visit):
        visits = []
        for u in range(span):
            i = b0 * span + u
            visits.append((pl.multiple_of(i * ch, ch), fwd))
            visits.append((pl.multiple_of((nc - 1 - i) * ch, ch), bwd))
        loaded = []
        for r0, _ in visits:
            rows = pl.ds(r0, ch)
            if first_visit:
                cs, sn = cos_ref[rows, :], sin_ref[rows, :]
                qf, kf = q_ref[rows, :].astype(F32), k_ref[rows, :].astype(F32)
                qb = (qf * cs + _rot_half(qf) * sn).astype(BF16)
                kr = (kf * cs + _rot_half(kf) * sn) * scale
                extra = None
            else:
                qb, kr = q_s[rows, :], k_s[rows, :]
                extra = (acc[rows, :], g_ref[rows, :])
            loaded.append((qb, kr, v_ref[rows, :], extra))
        raw = [_dot_nt(qb, kr.astype(BF16)) for qb, kr, _, _ in loaded]
        upd = [_dot_tn((kr * direction[3]).astype(BF16), vb)
               for (_, direction), (_, kr, vb, _) in zip(visits, loaded)]
        state = {id(fwd): st_f[...], id(bwd): st_b[...]}
        before = []
        for (_, direction), u_ in zip(visits, upd):
            before.append(state[id(direction)])
            state[id(direction)] = state[id(direction)] * direction[4] + u_
        cross = [jnp.dot(qb, s0.astype(BF16), preferred_element_type=F32)
                 for (qb, _, _, _), s0 in zip(loaded, before)]
        outs = []
        for (r0, direction), (qb, kr, vb, extra), s_raw, cr in zip(visits, loaded, raw, cross):
            _, intra, qd, kd, cd = direction
            o = jnp.dot((s_raw * intra).astype(BF16), vb, preferred_element_type=F32) + cr * qd
            if not first_visit:
                o = o + extra[0]
                mu = jnp.mean(o, axis=-1, keepdims=True)
                oc = o - mu
                var = jnp.mean(oc * oc, axis=-1, keepdims=True)
                o = (_silu(extra[1].astype(F32)) * (oc * lax.rsqrt(var + EPS) * nw)).astype(o_ref.dtype)
            outs.append(o)
        for (r0, _), (qb, kr, _, _), o in zip(visits, loaded, outs):
            rows = pl.ds(r0, ch)
            if first_visit:
                q_s[rows, :] = qb
                k_s[rows, :] = kr
                acc[rows, :] = o
            else:
                o_ref[rows, :] = o
        st_f[...] = state[id(fwd)]
        st_b[...] = state[id(bwd)]

    n_blocks = (nc // 2) // span

    def first_half(b0, carry):
        block(b0, True)
        return carry

    def second_half(b0, carry):
        block(b0, False)
        return carry

    lax.fori_loop(0, n_blocks, first_half, 0)
    lax.fori_loop(n_blocks, 2 * n_blocks, second_half, 0)


def _retention(p, pc, rope_c, rope_s, decay_fwd, decay_bwd, ret_norm_w):
    b, l, _ = p.shape
    lc = pc.shape[1]
    nc = l // RET_CHUNK
    z = jnp.broadcast_to(jnp.stack([decay_fwd, decay_bwd])[:, :, None, None], (2, RET_HEADS, 1, LANES))
    kq0, kk0 = OFF["qr"] // RET_DK, OFF["kr"] // RET_DK
    kv0, kg0 = OFF["vr"] // RET_DV, OFF["gr"] // RET_DV
    full = lambda bi, h: (0, 0)
    return pl.pallas_call(
        functools.partial(_ret_kernel, nc=nc, lc=lc),
        grid=(b, RET_HEADS),
        in_specs=[pl.BlockSpec((None, l, RET_DK), lambda bi, h: (bi, 0, kq0 + h)),
                  pl.BlockSpec((None, l, RET_DK), lambda bi, h: (bi, 0, kk0 + h)),
                  pl.BlockSpec((None, l, RET_DV), lambda bi, h: (bi, 0, kv0 + h)),
                  pl.BlockSpec((None, l, RET_DV), lambda bi, h: (bi, 0, kg0 + h)),
                  pl.BlockSpec((None, lc, RET_DK), lambda bi, h: (bi, 0, kk0 + h)),
                  pl.BlockSpec((None, lc, RET_DV), lambda bi, h: (bi, 0, kv0 + h)),
                  pl.BlockSpec((l, RET_DK), full),
                  pl.BlockSpec((l, RET_DK), full),
                  pl.BlockSpec((2, None, 1, LANES), lambda bi, h: (0, h, 0, 0)),
                  pl.BlockSpec((1, RET_DV), lambda bi, h: (0, h))],
        out_specs=pl.BlockSpec((None, l, RET_DV), lambda bi, h: (bi, 0, h)),
        out_shape=jax.ShapeDtypeStruct((b, l, RET_V_W), BF16),
        scratch_shapes=[pltpu.VMEM((l, RET_DK), BF16), pltpu.VMEM((l, RET_DK), F32), pltpu.VMEM((l, RET_DV), F32),
                        pltpu.VMEM((RET_DK, RET_DV), F32), pltpu.VMEM((RET_DK, RET_DV), F32)],
        compiler_params=_cparams("parallel", "parallel"),
        name="retention",
    )(p, p, p, p, pc, pc, rope_c, rope_s, z, ret_norm_w.reshape(1, RET_V_W))


def _merge_kernel(attn_ref, ret_ref, *refs, n_gate_blocks):
    gate_blocks = refs[:2 * n_gate_blocks]
    x_ref, gate_ref, shift_ref, scale_ref, nw_ref, wa_ref, wr_ref, wm_ref, x1_ref, h2_ref = refs[2 * n_gate_blocks:]
    ga = jnp.concatenate([g[...] for g in gate_blocks[:n_gate_blocks]], axis=1)
    gr = jnp.concatenate([g[...] for g in gate_blocks[n_gate_blocks:]], axis=1)
    a = jnp.dot(attn_ref[...], wa_ref[...], preferred_element_type=F32)
    r = jnp.dot(ret_ref[...], wr_ref[...], preferred_element_type=F32)
    mixed = _sigmoid(ga.astype(F32)) * a + _sigmoid(gr.astype(F32)) * r
    y = jnp.dot(mixed.astype(BF16), wm_ref[...], preferred_element_type=F32)
    x1 = x_ref[...] + gate_ref[...] * y
    x1_ref[...] = x1
    hn = x1 * lax.rsqrt(jnp.mean(x1 * x1, axis=-1, keepdims=True) + EPS) * nw_ref[...]
    h2_ref[...] = (hn * (1.0 + scale_ref[...]) + shift_ref[...]).astype(BF16)


def _merge(attn, ret, p, x, gate, shift2, scale2, norm2_w, wa, wr, wm, tm):
    b, l, d = x.shape
    assert d == D_MODEL
    n_gate_blocks = d // GATE_BLOCK
    gate_block0 = OFF["ga"] // GATE_BLOCK
    assert OFF["ga"] % GATE_BLOCK == 0 and OFF["gt"] == OFF["ga"] + d
    row = lambda bi, i: (bi, i, 0)
    mod = lambda bi, i: (bi, 0, 0)
    const = lambda bi, i: (0, 0)
    once = pl.Buffered(1)
    gate_specs = [pl.BlockSpec((None, tm, GATE_BLOCK), functools.partial(lambda kb, bi, i: (bi, i, kb), gate_block0 + kb))
                  for kb in range(2 * n_gate_blocks)]
    return pl.pallas_call(
        functools.partial(_merge_kernel, n_gate_blocks=n_gate_blocks),
        grid=(b, l // tm),
        in_specs=[pl.BlockSpec((None, tm, ATTN_Q_W), row),
                  pl.BlockSpec((None, tm, RET_V_W), row),
                  *gate_specs,
                  pl.BlockSpec((None, tm, d), row),
                  pl.BlockSpec((None, 1, d), mod),
                  pl.BlockSpec((None, 1, d), mod),
                  pl.BlockSpec((None, 1, d), mod),
                  pl.BlockSpec((1, d), const),
                  pl.BlockSpec(wa.shape, const, pipeline_mode=once),
                  pl.BlockSpec(wr.shape, const, pipeline_mode=once),
                  pl.BlockSpec(wm.shape, const, pipeline_mode=once)],
        out_specs=[pl.BlockSpec((None, tm, d), row), pl.BlockSpec((None, tm, d), row)],
        out_shape=[jax.ShapeDtypeStruct((b, l, d), F32), jax.ShapeDtypeStruct((b, l, d), BF16)],
        compiler_params=_cparams("parallel", "parallel"),
        name="merge",
    )(attn, ret, *([p] * (2 * n_gate_blocks)), x, gate, shift2, scale2, norm2_w.reshape(1, d), wa, wr, wm)


def _run(gen):
    try:
        while True:
            next(gen)
    except StopIteration as stop:
        return stop.value


def _top_rows_gen(s, k):
    n = s.shape[0]
    rid = lax.broadcasted_iota(I32, s.shape, 0)
    vals, ids = [], []
    for _ in range(k):
        m = jnp.max(s, axis=0, keepdims=True)
        sel = jnp.min(jnp.where(s == m, rid, n), axis=0, keepdims=True)
        vals.append(m)
        ids.append(sel)
        s = jnp.where(rid == sel, -jnp.inf, s)
        yield
    return jnp.concatenate(vals, axis=0), jnp.concatenate(ids, axis=0)


def _top_rows(s, k):
    return _run(_top_rows_gen(s, k))


def _pair_candidates(s1, s2):
    k = PEER_TOPK
    t = s1.shape[1]
    r8 = lax.broadcasted_iota(I32, (SUBLANES, t), 0)
    r16 = lax.broadcasted_iota(I32, (k, t), 0)
    neg = -jnp.inf
    sums = [s1[0:1] + s2]
    flat = [r16]
    for a, nb in ((1, 8), (2, 5), (3, 4)):
        sums.append(jnp.where(r8 < nb, s1[a:a + 1] + s2[0:8], neg))
        flat.append(a * k + r8)
    sums.append(s1[8:16] + s2[0:1])
    flat.append((r8 + 8) * k)
    for b_, lo, hi in ((0, 4, 8), (1, 4, 8), (2, 4, 5)):
        sums.append(jnp.where((r8 >= lo) & (r8 < hi), s1[0:8] + s2[b_:b_ + 1], neg))
        flat.append(r8 * k + b_)
    return jnp.concatenate(sums, axis=0), jnp.concatenate(flat, axis=0)


def _pair_topk_gen(s1, n1, s2, n2):
    k = PEER_TOPK
    cand, flat = _pair_candidates(s1, s2)
    tops, picks = [], []
    for _ in range(k):
        m = jnp.max(cand, axis=0, keepdims=True)
        f = jnp.min(jnp.where(cand == m, flat, k * k), axis=0, keepdims=True)
        cand = jnp.where(flat == f, -jnp.inf, cand)
        tops.append(m)
        picks.append(f)
        yield
    top = jnp.concatenate(tops, axis=0)
    pick = jnp.concatenate(picks, axis=0)
    pa = pick // k
    pb = pick - pa * k
    e1 = jnp.zeros_like(pick)
    e2 = jnp.zeros_like(pick)
    for a in range(k):
        e1 = jnp.where(pa == a, n1[a:a + 1], e1)
        e2 = jnp.where(pb == a, n2[a:a + 1], e2)
    ex = jnp.exp(top - top[0:1])
    return e1, e2, ex / jnp.sum(ex, axis=0, keepdims=True)


def _pair_topk(s1, n1, s2, n2):
    return _run(_pair_topk_gen(s1, n1, s2, n2))


def _peer_q_kernel(h_ref, wq_ref, q_ref):
    q_ref[...] = jnp.dot(h_ref[...], wq_ref[...], preferred_element_type=F32).astype(q_ref.dtype)


def _peer_q(h2, wq, tm):
    n, d = h2.shape
    return pl.pallas_call(
        _peer_q_kernel,
        grid=(n // tm,),
        in_specs=[pl.BlockSpec((tm, d), lambda i: (i, 0)),
                  pl.BlockSpec(wq.shape, lambda i: (0, 0), pipeline_mode=pl.Buffered(1))],
        out_specs=pl.BlockSpec((tm, wq.shape[1]), lambda i: (i, 0)),
        out_shape=jax.ShapeDtypeStruct((n, wq.shape[1]), BF16),
        compiler_params=_cparams("parallel"),
        name="peer_q",
    )(h2, wq)


def _peer_topk_kernel(q_s, keys_ref, i1_ref, i2_ref, g_ref, i1_s, i2_s, g_s):
    k = PEER_TOPK

    def head(h, carry):
        c0 = pl.multiple_of(h * 2 * PEER_D_HALF, 2 * PEER_D_HALF)
        qh = q_s[:, pl.ds(c0, 2 * PEER_D_HALF)]
        s1, n1 = _top_rows(_dot_nt(keys_ref[2 * h], qh[:, :PEER_D_HALF]), k)
        s2, n2 = _top_rows(_dot_nt(keys_ref[2 * h + 1], qh[:, PEER_D_HALF:]), k)
        e1, e2, gates = _pair_topk(s1, n1, s2, n2)
        r0 = pl.multiple_of(h * k, k)
        g_s[pl.ds(r0, k), :] = gates
        i1_s[pl.ds(r0, k), :] = e1
        i2_s[pl.ds(r0, k), :] = e2
        return carry

    lax.fori_loop(0, PEER_HEADS, head, 0)
    i1_ref[...] = i1_s[...].T
    i2_ref[...] = i2_s[...].T
    g_ref[...] = g_s[...].T


def _peer_topk(q, keys, n, tm):
    slots = PEER_HEADS * PEER_TOPK
    row = lambda i: (i, 0)
    out = jax.ShapeDtypeStruct
    return pl.pallas_call(
        _peer_topk_kernel,
        grid=(n // tm,),
        in_specs=[pl.BlockSpec((tm, q.shape[1]), row),
                  pl.BlockSpec(keys.shape, lambda i: (0, 0, 0))],
        out_specs=[pl.BlockSpec((tm, slots), row)] * 3,
        out_shape=[out((n, slots), I32), out((n, slots), I32), out((n, slots), F32)],
        scratch_shapes=[pltpu.VMEM((slots, tm), I32), pltpu.VMEM((slots, tm), I32), pltpu.VMEM((slots, tm), F32)],
        compiler_params=_cparams("parallel"),
        name="peer_topk",
    )(q, keys)


def _peer_w_kernel(i1_ref, i2_ref, g_ref, w_ref, w3):
    nk = PEER_N_KEYS
    tb = i1_ref.shape[0]
    grp = 2 * SUBLANES
    rid = lax.broadcasted_iota(I32, (nk, i1_ref.shape[1]), 0)
    sub = lax.broadcasted_iota(I32, (SUBLANES, nk), 0)

    def rows_to_tokens(v):
        v = list(v)
        for dist in (4, 2, 1):
            keep = (sub & dist) == 0
            for k in range(SUBLANES):
                if k & dist == 0:
                    a, b_ = v[k], v[k + dist]
                    v[k] = jnp.where(keep, a, pltpu.roll(b_, dist, 0))
                    v[k + dist] = jnp.where(keep, pltpu.roll(a, SUBLANES - dist, 0), b_)
        return v

    def group(gi, carry):
        t0 = pl.multiple_of(gi * grp, grp)
        i1 = i1_ref[pl.ds(t0, grp), :]
        i2 = i2_ref[pl.ds(t0, grp), :]
        g = g_ref[pl.ds(t0, grp), :]
        for k in range(grp):
            left = jnp.where(rid == i1[k:k + 1], g[k:k + 1], 0.0).astype(BF16)
            right = jnp.where(rid == i2[k:k + 1], 1.0, 0.0).astype(BF16)
            w3[k * nk:(k + 1) * nk, :] = _dot_nt(left, right)
        for j in range(nk // SUBLANES):
            halves = []
            for half in range(grp // SUBLANES):
                tiles = [w3[(half * SUBLANES + k) * nk + j * SUBLANES:(half * SUBLANES + k) * nk + (j + 1) * SUBLANES, :]
                         for k in range(SUBLANES)]
                halves.append(rows_to_tokens(tiles))
            for i in range(SUBLANES):
                r = j * SUBLANES + i
                tile = jnp.concatenate([h[i] for h in halves], axis=0)
                w_ref[pl.ds(t0, grp), r * nk:(r + 1) * nk] = tile.astype(w_ref.dtype)
        return carry

    lax.fori_loop(0, tb // grp, group, 0, unroll=4)


def _peer_w(i1, i2, g, tb):
    n, slots = i1.shape
    ne = PEER_N_KEYS * PEER_N_KEYS
    row = lambda i: (i, 0)
    return pl.pallas_call(
        _peer_w_kernel,
        grid=(n // tb,),
        in_specs=[pl.BlockSpec((tb, slots), row)] * 3,
        out_specs=pl.BlockSpec((tb, ne), row),
        out_shape=jax.ShapeDtypeStruct((n, ne), BF16),
        scratch_shapes=[pltpu.VMEM((2 * SUBLANES * PEER_N_KEYS, PEER_N_KEYS), F32)],
        compiler_params=_cparams("parallel"),
        name="peer_w",
    )(i1, i2, g)


def _gelu_tanh(x):
    return 0.5 * x * (1.0 + jnp.tanh(0.7978845608028654 * (x + 0.044715 * (x * x * x))))


def _peer_dense_kernel(*refs, retrieve_next):
    if retrieve_next:
        (h_ref, u_ref, v_ref, w_ref, x_ref, gate_ref, nw_ref, qn_ref, keys_ref,
         o_ref, i1_ref, i2_ref, g_ref, acc, a_buf, s1_s, n1_s, i1_s, i2_s, g_s) = refs
    else:
        h_ref, u_ref, v_ref, w_ref, x_ref, gate_ref, nw_ref, o_ref, acc, a_buf = refs
    j = pl.program_id(1)
    nj = pl.num_programs(1) - 1
    k = PEER_TOPK

    def retrieval_chains(half):
        if not retrieve_next:
            return []
        r0 = pl.multiple_of((j // 2) * k, k)
        groups = range(0, qn_ref.shape[0], LANES)

        def chain(t0):
            tl = slice(t0, t0 + LANES)
            s, n = yield from _top_rows_gen(_dot_nt(keys_ref[...], qn_ref[tl, :]), k)
            if half == 0:
                s1_s[:, tl] = s
                n1_s[:, tl] = n
            else:
                e1, e2, gates = yield from _pair_topk_gen(s1_s[:, tl], n1_s[:, tl], s, n)
                i1_s[pl.ds(r0, k), tl] = e1
                i2_s[pl.ds(r0, k), tl] = e2
                g_s[pl.ds(r0, k), tl] = gates

        return [chain(t0) for t0 in groups]

    def interleave(main, chains):
        roomy = [idx for idx, (_, has_room) in enumerate(main) if has_room]
        start = {}
        for ci, chain in enumerate(chains):
            start.setdefault(roomy[ci * len(roomy) // len(chains)], []).append(chain)
        for idx, (piece, _) in enumerate(main):
            for chain in start.get(idx, ()):
                _run(chain)
            piece()

    def score_pieces(slot):
        half_n = a_buf.shape[2] // 2

        def piece(c0):
            a_buf[slot, :, c0:c0 + half_n] = _dot_nt(h_ref[...], u_ref[c0:c0 + half_n, :])
        return [(functools.partial(piece, c0), True) for c0 in (0, half_n)]

    def finish_pieces(slot):
        quarter = acc.shape[1] // 4
        cell = {}

        def piece(c0):
            if "act" not in cell:
                cell["act"] = (_gelu_tanh(a_buf[slot]) * w_ref[...].astype(F32)).astype(BF16)
            acc[:, c0:c0 + quarter] += jnp.dot(cell["act"], v_ref[:, c0:c0 + quarter], preferred_element_type=F32)
        return [(functools.partial(piece, c0), c0 != 0) for c0 in range(0, acc.shape[1], quarter)]

    @pl.when(j == 0)
    def _():
        acc[...] = jnp.zeros_like(acc)
        interleave(score_pieces(0), retrieval_chains(0))

    for parity in range(2):
        @pl.when((j > 0) & (j < nj) & (j % 2 == parity))
        def _():
            interleave(score_pieces(parity) + finish_pieces(1 - parity), retrieval_chains(parity))

    @pl.when(j == nj)
    def _():
        interleave(finish_pieces((nj - 1) % 2), [])
        x2 = x_ref[...] + gate_ref[...] * acc[...]
        o_ref[...] = x2 * lax.rsqrt(jnp.mean(x2 * x2, axis=-1, keepdims=True) + EPS) * nw_ref[...]
        if retrieve_next:
            i1_ref[...] = i1_s[...].T
            i2_ref[...] = i2_s[...].T
            g_ref[...] = g_s[...].T


def _peer_dense(h2, u, v, w, x1, gate, norm_f_w, seq, tm, te, tile0, n_tiles, retrieval=None):
    n, d = h2.shape
    ne = u.shape[0]
    nj = ne // te
    assert nj % 2 == 0
    tiles_per_batch = seq // tm
    row = lambda i, j: (tile0 + i, 0)
    scored = lambda i, j: (jnp.minimum(j, nj - 1), 0)
    finished = lambda j: jnp.maximum(j - 1, 0)
    in_specs = [pl.BlockSpec((tm, d), row),
                pl.BlockSpec((te, d), scored),
                pl.BlockSpec((te, d), lambda i, j: (finished(j), 0)),
                pl.BlockSpec((tm, te), lambda i, j: (i, finished(j))),
                pl.BlockSpec((tm, d), row),
                pl.BlockSpec((None, 1, d), lambda i, j: ((tile0 + i) // tiles_per_batch, 0, 0)),
                pl.BlockSpec((1, d), lambda i, j: (0, 0))]
    args = [h2, u, v, w, x1, gate, norm_f_w.reshape(1, d)]
    out_specs = [pl.BlockSpec((tm, d), row)]
    out_shape = [jax.ShapeDtypeStruct((n, d), F32)]
    scratch = [pltpu.VMEM((tm, d), F32), pltpu.VMEM((2, tm, te), F32)]
    if retrieval is not None:
        q, keys = retrieval
        slots = PEER_HEADS * PEER_TOPK
        assert nj == keys.shape[0]
        half_head = lambda j: jnp.minimum(j, nj - 1)
        in_specs += [pl.BlockSpec((tm, PEER_D_HALF), lambda i, j: (tile0 + n_tiles + i, half_head(j))),
                     pl.BlockSpec((None, PEER_N_KEYS, PEER_D_HALF), lambda i, j: (half_head(j), 0, 0))]
        args += [q, keys]
        out_specs += [pl.BlockSpec((tm, slots), lambda i, j: (i, 0))] * 3
        out_shape += [jax.ShapeDtypeStruct((n_tiles * tm, slots), t) for t in (I32, I32, F32)]
        scratch += [pltpu.VMEM((PEER_TOPK, tm), F32), pltpu.VMEM((PEER_TOPK, tm), I32),
                    pltpu.VMEM((slots, tm), I32), pltpu.VMEM((slots, tm), I32), pltpu.VMEM((slots, tm), F32)]
    outs = pl.pallas_call(
        functools.partial(_peer_dense_kernel, retrieve_next=retrieval is not None),
        grid=(n_tiles, nj + 1),
        in_specs=in_specs,
        out_specs=out_specs,
        out_shape=out_shape,
        scratch_shapes=scratch,
        input_output_aliases={4: 0},
        compiler_params=_cparams("parallel", "arbitrary"),
        name="peer_dense",
    )(*args)
    return outs[0], tuple(outs[1:])


def _rope_tables(length):
    t = jnp.arange(length, dtype=jnp.int32)
    row = (t // GRID_W).astype(F32)
    col = (t % GRID_W).astype(F32)
    n_freq = HEAD_DIM // 4
    inv_freq = ROPE_BASE ** (-jnp.arange(n_freq, dtype=F32) / n_freq)
    ang = jnp.concatenate([row[:, None] * inv_freq, col[:, None] * inv_freq], axis=-1)
    cos, sin = jnp.cos(ang), jnp.sin(ang)
    return jnp.concatenate([cos, cos], axis=-1), jnp.concatenate([-sin, sin], axis=-1)


def _layer(x, ctx, c_rows, rope_c, rope_s, ada_w, ada_b, norm1_w, w_in, q_norm_w, k_norm_w, ret_decay_fwd,
           ret_decay_bwd, ret_norm_w, w_attn_branch, w_ret_branch, w_merge_out, norm2_w, peer_w_q, peer_keys,
           peer_u, peer_v, norm_f_w):
    b, l, d = x.shape
    lc = ctx.shape[1]
    mod = _ada(c_rows, ada_w, ada_b)
    mod_x = [mod[:b, i * d:(i + 1) * d].reshape(b, 1, d) for i in range(6)]
    mod_c = [mod[b:b + 1, i * d:(i + 1) * d].reshape(1, 1, d) for i in range(2)]

    w_in_b = w_in.astype(BF16)
    n_in = w_in.shape[1]
    tn = n_in // 4
    p = _in_proj(x, norm1_w, mod_x[0], mod_x[1], w_in_b, tm=min(512, l), tn=tn)
    pc = _in_proj(ctx, norm1_w, mod_c[0], mod_c[1], w_in_b, tm=min(256, lc), tn=tn)

    attn = _attention(p, pc, rope_c, rope_s, q_norm_w, k_norm_w, tq=min(1024, l))
    ret = _retention(p, pc, rope_c, rope_s, ret_decay_fwd, ret_decay_bwd, ret_norm_w)
    x1, h2 = _merge(attn, ret, p, x, mod_x[2], mod_x[3], mod_x[4], norm2_w, w_attn_branch.astype(BF16),
                    w_ret_branch.astype(BF16), w_merge_out.astype(BF16), tm=min(256, l))

    n = b * l
    h2f = h2.reshape(n, d)
    keys = peer_keys.reshape(PEER_HEADS * 2, PEER_N_KEYS, PEER_D_HALF).astype(BF16)
    q = _peer_q(h2f, peer_w_q.astype(BF16), tm=min(1024, l))
    u, v = peer_u.astype(BF16), peer_v.astype(BF16)
    tm = min(512, l)
    te = 1024
    n_tiles = n // tm
    n_chunks = next(c for c in (8, 4, 2, 1) if n_tiles % c == 0)
    tiles = n_tiles // n_chunks
    sel = _peer_topk(q, keys, n=tiles * tm, tm=256)
    out = x1.reshape(n, d)
    for chunk in range(n_chunks):
        w = _peer_w(*sel, tb=128)
        retrieval = (q, keys) if chunk + 1 < n_chunks else None
        out, sel = _peer_dense(h2f, u, v, w, out, mod_x[5], norm_f_w, seq=l, tm=tm, te=te,
                               tile0=chunk * tiles, n_tiles=tiles, retrieval=retrieval)
    return out.reshape(b, l, d)


def kernel(x, c, ctx, c_ctx, ada_w, ada_b, norm1_w, w_in, q_norm_w, k_norm_w, ret_decay_fwd, ret_decay_bwd,
           ret_norm_w, w_attn_branch, w_ret_branch, w_merge_out, norm2_w, peer_w_q, peer_keys, peer_u, peer_v,
           norm_f_w):
    depth = ada_w.shape[0]
    assert depth == 1, "context-stream update between layers is not implemented"
    b, l, d = x.shape
    rows = -(-(b + 1) // SUBLANES) * SUBLANES
    c_rows = jnp.zeros((rows, d), F32).at[:b].set(c).at[b].set(c_ctx)
    rope_c, rope_s = _rope_tables(l)
    return _layer(x, ctx, c_rows, rope_c, rope_s, ada_w[0], ada_b[0], norm1_w[0], w_in[0], q_norm_w[0], k_norm_w[0],
                  ret_decay_fwd[0], ret_decay_bwd[0], ret_norm_w[0], w_attn_branch[0], w_ret_branch[0],
                  w_merge_out[0], norm2_w[0], peer_w_q[0], peer_keys[0], peer_u[0], peer_v[0], norm_f_w)
```

```python
import functools

import jax
import jax.numpy as jnp
from jax import lax
from jax.experimental import pallas as pl
from jax.experimental.pallas import tpu as pltpu

F32 = jnp.float32
BF16 = jnp.bfloat16
I32 = jnp.int32

EPS = 1e-6
GRID_W = 64
ROPE_BASE = 10000.0
ATTN_HEADS = 8
ATTN_KV_HEADS = 2
HEAD_DIM = 128
RET_HEADS = 8
RET_DK = 128
RET_DV = 256
RET_CHUNK = 128
PEER_HEADS = 8
PEER_N_KEYS = 128
PEER_D_HALF = 128
PEER_TOPK = 16

LANES = 128
SUBLANES = 8
VMEM_LIMIT = 60 * 1024 * 1024

ATTN_Q_W = ATTN_HEADS * HEAD_DIM
ATTN_KV_W = ATTN_KV_HEADS * HEAD_DIM
RET_QK_W = RET_HEADS * RET_DK
RET_V_W = RET_HEADS * RET_DV
D_MODEL = 2048
_IN_SPLITS = (("qa", ATTN_Q_W), ("ka", ATTN_KV_W), ("va", ATTN_KV_W), ("qr", RET_QK_W), ("kr", RET_QK_W),
              ("vr", RET_V_W), ("gr", RET_V_W), ("ga", D_MODEL), ("gt", D_MODEL))
OFF = {}
for _name, _width in _IN_SPLITS:
    OFF[_name] = sum(w for n, w in _IN_SPLITS[:len(OFF)])
GATE_BLOCK = 512


def _cparams(*sem):
    return pltpu.CompilerParams(dimension_semantics=sem, vmem_limit_bytes=VMEM_LIMIT)


def _sigmoid(x):
    return 1.0 / (1.0 + jnp.exp(-x))


def _silu(x):
    return x * _sigmoid(x)


def _rot_half(x):
    return pltpu.roll(x, HEAD_DIM // 2, 1)


def _dot_nt(a, b):
    return lax.dot_general(a, b, (((1,), (1,)), ((), ())), preferred_element_type=F32)


def _dot_tn(a, b):
    return lax.dot_general(a, b, (((0,), (0,)), ((), ())), preferred_element_type=F32)


def _ada_kernel(c_ref, w_ref, b_ref, o_ref):
    sc = _silu(c_ref[...]).astype(BF16)
    o_ref[...] = jnp.dot(sc, w_ref[...].astype(BF16), preferred_element_type=F32) + b_ref[...]


def _ada(c_rows, ada_w, ada_b):
    rows, d = c_rows.shape
    n = ada_w.shape[1]
    tn = 1536
    return pl.pallas_call(
        _ada_kernel,
        grid=(n // tn,),
        in_specs=[pl.BlockSpec((rows, d), lambda j: (0, 0)),
                  pl.BlockSpec((d, tn), lambda j: (0, j)),
                  pl.BlockSpec((1, tn), lambda j: (0, j))],
        out_specs=pl.BlockSpec((rows, tn), lambda j: (0, j)),
        out_shape=jax.ShapeDtypeStruct((rows, n), F32),
        compiler_params=_cparams("arbitrary"),
        name="ada",
    )(c_rows, ada_w, ada_b.reshape(1, n))


def _in_proj_kernel(x_ref, xn_ref, nw_ref, shift_ref, scale_ref, shiftn_ref, scalen_ref, w_ref, o_ref, h_ref):
    t, j = pl.program_id(0), pl.program_id(1)
    nj = pl.num_programs(1)
    slot = t % 2

    def prepare(x, shift, scale, dst):
        y = x * lax.rsqrt(jnp.mean(x * x, axis=-1, keepdims=True) + EPS) * nw_ref[...]
        h_ref[dst] = (y * (1.0 + scale) + shift).astype(BF16)

    def project():
        o_ref[...] = jnp.dot(h_ref[slot], w_ref[...], preferred_element_type=F32).astype(o_ref.dtype)

    @pl.when((t == 0) & (j == 0))
    def _():
        prepare(x_ref[...], shift_ref[...], scale_ref[...], 0)

    @pl.when(j < nj - 1)
    def _():
        project()

    @pl.when(j == nj - 1)
    def _():
        prepare(xn_ref[...], shiftn_ref[...], scalen_ref[...], 1 - slot)
        project()


def _in_proj(x, norm_w, shift, scale, w, tm, tn):
    b, l, d = x.shape
    n = w.shape[1]
    per_batch = shift.shape[0] > 1
    tiles_per_batch = l // tm
    n_tiles = b * tiles_per_batch
    nxt = lambda t: jnp.minimum(t + 1, n_tiles - 1)
    row = lambda t, j: (0, 0)
    row_next = lambda t, j: (nxt(t), 0)
    mod = lambda t, j: (0, 0, 0)
    mod_next = (lambda t, j: (nxt(t) // tiles_per_batch, 0, 0)) if per_batch else (lambda t, j: (0, 0, 0))
    x2 = x.reshape(b * l, d)
    out = pl.pallas_call(
        _in_proj_kernel,
        grid=(n_tiles, n // tn),
        in_specs=[pl.BlockSpec((tm, d), row),
                  pl.BlockSpec((tm, d), row_next),
                  pl.BlockSpec((1, d), lambda t, j: (0, 0)),
                  pl.BlockSpec((None, 1, d), mod),
                  pl.BlockSpec((None, 1, d), mod),
                  pl.BlockSpec((None, 1, d), mod_next),
                  pl.BlockSpec((None, 1, d), mod_next),
                  pl.BlockSpec((d, tn), lambda t, j: (0, j))],
        out_specs=pl.BlockSpec((tm, tn), lambda t, j: (t, j)),
        out_shape=jax.ShapeDtypeStruct((b * l, n), BF16),
        scratch_shapes=[pltpu.VMEM((2, tm, d), BF16)],
        compiler_params=_cparams("arbitrary", "arbitrary"),
        name="in_proj",
    )(x2, x2, norm_w.reshape(1, d), shift, scale, shift, scale, w)
    return out.reshape(b, l, n)


def _head_rms(x, w):
    return x * lax.rsqrt(jnp.mean(x * x, axis=-1, keepdims=True) + EPS) * w


def _attn_kernel(q_ref, kc_ref, k_ref, vc_ref, v_ref, cq_ref, sq_ref, ck_ref, sk_ref, qnw_ref, knw_ref,
                 o_ref, k_s, v_s, *, lc, group, tq_sub):
    n_kv = k_s.shape[0]

    @pl.when(pl.program_id(1) == 0)
    def _():
        knw = knw_ref[...]
        for kv in range(n_kv):
            hs = slice(kv * HEAD_DIM, (kv + 1) * HEAD_DIM)
            k_s[kv, 0:lc, :] = _head_rms(kc_ref[:, hs].astype(F32), knw).astype(BF16)
            kn = _head_rms(k_ref[:, hs].astype(F32), knw)
            k_s[kv, lc:, :] = (kn * ck_ref[...] + _rot_half(kn) * sk_ref[...]).astype(BF16)
            v_s[kv, 0:lc, 0:HEAD_DIM] = vc_ref[:, hs]
            v_s[kv, lc:, 0:HEAD_DIM] = v_ref[:, hs]
            v_s[kv, :, HEAD_DIM:] = jnp.ones((v_s.shape[1], HEAD_DIM), BF16)

    c = (HEAD_DIM ** -0.5) * 1.4426950408889634
    qnw = qnw_ref[...]
    tq = tq_sub

    def scores(unit):
        r0, g = unit
        rows = slice(r0, r0 + tq)
        qn = _head_rms(q_ref[rows, g * HEAD_DIM:(g + 1) * HEAD_DIM].astype(F32), qnw)
        qr = (qn * (cq_ref[rows, :] * c) + _rot_half(qn) * (sq_ref[rows, :] * c)).astype(BF16)
        return _dot_nt(qr, k_s[g // group])

    def output(unit, s):
        r0, g = unit
        m = jnp.max(s, axis=-1, keepdims=True)
        p = jnp.exp2((s - m).astype(BF16))
        ol = jnp.dot(p, v_s[g // group], preferred_element_type=F32)
        o_ref[r0:r0 + tq, g * HEAD_DIM:(g + 1) * HEAD_DIM] = (
            ol[:, :HEAD_DIM] * (1.0 / ol[:, HEAD_DIM:HEAD_DIM + 1])).astype(o_ref.dtype)

    units = [(r0, g) for r0 in range(0, q_ref.shape[0], tq) for g in range(n_kv * group)]
    s_next = scores(units[0])
    for idx, unit in enumerate(units):
        s = s_next
        if idx + 1 < len(units):
            s_next = scores(units[idx + 1])
        output(unit, s)


def _attention(p, pc, rope_c, rope_s, q_norm_w, k_norm_w, tq):
    b, l, _ = p.shape
    lc = pc.shape[1]
    group = ATTN_HEADS // ATTN_KV_HEADS
    kq0 = OFF["qa"] // ATTN_Q_W
    kk0 = OFF["ka"] // ATTN_KV_W
    kv0 = OFF["va"] // ATTN_KV_W
    full = lambda bi, i: (0, 0)
    return pl.pallas_call(
        functools.partial(_attn_kernel, lc=lc, group=group, tq_sub=min(256, tq)),
        grid=(b, l // tq),
        in_specs=[pl.BlockSpec((None, tq, ATTN_Q_W), lambda bi, i: (bi, i, kq0)),
                  pl.BlockSpec((None, lc, ATTN_KV_W), lambda bi, i: (bi, 0, kk0)),
                  pl.BlockSpec((None, l, ATTN_KV_W), lambda bi, i: (bi, 0, kk0)),
                  pl.BlockSpec((None, lc, ATTN_KV_W), lambda bi, i: (bi, 0, kv0)),
                  pl.BlockSpec((None, l, ATTN_KV_W), lambda bi, i: (bi, 0, kv0)),
                  pl.BlockSpec((tq, HEAD_DIM), lambda bi, i: (i, 0)),
                  pl.BlockSpec((tq, HEAD_DIM), lambda bi, i: (i, 0)),
                  pl.BlockSpec((l, HEAD_DIM), full),
                  pl.BlockSpec((l, HEAD_DIM), full),
                  pl.BlockSpec((1, HEAD_DIM), full),
                  pl.BlockSpec((1, HEAD_DIM), full)],
        out_specs=pl.BlockSpec((None, tq, ATTN_Q_W), lambda bi, i: (bi, i, 0)),
        out_shape=jax.ShapeDtypeStruct((b, l, ATTN_Q_W), BF16),
        scratch_shapes=[pltpu.VMEM((ATTN_KV_HEADS, lc + l, HEAD_DIM), BF16),
                        pltpu.VMEM((ATTN_KV_HEADS, lc + l, 2 * HEAD_DIM), BF16)],
        compiler_params=_cparams("parallel", "arbitrary"),
        name="attention",
    )(p, pc, p, pc, p, rope_c, rope_s, rope_c, rope_s, q_norm_w.reshape(1, HEAD_DIM), k_norm_w.reshape(1, HEAD_DIM))


def _ret_kernel(q_ref, k_ref, v_ref, g_ref, kc_ref, vc_ref, cos_ref, sin_ref, z_ref, nw_ref, o_ref,
                q_s, k_s, acc, st_f, st_b, *, nc, lc):
    ch = RET_CHUNK
    assert nc % 2 == 0
    scale = RET_DK ** -0.5
    lg_f = -jnp.exp(z_ref[0])
    lg_b = -jnp.exp(z_ref[1])
    row = lax.broadcasted_iota(I32, (ch, ch), 0).astype(F32)
    col = lax.broadcasted_iota(I32, (ch, ch), 1).astype(F32)
    d = row - col
    intra_f = jnp.where(d >= 0, jnp.exp(lg_f * jnp.maximum(d, 0.0)), 0.0)
    intra_b = jnp.where(d <= 0, jnp.exp(lg_b * jnp.maximum(-d, 0.0)), 0.0)

    def wide(a):
        return jnp.concatenate([a] * (RET_DV // LANES), axis=1)

    qd_f = wide(jnp.exp(lg_f * (row + 1.0)))
    kd_f = jnp.exp(lg_f * (ch - 1.0 - row))
    cd_f = wide(jnp.exp(lg_f * float(ch)))
    qd_b = wide(jnp.exp(lg_b * (ch - row)))
    kd_b = jnp.exp(lg_b * row)
    cd_b = wide(jnp.exp(lg_b * float(ch)))

    j = lax.broadcasted_iota(I32, (lc, RET_DK), 0).astype(F32)
    kc = kc_ref[...].astype(F32) * scale
    vc = vc_ref[...]
    st_f[...] = _dot_tn((kc * jnp.exp(lg_f * (lc - 1.0 - j))).astype(BF16), vc)
    st_b[...] = _dot_tn((kc * jnp.exp(lg_b * j)).astype(BF16), vc)

    nw = nw_ref[...]
    fwd = (st_f, intra_f, qd_f, kd_f, cd_f)
    bwd = (st_b, intra_b, qd_b, kd_b, cd_b)

    span = 4 if (nc // 2) % 4 == 0 else (2 if (nc // 2) % 2 == 0 else 1)

    def block(b0, first_visit):
        visits = []
        for u in range(span):
            i = b0 * span + u
            visits.append((pl.multiple_of(i * ch, ch), fwd))
            visits.append((pl.multiple_of((nc - 1 - i) * ch, ch), bwd))
        loaded = []
        for r0, _ in visits:
            rows = pl.ds(r0, ch)
            if first_visit:
                cs, sn = cos_ref[rows, :], sin_ref[rows, :]
                qf, kf = q_ref[rows, :].astype(F32), k_ref[rows, :].astype(F32)
                qb = (qf * cs + _rot_half(qf) * sn).astype(BF16)
                kr = (kf * cs + _rot_half(kf) * sn) * scale
                extra = None
            else:
                qb, kr = q_s[rows, :], k_s[rows, :]
                extra = (acc[rows, :], g_ref[rows, :])
            loaded.append((qb, kr, v_ref[rows, :], extra))
        raw = [_dot_nt(qb, kr.astype(BF16)) for qb, kr, _, _ in loaded]
        upd = [_dot_tn((kr * direction[3]).astype(BF16), vb)
               for (_, direction), (_, kr, vb, _) in zip(visits, loaded)]
        state = {id(fwd): st_f[...], id(bwd): st_b[...]}
        before = []
        for (_, direction), u_ in zip(visits, upd):
            before.append(state[id(direction)])
            state[id(direction)] = state[id(direction)] * direction[4] + u_
        cross = [jnp.dot(qb, s0.astype(BF16), preferred_element_type=F32)
                 for (qb, _, _, _), s0 in zip(loaded, before)]
        outs = []
        for (r0, direction), (qb, kr, vb, extra), s_raw, cr in zip(visits, loaded, raw, cross):
            _, intra, qd, kd, cd = direction
            o = jnp.dot((s_raw * intra).astype(BF16), vb, preferred_element_type=F32) + cr * qd
            if not first_visit:
                o = o + extra[0]
                mu = jnp.mean(o, axis=-1, keepdims=True)
                oc = o - mu
                var = jnp.mean(oc * oc, axis=-1, keepdims=True)
                o = (_silu(extra[1].astype(F32)) * (oc * lax.rsqrt(var + EPS) * nw)).astype(o_ref.dtype)
            outs.append(o)
        for (r0, _), (qb, kr, _, _), o in zip(visits, loaded, outs):
            rows = pl.ds(r0, ch)
            if first_visit:
                q_s[rows, :] = qb
                k_s[rows, :] = kr
                acc[rows, :] = o
            else:
                o_ref[rows, :] = o
        st_f[...] = state[id(fwd)]
        st_b[...] = state[id(bwd)]

    n_blocks = (nc // 2) // span

    def first_half(b0, carry):
        block(b0, True)
        return carry

    def second_half(b0, carry):
        block(b0, False)
        return carry

    lax.fori_loop(0, n_blocks, first_half, 0)
    lax.fori_loop(n_blocks, 2 * n_blocks, second_half, 0)


def _retention(p, pc, rope_c, rope_s, decay_fwd, decay_bwd, ret_norm_w):
    b, l, _ = p.shape
    lc = pc.shape[1]
    nc = l // RET_CHUNK
    z = jnp.broadcast_to(jnp.stack([decay_fwd, decay_bwd])[:, :, None, None], (2, RET_HEADS, 1, LANES))
    kq0, kk0 = OFF["qr"] // RET_DK, OFF["kr"] // RET_DK
    kv0, kg0 = OFF["vr"] // RET_DV, OFF["gr"] // RET_DV
    full = lambda bi, h: (0, 0)
    return pl.pallas_call(
        functools.partial(_ret_kernel, nc=nc, lc=lc),
        grid=(b, RET_HEADS),
        in_specs=[pl.BlockSpec((None, l, RET_DK), lambda bi, h: (bi, 0, kq0 + h)),
                  pl.BlockSpec((None, l, RET_DK), lambda bi, h: (bi, 0, kk0 + h)),
                  pl.BlockSpec((None, l, RET_DV), lambda bi, h: (bi, 0, kv0 + h)),
                  pl.BlockSpec((None, l, RET_DV), lambda bi, h: (bi, 0, kg0 + h)),
                  pl.BlockSpec((None, lc, RET_DK), lambda bi, h: (bi, 0, kk0 + h)),
                  pl.BlockSpec((None, lc, RET_DV), lambda bi, h: (bi, 0, kv0 + h)),
                  pl.BlockSpec((l, RET_DK), full),
                  pl.BlockSpec((l, RET_DK), full),
                  pl.BlockSpec((2, None, 1, LANES), lambda bi, h: (0, h, 0, 0)),
                  pl.BlockSpec((1, RET_DV), lambda bi, h: (0, h))],
        out_specs=pl.BlockSpec((None, l, RET_DV), lambda bi, h: (bi, 0, h)),
        out_shape=jax.ShapeDtypeStruct((b, l, RET_V_W), BF16),
        scratch_shapes=[pltpu.VMEM((l, RET_DK), BF16), pltpu.VMEM((l, RET_DK), F32), pltpu.VMEM((l, RET_DV), F32),
                        pltpu.VMEM((RET_DK, RET_DV), F32), pltpu.VMEM((RET_DK, RET_DV), F32)],
        compiler_params=_cparams("parallel", "parallel"),
        name="retention",
    )(p, p, p, p, pc, pc, rope_c, rope_s, z, ret_norm_w.reshape(1, RET_V_W))


def _merge_kernel(attn_ref, ret_ref, *refs, n_gate_blocks):
    gate_blocks = refs[:2 * n_gate_blocks]
    x_ref, gate_ref, shift_ref, scale_ref, nw_ref, wa_ref, wr_ref, wm_ref, x1_ref, h2_ref = refs[2 * n_gate_blocks:]
    ga = jnp.concatenate([g[...] for g in gate_blocks[:n_gate_blocks]], axis=1)
    gr = jnp.concatenate([g[...] for g in gate_blocks[n_gate_blocks:]], axis=1)
    a = jnp.dot(attn_ref[...], wa_ref[...], preferred_element_type=F32)
    r = jnp.dot(ret_ref[...], wr_ref[...], preferred_element_type=F32)
    mixed = _sigmoid(ga.astype(F32)) * a + _sigmoid(gr.astype(F32)) * r
    y = jnp.dot(mixed.astype(BF16), wm_ref[...], preferred_element_type=F32)
    x1 = x_ref[...] + gate_ref[...] * y
    x1_ref[...] = x1
    hn = x1 * lax.rsqrt(jnp.mean(x1 * x1, axis=-1, keepdims=True) + EPS) * nw_ref[...]
    h2_ref[...] = (hn * (1.0 + scale_ref[...]) + shift_ref[...]).astype(BF16)


def _merge(attn, ret, p, x, gate, shift2, scale2, norm2_w, wa, wr, wm, tm):
    b, l, d = x.shape
    assert d == D_MODEL
    n_gate_blocks = d // GATE_BLOCK
    gate_block0 = OFF["ga"] // GATE_BLOCK
    assert OFF["ga"] % GATE_BLOCK == 0 and OFF["gt"] == OFF["ga"] + d
    row = lambda bi, i: (bi, i, 0)
    mod = lambda bi, i: (bi, 0, 0)
    const = lambda bi, i: (0, 0)
    once = pl.Buffered(1)
    gate_specs = [pl.BlockSpec((None, tm, GATE_BLOCK), functools.partial(lambda kb, bi, i: (bi, i, kb), gate_block0 + kb))
                  for kb in range(2 * n_gate_blocks)]
    return pl.pallas_call(
        functools.partial(_merge_kernel, n_gate_blocks=n_gate_blocks),
        grid=(b, l // tm),
        in_specs=[pl.BlockSpec((None, tm, ATTN_Q_W), row),
                  pl.BlockSpec((None, tm, RET_V_W), row),
                  *gate_specs,
                  pl.BlockSpec((None, tm, d), row),
                  pl.BlockSpec((None, 1, d), mod),
                  pl.BlockSpec((None, 1, d), mod),
                  pl.BlockSpec((None, 1, d), mod),
                  pl.BlockSpec((1, d), const),
                  pl.BlockSpec(wa.shape, const, pipeline_mode=once),
                  pl.BlockSpec(wr.shape, const, pipeline_mode=once),
                  pl.BlockSpec(wm.shape, const, pipeline_mode=once)],
        out_specs=[pl.BlockSpec((None, tm, d), row), pl.BlockSpec((None, tm, d), row)],
        out_shape=[jax.ShapeDtypeStruct((b, l, d), F32), jax.ShapeDtypeStruct((b, l, d), BF16)],
        compiler_params=_cparams("parallel", "parallel"),
        name="merge",
    )(attn, ret, *([p] * (2 * n_gate_blocks)), x, gate, shift2, scale2, norm2_w.reshape(1, d), wa, wr, wm)


def _run(gen):
    try:
        while True:
            next(gen)
    except StopIteration as stop:
        return stop.value


def _top_rows_gen(s, k):
    n = s.shape[0]
    rid = lax.broadcasted_iota(I32, s.shape, 0)
    vals, ids = [], []
    for _ in range(k):
        m = jnp.max(s, axis=0, keepdims=True)
        sel = jnp.min(jnp.where(s == m, rid, n), axis=0, keepdims=True)
        vals.append(m)
        ids.append(sel)
        s = jnp.where(rid == sel, -jnp.inf, s)
        yield
    return jnp.concatenate(vals, axis=0), jnp.concatenate(ids, axis=0)


def _top_rows(s, k):
    return _run(_top_rows_gen(s, k))


def _pair_candidates(s1, s2):
    k = PEER_TOPK
    t = s1.shape[1]
    r8 = lax.broadcasted_iota(I32, (SUBLANES, t), 0)
    r16 = lax.broadcasted_iota(I32, (k, t), 0)
    neg = -jnp.inf
    sums = [s1[0:1] + s2]
    flat = [r16]
    for a, nb in ((1, 8), (2, 5), (3, 4)):
        sums.append(jnp.where(r8 < nb, s1[a:a + 1] + s2[0:8], neg))
        flat.append(a * k + r8)
    sums.append(s1[8:16] + s2[0:1])
    flat.append((r8 + 8) * k)
    for b_, lo, hi in ((0, 4, 8), (1, 4, 8), (2, 4, 5)):
        sums.append(jnp.where((r8 >= lo) & (r8 < hi), s1[0:8] + s2[b_:b_ + 1], neg))
        flat.append(r8 * k + b_)
    return jnp.concatenate(sums, axis=0), jnp.concatenate(flat, axis=0)


def _pair_topk_gen(s1, n1, s2, n2):
    k = PEER_TOPK
    cand, flat = _pair_candidates(s1, s2)
    tops, picks = [], []
    for _ in range(k):
        m = jnp.max(cand, axis=0, keepdims=True)
        f = jnp.min(jnp.where(cand == m, flat, k * k), axis=0, keepdims=True)
        cand = jnp.where(flat == f, -jnp.inf, cand)
        tops.append(m)
        picks.append(f)
        yield
    top = jnp.concatenate(tops, axis=0)
    pick = jnp.concatenate(picks, axis=0)
    pa = pick // k
    pb = pick - pa * k
    e1 = jnp.zeros_like(pick)
    e2 = jnp.zeros_like(pick)
    for a in range(k):
        e1 = jnp.where(pa == a, n1[a:a + 1], e1)
        e2 = jnp.where(pb == a, n2[a:a + 1], e2)
    ex = jnp.exp(top - top[0:1])
    return e1, e2, ex / jnp.sum(ex, axis=0, keepdims=True)


def _pair_topk(s1, n1, s2, n2):
    return _run(_pair_topk_gen(s1, n1, s2, n2))


def _peer_q_kernel(h_ref, wq_ref, q_ref):
    q_ref[...] = jnp.dot(h_ref[...], wq_ref[...], preferred_element_type=F32).astype(q_ref.dtype)


def _peer_q(h2, wq, tm):
    n, d = h2.shape
    return pl.pallas_call(
        _peer_q_kernel,
        grid=(n // tm,),
        in_specs=[pl.BlockSpec((tm, d), lambda i: (i, 0)),
                  pl.BlockSpec(wq.shape, lambda i: (0, 0), pipeline_mode=pl.Buffered(1))],
        out_specs=pl.BlockSpec((tm, wq.shape[1]), lambda i: (i, 0)),
        out_shape=jax.ShapeDtypeStruct((n, wq.shape[1]), BF16),
        compiler_params=_cparams("parallel"),
        name="peer_q",
    )(h2, wq)


def _peer_topk_kernel(q_s, keys_ref, i1_ref, i2_ref, g_ref, i1_s, i2_s, g_s):
    k = PEER_TOPK

    def head(h, carry):
        c0 = pl.multiple_of(h * 2 * PEER_D_HALF, 2 * PEER_D_HALF)
        qh = q_s[:, pl.ds(c0, 2 * PEER_D_HALF)]
        s1, n1 = _top_rows(_dot_nt(keys_ref[2 * h], qh[:, :PEER_D_HALF]), k)
        s2, n2 = _top_rows(_dot_nt(keys_ref[2 * h + 1], qh[:, PEER_D_HALF:]), k)
        e1, e2, gates = _pair_topk(s1, n1, s2, n2)
        r0 = pl.multiple_of(h * k, k)
        g_s[pl.ds(r0, k), :] = gates
        i1_s[pl.ds(r0, k), :] = e1
        i2_s[pl.ds(r0, k), :] = e2
        return carry

    lax.fori_loop(0, PEER_HEADS, head, 0)
    i1_ref[...] = i1_s[...].T
    i2_ref[...] = i2_s[...].T
    g_ref[...] = g_s[...].T


def _peer_topk(q, keys, n, tm):
    slots = PEER_HEADS * PEER_TOPK
    row = lambda i: (i, 0)
    out = jax.ShapeDtypeStruct
    return pl.pallas_call(
        _peer_topk_kernel,
        grid=(n // tm,),
        in_specs=[pl.BlockSpec((tm, q.shape[1]), row),
                  pl.BlockSpec(keys.shape, lambda i: (0, 0, 0))],
        out_specs=[pl.BlockSpec((tm, slots), row)] * 3,
        out_shape=[out((n, slots), I32), out((n, slots), I32), out((n, slots), F32)],
        scratch_shapes=[pltpu.VMEM((slots, tm), I32), pltpu.VMEM((slots, tm), I32), pltpu.VMEM((slots, tm), F32)],
        compiler_params=_cparams("parallel"),
        name="peer_topk",
    )(q, keys)


def _peer_w_kernel(i1_ref, i2_ref, g_ref, w_ref, w3):
    nk = PEER_N_KEYS
    tb = i1_ref.shape[0]
    grp = 2 * SUBLANES
    rid = lax.broadcasted_iota(I32, (nk, i1_ref.shape[1]), 0)
    sub = lax.broadcasted_iota(I32, (SUBLANES, nk), 0)

    def rows_to_tokens(v):
        v = list(v)
        for dist in (4, 2, 1):
            keep = (sub & dist) == 0
            for k in range(SUBLANES):
                if k & dist == 0:
                    a, b_ = v[k], v[k + dist]
                    v[k] = jnp.where(keep, a, pltpu.roll(b_, dist, 0))
                    v[k + dist] = jnp.where(keep, pltpu.roll(a, SUBLANES - dist, 0), b_)
        return v

    def group(gi, carry):
        t0 = pl.multiple_of(gi * grp, grp)
        i1 = i1_ref[pl.ds(t0, grp), :]
        i2 = i2_ref[pl.ds(t0, grp), :]
        g = g_ref[pl.ds(t0, grp), :]
        for k in range(grp):
            left = jnp.where(rid == i1[k:k + 1], g[k:k + 1], 0.0).astype(BF16)
            right = jnp.where(rid == i2[k:k + 1], 1.0, 0.0).astype(BF16)
            w3[k * nk:(k + 1) * nk, :] = _dot_nt(left, right)
        for j in range(nk // SUBLANES):
            halves = []
            for half in range(grp // SUBLANES):
                tiles = [w3[(half * SUBLANES + k) * nk + j * SUBLANES:(half * SUBLANES + k) * nk + (j + 1) * SUBLANES, :]
                         for k in range(SUBLANES)]
                halves.append(rows_to_tokens(tiles))
            for i in range(SUBLANES):
                r = j * SUBLANES + i
                tile = jnp.concatenate([h[i] for h in halves], axis=0)
                w_ref[pl.ds(t0, grp), r * nk:(r + 1) * nk] = tile.astype(w_ref.dtype)
        return carry

    lax.fori_loop(0, tb // grp, group, 0, unroll=4)


def _peer_w(i1, i2, g, tb):
    n, slots = i1.shape
    ne = PEER_N_KEYS * PEER_N_KEYS
    row = lambda i: (i, 0)
    return pl.pallas_call(
        _peer_w_kernel,
        grid=(n // tb,),
        in_specs=[pl.BlockSpec((tb, slots), row)] * 3,
        out_specs=pl.BlockSpec((tb, ne), row),
        out_shape=jax.ShapeDtypeStruct((n, ne), BF16),
        scratch_shapes=[pltpu.VMEM((2 * SUBLANES * PEER_N_KEYS, PEER_N_KEYS), F32)],
        compiler_params=_cparams("parallel"),
        name="peer_w",
    )(i1, i2, g)


def _gelu_tanh(x):
    return 0.5 * x * (1.0 + jnp.tanh(0.7978845608028654 * (x + 0.044715 * (x * x * x))))


def _peer_dense_kernel(*refs, retrieve_next):
    if retrieve_next:
        (h_ref, u_ref, v_ref, w_ref, x_ref, gate_ref, nw_ref, qn_ref, keys_ref,
         o_ref, i1_ref, i2_ref, g_ref, acc, a_buf, s1_s, n1_s, i1_s, i2_s, g_s) = refs
    else:
        h_ref, u_ref, v_ref, w_ref, x_ref, gate_ref, nw_ref, o_ref, acc, a_buf = refs
    j = pl.program_id(1)
    nj = pl.num_programs(1) - 1
    k = PEER_TOPK

    def retrieval_chains(half):
        if not retrieve_next:
            return []
        r0 = pl.multiple_of((j // 2) * k, k)
        groups = range(0, qn_ref.shape[0], LANES)

        def chain(t0):
            tl = slice(t0, t0 + LANES)
            s, n = yield from _top_rows_gen(_dot_nt(keys_ref[...], qn_ref[tl, :]), k)
            if half == 0:
                s1_s[:, tl] = s
                n1_s[:, tl] = n
            else:
                e1, e2, gates = yield from _pair_topk_gen(s1_s[:, tl], n1_s[:, tl], s, n)
                i1_s[pl.ds(r0, k), tl] = e1
                i2_s[pl.ds(r0, k), tl] = e2
                g_s[pl.ds(r0, k), tl] = gates

        return [chain(t0) for t0 in groups]

    def interleave(main, chains):
        roomy = [idx for idx, (_, has_room) in enumerate(main) if has_room]
        start = {}
        for ci, chain in enumerate(chains):
            start.setdefault(roomy[ci * len(roomy) // len(chains)], []).append(chain)
        for idx, (piece, _) in enumerate(main):
            for chain in start.get(idx, ()):
                _run(chain)
            piece()

    def score_pieces(slot):
        half_n = a_buf.shape[2] // 2

        def piece(c0):
            a_buf[slot, :, c0:c0 + half_n] = _dot_nt(h_ref[...], u_ref[c0:c0 + half_n, :])
        return [(functools.partial(piece, c0), True) for c0 in (0, half_n)]

    def finish_pieces(slot):
        quarter = acc.shape[1] // 4
        cell = {}

        def piece(c0):
            if "act" not in cell:
                cell["act"] = (_gelu_tanh(a_buf[slot]) * w_ref[...].astype(F32)).astype(BF16)
            acc[:, c0:c0 + quarter] += jnp.dot(cell["act"], v_ref[:, c0:c0 + quarter], preferred_element_type=F32)
        return [(functools.partial(piece, c0), c0 != 0) for c0 in range(0, acc.shape[1], quarter)]

    @pl.when(j == 0)
    def _():
        acc[...] = jnp.zeros_like(acc)
        interleave(score_pieces(0), retrieval_chains(0))

    for parity in range(2):
        @pl.when((j > 0) & (j < nj) & (j % 2 == parity))
        def _():
            interleave(score_pieces(parity) + finish_pieces(1 - parity), retrieval_chains(parity))

    @pl.when(j == nj)
    def _():
        interleave(finish_pieces((nj - 1) % 2), [])
        x2 = x_ref[...] + gate_ref[...] * acc[...]
        o_ref[...] = x2 * lax.rsqrt(jnp.mean(x2 * x2, axis=-1, keepdims=True) + EPS) * nw_ref[...]
        if retrieve_next:
            i1_ref[...] = i1_s[...].T
            i2_ref[...] = i2_s[...].T
            g_ref[...] = g_s[...].T


def _peer_dense(h2, u, v, w, x1, gate, norm_f_w, seq, tm, te, tile0, n_tiles, retrieval=None):
    n, d = h2.shape
    ne = u.shape[0]
    nj = ne // te
    assert nj % 2 == 0
    tiles_per_batch = seq // tm
    row = lambda i, j: (tile0 + i, 0)
    scored = lambda i, j: (jnp.minimum(j, nj - 1), 0)
    finished = lambda j: jnp.maximum(j - 1, 0)
    in_specs = [pl.BlockSpec((tm, d), row),
                pl.BlockSpec((te, d), scored),
                pl.BlockSpec((te, d), lambda i, j: (finished(j), 0)),
                pl.BlockSpec((tm, te), lambda i, j: (i, finished(j))),
                pl.BlockSpec((tm, d), row),
                pl.BlockSpec((None, 1, d), lambda i, j: ((tile0 + i) // tiles_per_batch, 0, 0)),
                pl.BlockSpec((1, d), lambda i, j: (0, 0))]
    args = [h2, u, v, w, x1, gate, norm_f_w.reshape(1, d)]
    out_specs = [pl.BlockSpec((tm, d), row)]
    out_shape = [jax.ShapeDtypeStruct((n, d), F32)]
    scratch = [pltpu.VMEM((tm, d), F32), pltpu.VMEM((2, tm, te), F32)]
    if retrieval is not None:
        q, keys = retrieval
        slots = PEER_HEADS * PEER_TOPK
        assert nj == keys.shape[0]
        half_head = lambda j: jnp.minimum(j, nj - 1)
        in_specs += [pl.BlockSpec((tm, PEER_D_HALF), lambda i, j: (tile0 + n_tiles + i, half_head(j))),
                     pl.BlockSpec((None, PEER_N_KEYS, PEER_D_HALF), lambda i, j: (half_head(j), 0, 0))]
        args += [q, keys]
        out_specs += [pl.BlockSpec((tm, slots), lambda i, j: (i, 0))] * 3
        out_shape += [jax.ShapeDtypeStruct((n_tiles * tm, slots), t) for t in (I32, I32, F32)]
        scratch += [pltpu.VMEM((PEER_TOPK, tm), F32), pltpu.VMEM((PEER_TOPK, tm), I32),
                    pltpu.VMEM((slots, tm), I32), pltpu.VMEM((slots, tm), I32), pltpu.VMEM((slots, tm), F32)]
    outs = pl.pallas_call(
        functools.partial(_peer_dense_kernel, retrieve_next=retrieval is not None),
        grid=(n_tiles, nj + 1),
        in_specs=in_specs,
        out_specs=out_specs,
        out_shape=out_shape,
        scratch_shapes=scratch,
        input_output_aliases={4: 0},
        compiler_params=_cparams("parallel", "arbitrary"),
        name="peer_dense",
    )(*args)
    return outs[0], tuple(outs[1:])


def _rope_tables(length):
    t = jnp.arange(length, dtype=jnp.int32)
    row = (t // GRID_W).astype(F32)
    col = (t % GRID_W).astype(F32)
    n_freq = HEAD_DIM // 4
    inv_freq = ROPE_BASE ** (-jnp.arange(n_freq, dtype=F32) / n_freq)
    ang = jnp.concatenate([row[:, None] * inv_freq, col[:, None] * inv_freq], axis=-1)
    cos, sin = jnp.cos(ang), jnp.sin(ang)
    return jnp.concatenate([cos, cos], axis=-1), jnp.concatenate([-sin, sin], axis=-1)


def _layer(x, ctx, c_rows, rope_c, rope_s, ada_w, ada_b, norm1_w, w_in, q_norm_w, k_norm_w, ret_decay_fwd,
           ret_decay_bwd, ret_norm_w, w_attn_branch, w_ret_branch, w_merge_out, norm2_w, peer_w_q, peer_keys,
           peer_u, peer_v, norm_f_w):
    b, l, d = x.shape
    lc = ctx.shape[1]
    mod = _ada(c_rows, ada_w, ada_b)
    mod_x = [mod[:b, i * d:(i + 1) * d].reshape(b, 1, d) for i in range(6)]
    mod_c = [mod[b:b + 1, i * d:(i + 1) * d].reshape(1, 1, d) for i in range(2)]

    w_in_b = w_in.astype(BF16)
    n_in = w_in.shape[1]
    tn = n_in // 4
    p = _in_proj(x, norm1_w, mod_x[0], mod_x[1], w_in_b, tm=min(512, l), tn=tn)
    pc = _in_proj(ctx, norm1_w, mod_c[0], mod_c[1], w_in_b, tm=min(256, lc), tn=tn)

    attn = _attention(p, pc, rope_c, rope_s, q_norm_w, k_norm_w, tq=min(512, l))
    ret = _retention(p, pc, rope_c, rope_s, ret_decay_fwd, ret_decay_bwd, ret_norm_w)
    x1, h2 = _merge(attn, ret, p, x, mod_x[2], mod_x[3], mod_x[4], norm2_w, w_attn_branch.astype(BF16),
                    w_ret_branch.astype(BF16), w_merge_out.astype(BF16), tm=min(256, l))

    n = b * l
    h2f = h2.reshape(n, d)
    keys = peer_keys.reshape(PEER_HEADS * 2, PEER_N_KEYS, PEER_D_HALF).astype(BF16)
    q = _peer_q(h2f, peer_w_q.astype(BF16), tm=min(1024, l))
    u, v = peer_u.astype(BF16), peer_v.astype(BF16)
    tm = min(512, l)
    te = 1024
    n_tiles = n // tm
    n_chunks = next(c for c in (8, 4, 2, 1) if n_tiles % c == 0)
    tiles = n_tiles // n_chunks
    sel = _peer_topk(q, keys, n=tiles * tm, tm=256)
    out = x1.reshape(n, d)
    for chunk in range(n_chunks):
        w = _peer_w(*sel, tb=128)
        retrieval = (q, keys) if chunk + 1 < n_chunks else None
        out, sel = _peer_dense(h2f, u, v, w, out, mod_x[5], norm_f_w, seq=l, tm=tm, te=te,
                               tile0=chunk * tiles, n_tiles=tiles, retrieval=retrieval)
    return out.reshape(b, l, d)


def kernel(x, c, ctx, c_ctx, ada_w, ada_b, norm1_w, w_in, q_norm_w, k_norm_w, ret_decay_fwd, ret_decay_bwd,
           ret_norm_w, w_attn_branch, w_ret_branch, w_merge_out, norm2_w, peer_w_q, peer_keys, peer_u, peer_v,
           norm_f_w):
    depth = ada_w.shape[0]
    assert depth == 1, "context-stream update between layers is not implemented"
    b, l, d = x.shape
    rows = -(-(b + 1) // SUBLANES) * SUBLANES
    c_rows = jnp.zeros((rows, d), F32).at[:b].set(c).at[b].set(c_ctx)
    rope_c, rope_s = _rope_tables(l)
    return _layer(x, ctx, c_rows, rope_c, rope_s, ada_w[0], ada_b[0], norm1_w[0], w_in[0], q_norm_w[0], k_norm_w[0],
                  ret_decay_fwd[0], ret_decay_bwd[0], ret_norm_w[0], w_attn_branch[0], w_ret_branch[0],
                  w_merge_out[0], norm2_w[0], peer_w_q[0], peer_keys[0], peer_u[0], peer_v[0], norm_f_w)
```
